```python
import jax, jax.numpy as jnp
from jax import lax
import numpy as np

D_MODEL = 2048
BATCH = 2
SEQ = 4096
DEPTH = 4
DEC_BATCH = 32
DEC_SEQ = 4
PAST_LEN = 16384
PAGE_SIZE = 128

N_HEADS = 16
KV_HEADS = 4
GQA_GROUP = N_HEADS // KV_HEADS
HEAD_DIM = 64
ROPE_DIM = HEAD_DIM // 4
ROPE_THETA = 500000.0
WINDOW = 128
BLOCK = 128
ATTN_WIDTH = N_HEADS * HEAD_DIM
KV_WIDTH = KV_HEADS * HEAD_DIM
CONV_WIDTH = D_MODEL // 2
SCONV_K = 3
LRU_WIDTH = D_MODEL // 2
LRU_BLOCKS = 16
LRU_BLOCK_DIM = LRU_WIDTH // LRU_BLOCKS
LRU_CONV_K = 4
LRU_C = 8.0
MEM_LEN = 256
MEM_HEADS = 4
MEM_HEAD_DIM = 256
MEM_WIDTH = MEM_HEADS * MEM_HEAD_DIM
N_BRANCH = 4
BRANCH_WIDTH = D_MODEL // 2
MIX_IN = ATTN_WIDTH + 2 * KV_WIDTH + 3 * CONV_WIDTH + LRU_WIDTH + MEM_WIDTH + N_BRANCH * D_MODEL
D_FF = 5632
N_EXPERTS = 8
TOP_K = 2
N_DENSE = (DEPTH + 1) // 2
N_MOE = DEPTH // 2
ALPHA = (2 * DEPTH) ** 0.25
BETA = (8 * DEPTH) ** -0.25
LN_EPS = 1e-5
NEG_INF = -1e30

kernel_name = "hybrid_gated_parallel_decoder_step"


def layer_norm(x, g, b):
    xf = x.astype(jnp.float32)
    xc = xf - xf.mean(-1, keepdims=True)
    var = (xc * xc).mean(-1, keepdims=True)
    y = xc * lax.rsqrt(var + LN_EPS) * g.astype(jnp.float32) + b.astype(jnp.float32)
    return y.astype(x.dtype)


def partial_rope(x, pos):
    half = ROPE_DIM // 2
    inv_freq = ROPE_THETA ** (-jnp.arange(half, dtype=jnp.float32) * (2.0 / ROPE_DIM))
    ang = pos.astype(jnp.float32)[:, None] * inv_freq[None, :]
    cos = jnp.cos(ang)[:, None, :]
    sin = jnp.sin(ang)[:, None, :]
    xf = x.astype(jnp.float32)
    x1 = xf[..., :half]
    x2 = xf[..., half:ROPE_DIM]
    out = jnp.concatenate([x1 * cos - x2 * sin, x2 * cos + x1 * sin, xf[..., ROPE_DIM:]], axis=-1)
    return out.astype(x.dtype)


def sink_attention(q, k, v, q_pos, k_pos, sinks):
    s = jnp.einsum('bnqkgd,bnskd->bnkgqs', q, k).astype(jnp.float32) * (HEAD_DIM ** -0.5)
    qp = q_pos[:, :, None]
    kp = k_pos[:, None, :]
    mask = (kp <= qp) & (qp - kp < WINDOW) & (kp >= 0)
    s = jnp.where(mask[None, :, None, None], s, NEG_INF)
    sink = sinks.astype(jnp.float32).reshape(KV_HEADS, GQA_GROUP)[None, None, :, :, None, None]
    m = jnp.maximum(s.max(-1, keepdims=True), sink)
    p = jnp.exp(s - m)
    denom = p.sum(-1, keepdims=True) + jnp.exp(sink - m)
    o = jnp.einsum('bnkgqs,bnskd->bnqkgd', p / denom, v.astype(jnp.float32))
    return o.astype(q.dtype)


def banded_window_attention(q, k, v, sinks):
    Bn, S = q.shape[0], q.shape[1]
    nblk = S // BLOCK
    qb = q.reshape(Bn, nblk, BLOCK, KV_HEADS, GQA_GROUP, HEAD_DIM)
    kc = k.reshape(Bn, nblk, BLOCK, KV_HEADS, HEAD_DIM)
    vc = v.reshape(Bn, nblk, BLOCK, KV_HEADS, HEAD_DIM)
    kb = jnp.concatenate([jnp.concatenate([jnp.zeros_like(kc[:, :1]), kc[:, :-1]], axis=1), kc], axis=2)
    vb = jnp.concatenate([jnp.concatenate([jnp.zeros_like(vc[:, :1]), vc[:, :-1]], axis=1), vc], axis=2)
    q_pos = jnp.arange(S, dtype=jnp.int32).reshape(nblk, BLOCK)
    k_pos = jnp.concatenate([q_pos - BLOCK, q_pos], axis=1)
    o = sink_attention(qb, kb, vb, q_pos, k_pos, sinks)
    return o.reshape(Bn, S, ATTN_WIDTH)


def cached_window_attention(q, k, v, buf_k, buf_v, sinks):
    Bn, T = q.shape[0], q.shape[1]
    L = buf_k.shape[1]
    k_all = jnp.concatenate([buf_k.astype(k.dtype), k], axis=1)
    v_all = jnp.concatenate([buf_v.astype(v.dtype), v], axis=1)
    q_pos = PAST_LEN + jnp.arange(T, dtype=jnp.int32)
    k_pos = jnp.concatenate([PAST_LEN - L + jnp.arange(L, dtype=jnp.int32), q_pos])
    o = sink_attention(q.reshape(Bn, 1, T, KV_HEADS, GQA_GROUP, HEAD_DIM), k_all[:, None], v_all[:, None],
                       q_pos[None], k_pos[None], sinks)
    return o.reshape(Bn, T, ATTN_WIDTH), k_all[:, -L:], v_all[:, -L:]


def causal_depthwise_conv(x, prev, w):
    K = w.shape[0]
    T = x.shape[1]
    xp = jnp.concatenate([prev.astype(x.dtype), x], axis=1)
    y = xp[:, 0:T] * w[0]
    for j in range(1, K):
        y = y + xp[:, j:j + T] * w[j]
    return y, xp[:, xp.shape[1] - (K - 1):]


def memory_attention(qm, mem_k, mem_v):
    Bn, T = qm.shape[0], qm.shape[1]
    q = qm.reshape(Bn, T, MEM_HEADS, MEM_HEAD_DIM)
    s = jnp.einsum('bthd,bmhd->bhtm', q, mem_k.astype(q.dtype)).astype(jnp.float32) * (MEM_HEAD_DIM ** -0.5)
    p = jax.nn.softmax(s, axis=-1)
    o = jnp.einsum('bhtm,bmhd->bthd', p, mem_v.astype(jnp.float32))
    return o.astype(qm.dtype).reshape(Bn, T, MEM_WIDTH)


def lru_scan(a, bx, h0):
    def step(h, ab):
        at, bt = ab
        h = at * h + bt
        return h, h
    hT, hs = lax.scan(step, h0, (jnp.swapaxes(a, 0, 1), jnp.swapaxes(bx, 0, 1)))
    return jnp.swapaxes(hs, 0, 1), hT


def rg_lru(xl, prev, h0, w_conv, b_conv, w_a, b_a, w_x, b_x, lam):
    xc, new_prev = causal_depthwise_conv(xl, prev, w_conv)
    xc = xc + b_conv
    Bn, T = xc.shape[0], xc.shape[1]
    xb = xc.reshape(Bn, T, LRU_BLOCKS, LRU_BLOCK_DIM)
    r = jax.nn.sigmoid((jnp.einsum('btnc,ncd->btnd', xb, w_a) + b_a).astype(jnp.float32)).reshape(Bn, T, LRU_WIDTH)
    i = jax.nn.sigmoid((jnp.einsum('btnc,ncd->btnd', xb, w_x) + b_x).astype(jnp.float32)).reshape(Bn, T, LRU_WIDTH)
    log_a = -LRU_C * r * jax.nn.softplus(-lam.astype(jnp.float32))
    a = jnp.exp(log_a)
    mult = jnp.sqrt(-jnp.expm1(2.0 * log_a))
    bx = mult * (i * xc.astype(jnp.float32))
    hs, hT = lru_scan(a, bx, h0.astype(jnp.float32))
    return hs.astype(xl.dtype), new_prev, hT.astype(xl.dtype)


def split_columns(z):
    sizes = [ATTN_WIDTH, KV_WIDTH, KV_WIDTH, CONV_WIDTH, CONV_WIDTH, CONV_WIDTH, LRU_WIDTH, MEM_WIDTH]
    pts, acc = [], 0
    for s in sizes:
        acc += s
        pts.append(acc)
    return jnp.split(z, pts, axis=-1)


def mixing(x, pos, p, mem_k, mem_v, buf_k, buf_v, sconv_prev, lru_prev, lru_h0):
    Bn, T = x.shape[0], x.shape[1]
    z = x @ p['w_mix_in']
    q, k, v, cb, cc, ch, xl, qm, zg = split_columns(z)
    q = partial_rope(q.reshape(Bn, T, N_HEADS, HEAD_DIM), pos)
    k = partial_rope(k.reshape(Bn, T, KV_HEADS, HEAD_DIM), pos)
    v = v.reshape(Bn, T, KV_HEADS, HEAD_DIM)
    if buf_k is None:
        y_attn = banded_window_attention(q, k, v, p['sinks'])
        L = min(WINDOW, T)
        new_k, new_v = k[:, T - L:], v[:, T - L:]
    else:
        y_attn, new_k, new_v = cached_window_attention(q, k, v, buf_k, buf_v, p['sinks'])
    y_c, new_sconv = causal_depthwise_conv(cc * ch, sconv_prev, p['w_sconv'])
    y_conv = cb * y_c
    y_lru, new_lru_prev, new_h = rg_lru(xl, lru_prev, lru_h0, p['w_lru_conv'], p['b_lru_conv'],
                                        p['w_lru_a'], p['b_lru_a'], p['w_lru_x'], p['b_lru_x'], p['lru_lambda'])
    y_mem = memory_attention(qm, mem_k, mem_v)
    ys = jnp.stack([y_attn, y_conv, y_lru, y_mem], axis=2)
    proj = jnp.einsum('btnc,ncd->btnd', ys, p['w_branch'])
    gates = jax.nn.sigmoid(zg.reshape(Bn, T, N_BRANCH, D_MODEL))
    merged = jnp.einsum('btnd,btnd->btd', gates, proj)
    return merged @ p['w_o'], (new_k, new_v, new_sconv, new_lru_prev, new_h)


def swiglu(x, wg, wu, wd):
    return (jax.nn.silu(x @ wg) * (x @ wu)) @ wd


def moe_swiglu(x, w_router, wg, wu, wd):
    logits = (x @ w_router).astype(jnp.float32)
    top_v, top_i = lax.top_k(logits, TOP_K)
    probs = jax.nn.softmax(top_v, axis=-1)
    combine = jnp.sum(jax.nn.one_hot(top_i, N_EXPERTS, dtype=jnp.float32) * probs[..., None], axis=-2).astype(x.dtype)
    out = combine[..., 0:1] * swiglu(x, wg[0], wu[0], wd[0])
    for e in range(1, N_EXPERTS):
        out = out + combine[..., e:e + 1] * swiglu(x, wg[e], wu[e], wd[e])
    return out


def setup_inputs(seed: int = 0) -> dict:
    key = jax.random.key(seed)
    ks = jax.random.split(key, 40)
    f32 = jnp.float32

    def nrm(k, shape, scale):
        return jax.random.normal(k, shape, f32) * scale

    wb = min(WINDOW, PAST_LEN)
    u = jax.random.uniform(ks[20], (DEPTH, LRU_WIDTH), f32, minval=0.9, maxval=0.999)
    a0 = u ** (1.0 / LRU_C)
    lru_lambda = jnp.log(a0) - jnp.log1p(-a0)
    return {
        "x_prompt": nrm(ks[0], (BATCH, SEQ, D_MODEL), 1.0),
        "x_sample": nrm(ks[1], (DEC_BATCH, DEC_SEQ, D_MODEL), 1.0),
        "mem_prompt": nrm(ks[2], (BATCH, MEM_LEN, D_MODEL), 1.0),
        "cache_win_k": nrm(ks[3], (DEPTH, DEC_BATCH, wb, KV_HEADS, HEAD_DIM), 1.0),
        "cache_win_v": nrm(ks[4], (DEPTH, DEC_BATCH, wb, KV_HEADS, HEAD_DIM), 1.0),
        "cache_mem_k": nrm(ks[5], (DEPTH, DEC_BATCH, MEM_LEN, MEM_HEADS, MEM_HEAD_DIM), 1.0),
        "cache_mem_v": nrm(ks[6], (DEPTH, DEC_BATCH, MEM_LEN, MEM_HEADS, MEM_HEAD_DIM), 1.0),
        "state_sconv": nrm(ks[7], (DEPTH, DEC_BATCH, SCONV_K - 1, CONV_WIDTH), 1.0),
        "state_lru_conv": nrm(ks[8], (DEPTH, DEC_BATCH, LRU_CONV_K - 1, LRU_WIDTH), 1.0),
        "state_lru_h": nrm(ks[9], (DEPTH, DEC_BATCH, LRU_WIDTH), 0.5),
        "w_mix_in": nrm(ks[10], (DEPTH, D_MODEL, MIX_IN), D_MODEL ** -0.5),
        "sinks": nrm(ks[11], (DEPTH, N_HEADS), 0.5),
        "w_sconv": nrm(ks[12], (DEPTH, SCONV_K, CONV_WIDTH), SCONV_K ** -0.5),
        "w_lru_conv": nrm(ks[13], (DEPTH, LRU_CONV_K, LRU_WIDTH), LRU_CONV_K ** -0.5),
        "b_lru_conv": nrm(ks[14], (DEPTH, LRU_WIDTH), 0.01),
        "w_lru_a": nrm(ks[15], (DEPTH, LRU_BLOCKS, LRU_BLOCK_DIM, LRU_BLOCK_DIM), LRU_BLOCK_DIM ** -0.5),
        "b_lru_a": nrm(ks[16], (DEPTH, LRU_BLOCKS, LRU_BLOCK_DIM), 0.01),
        "w_lru_x": nrm(ks[17], (DEPTH, LRU_BLOCKS, LRU_BLOCK_DIM, LRU_BLOCK_DIM), LRU_BLOCK_DIM ** -0.5),
        "b_lru_x": nrm(ks[18], (DEPTH, LRU_BLOCKS, LRU_BLOCK_DIM), 0.01),
        "lru_lambda": lru_lambda,
        "w_mem_kv": nrm(ks[21], (DEPTH, D_MODEL, 2 * MEM_WIDTH), D_MODEL ** -0.5),
        "w_branch": nrm(ks[22], (DEPTH, N_BRANCH, BRANCH_WIDTH, D_MODEL), BRANCH_WIDTH ** -0.5),
        "w_o": nrm(ks[23], (DEPTH, D_MODEL, D_MODEL), BETA * D_MODEL ** -0.5),
        "ln_g": 1.0 + nrm(ks[24], (DEPTH, 2, D_MODEL), 0.02),
        "ln_b": nrm(ks[25], (DEPTH, 2, D_MODEL), 0.02),
        "w_router": nrm(ks[26], (N_MOE, D_MODEL, N_EXPERTS), D_MODEL ** -0.5),
        "w_ffn_gate": nrm(ks[27], (N_DENSE, D_MODEL, D_FF), D_MODEL ** -0.5),
        "w_ffn_up": nrm(ks[28], (N_DENSE, D_MODEL, D_FF), D_MODEL ** -0.5),
        "w_ffn_down": nrm(ks[29], (N_DENSE, D_FF, D_MODEL), BETA * D_FF ** -0.5),
        "w_exp_gate": nrm(ks[30], (N_MOE, N_EXPERTS, D_MODEL, D_FF), D_MODEL ** -0.5),
        "w_exp_up": nrm(ks[31], (N_MOE, N_EXPERTS, D_MODEL, D_FF), D_MODEL ** -0.5),
        "w_exp_down": nrm(ks[32], (N_MOE, N_EXPERTS, D_FF, D_MODEL), BETA * D_FF ** -0.5),
    }


def reference(x_prompt, x_sample, mem_prompt, cache_win_k, cache_win_v, cache_mem_k, cache_mem_v,
              state_sconv, state_lru_conv, state_lru_h, w_mix_in, sinks, w_sconv, w_lru_conv, b_lru_conv,
              w_lru_a, b_lru_a, w_lru_x, b_lru_x, lru_lambda, w_mem_kv, w_branch, w_o, ln_g, ln_b,
              w_router, w_ffn_gate, w_ffn_up, w_ffn_down, w_exp_gate, w_exp_up, w_exp_down):
    Bp, S = x_prompt.shape[0], x_prompt.shape[1]
    Ts = x_sample.shape[1]
    dt = x_prompt.dtype
    pos_p = jnp.arange(S, dtype=jnp.int32)
    pos_s = PAST_LEN + jnp.arange(Ts, dtype=jnp.int32)
    zeros_sconv = jnp.zeros((Bp, SCONV_K - 1, CONV_WIDTH), dt)
    zeros_lconv = jnp.zeros((Bp, LRU_CONV_K - 1, LRU_WIDTH), dt)
    zeros_h = jnp.zeros((Bp, LRU_WIDTH), dt)

    xp, xs = x_prompt, x_sample
    wk_p, wv_p, mk_p, mv_p, sc_p, lc_p, lh_p = [], [], [], [], [], [], []
    wk_s, wv_s, sc_s, lc_s, lh_s = [], [], [], [], []
    for l in range(DEPTH):
        p = {"w_mix_in": w_mix_in[l], "sinks": sinks[l], "w_sconv": w_sconv[l], "w_lru_conv": w_lru_conv[l],
             "b_lru_conv": b_lru_conv[l], "w_lru_a": w_lru_a[l], "b_lru_a": b_lru_a[l], "w_lru_x": w_lru_x[l],
             "b_lru_x": b_lru_x[l], "lru_lambda": lru_lambda[l], "w_branch": w_branch[l], "w_o": w_o[l]}
        mkv = mem_prompt @ w_mem_kv[l]
        mem_k = mkv[..., :MEM_WIDTH].reshape(Bp, MEM_LEN, MEM_HEADS, MEM_HEAD_DIM)
        mem_v = mkv[..., MEM_WIDTH:].reshape(Bp, MEM_LEN, MEM_HEADS, MEM_HEAD_DIM)

        mix_p, st_p = mixing(xp, pos_p, p, mem_k, mem_v, None, None, zeros_sconv, zeros_lconv, zeros_h)
        mix_s, st_s = mixing(xs, pos_s, p, cache_mem_k[l], cache_mem_v[l], cache_win_k[l], cache_win_v[l],
                             state_sconv[l], state_lru_conv[l], state_lru_h[l])
        xp = layer_norm(ALPHA * xp + mix_p, ln_g[l, 0], ln_b[l, 0])
        xs = layer_norm(ALPHA * xs + mix_s, ln_g[l, 0], ln_b[l, 0])

        j = l // 2
        if l % 2 == 0:
            f_p = swiglu(xp, w_ffn_gate[j], w_ffn_up[j], w_ffn_down[j])
            f_s = swiglu(xs, w_ffn_gate[j], w_ffn_up[j], w_ffn_down[j])
        else:
            f_p = moe_swiglu(xp, w_router[j], w_exp_gate[j], w_exp_up[j], w_exp_down[j])
            f_s = moe_swiglu(xs, w_router[j], w_exp_gate[j], w_exp_up[j], w_exp_down[j])
        xp = layer_norm(ALPHA * xp + f_p, ln_g[l, 1], ln_b[l, 1])
        xs = layer_norm(ALPHA * xs + f_s, ln_g[l, 1], ln_b[l, 1])

        wk_p.append(st_p[0]); wv_p.append(st_p[1]); mk_p.append(mem_k); mv_p.append(mem_v)
        sc_p.append(st_p[2]); lc_p.append(st_p[3]); lh_p.append(st_p[4])
        wk_s.append(st_s[0]); wv_s.append(st_s[1]); sc_s.append(st_s[2]); lc_s.append(st_s[3]); lh_s.append(st_s[4])

    return (xp, xs,
            jnp.stack(wk_p), jnp.stack(wv_p), jnp.stack(mk_p), jnp.stack(mv_p),
            jnp.stack(sc_p), jnp.stack(lc_p), jnp.stack(lh_p),
            jnp.stack(wk_s), jnp.stack(wv_s), jnp.stack(sc_s), jnp.stack(lc_s), jnp.stack(lh_s))
```

```python
import functools

import jax
import jax.numpy as jnp
from jax import lax
from jax.experimental import pallas as pl
from jax.experimental.pallas import tpu as pltpu

f32 = jnp.float32
bf16 = jnp.bfloat16

D_MODEL = 2048
BATCH = 2
SEQ = 4096
DEPTH = 4
DEC_BATCH = 32
DEC_SEQ = 4
PAST_LEN = 16384
N_HEADS = 16
KV_HEADS = 4
GQA_GROUP = N_HEADS // KV_HEADS
HEAD_DIM = 64
ROPE_DIM = HEAD_DIM // 4
ROPE_THETA = 500000.0
WINDOW = 128
ATTN_WIDTH = N_HEADS * HEAD_DIM
KV_WIDTH = KV_HEADS * HEAD_DIM
CONV_WIDTH = D_MODEL // 2
SCONV_K = 3
LRU_WIDTH = D_MODEL // 2
LRU_BLOCKS = 16
LRU_BLOCK_DIM = LRU_WIDTH // LRU_BLOCKS
LRU_CONV_K = 4
LRU_C = 8.0
MEM_LEN = 256
MEM_HEADS = 4
MEM_HEAD_DIM = 256
MEM_WIDTH = MEM_HEADS * MEM_HEAD_DIM
N_BRANCH = 4
BRANCH_WIDTH = D_MODEL // 2
D_FF = 5632
N_EXPERTS = 8
TOP_K = 2
ALPHA = (2 * DEPTH) ** 0.25
LN_EPS = 1e-5
NEG_INF = -1e30

ROWS_P = BATCH * SEQ
ROWS_S = DEC_BATCH * DEC_SEQ
ROWS = ROWS_P + ROWS_S

OFF_Q = 0
OFF_K = OFF_Q + ATTN_WIDTH
OFF_V = OFF_K + KV_WIDTH
OFF_CB = OFF_V + KV_WIDTH
OFF_CC = OFF_CB + CONV_WIDTH
OFF_CH = OFF_CC + CONV_WIDTH
OFF_XL = OFF_CH + CONV_WIDTH
OFF_QM = OFF_XL + LRU_WIDTH
OFF_G = OFF_QM + MEM_WIDTH
MIX_IN = OFF_G + N_BRANCH * D_MODEL

LANE = 128
SUBLANE = 8
MIB = 1024 * 1024
TM = 832
TN_IN = 512
TM_OPROJ = 416
TM_FFN = 640
TF = 512
CW = 512
TT_CONV = 1024
TT_LRU = 512
TQ_MEM = 2048

NT_DIMS = (((1,), (1,)), ((), ()))


def _params(n_axes, vmem_mib):
    return pltpu.CompilerParams(dimension_semantics=("arbitrary",) * n_axes,
                                vmem_limit_bytes=vmem_mib * MIB)


def _layer_norm(v, g, b):
    mu = jnp.mean(v, axis=-1, keepdims=True)
    vc = v - mu
    var = jnp.mean(vc * vc, axis=-1, keepdims=True)
    return vc * lax.rsqrt(var + LN_EPS) * g + b


def _expm1(x):
    p = 1.0 + x * (1.0 / 13.0)
    for k in range(12, 1, -1):
        p = 1.0 + (x * (1.0 / k)) * p
    return jnp.where(jnp.abs(x) < 0.5, x * p, jnp.exp(x) - 1.0)


def _mm_kernel(x_ref, w_ref, o_ref, xb_ref):
    @pl.when(pl.program_id(1) == 0)
    def _():
        xb_ref[...] = x_ref[...].astype(bf16)

    o_ref[...] = jnp.dot(xb_ref[...], w_ref[...], preferred_element_type=f32)


def _matmul(x, w_stack, layer, tm, tn):
    m, k = x.shape
    n = w_stack.shape[-1]
    return pl.pallas_call(
        _mm_kernel,
        grid=(m // tm, n // tn),
        in_specs=[pl.BlockSpec((tm, k), lambda i, j: (i, 0)),
                  pl.BlockSpec((None, k, tn), lambda i, j: (layer, 0, j))],
        out_specs=pl.BlockSpec((tm, tn), lambda i, j: (i, j)),
        out_shape=jax.ShapeDtypeStruct((m, n), f32),
        scratch_shapes=[pltpu.VMEM((tm, k), bf16)],
        compiler_params=_params(2, 40),
        name="matmul",
    )(x, w_stack)


def _rope_tables(pos):
    half = ROPE_DIM // 2
    inv_freq = ROPE_THETA ** (-jnp.arange(half, dtype=f32) * (2.0 / ROPE_DIM))
    ang = pos.astype(f32)[:, None] * inv_freq[None, :]
    cos = jnp.cos(ang)
    sin = jnp.sin(ang)
    p = pos.shape[0]
    rest = HEAD_DIM - ROPE_DIM
    c = jnp.concatenate([cos, cos, jnp.ones((p, rest), f32)], axis=1)
    sa = jnp.concatenate([-sin, jnp.zeros((p, HEAD_DIM - half), f32)], axis=1)
    sb = jnp.concatenate([jnp.zeros((p, half), f32), sin, jnp.zeros((p, rest), f32)], axis=1)
    rep = LANE // HEAD_DIM
    return jnp.tile(c, (1, rep)), jnp.tile(sa, (1, rep)), jnp.tile(sb, (1, rep))


def _rope(x, c, sa, sb):
    half = ROPE_DIM // 2
    chunks = []
    for j in range(x.shape[1] // LANE):
        xc = x[:, LANE * j:LANE * (j + 1)]
        chunks.append(xc * c + pltpu.roll(xc, LANE - half, 1) * sa + pltpu.roll(xc, half, 1) * sb)
    return chunks[0] if len(chunks) == 1 else jnp.concatenate(chunks, axis=1)


def _sink_softmax(s, sink):
    m = jnp.maximum(jnp.max(s, axis=-1, keepdims=True), sink)
    p = jnp.exp(s - m)
    denom = jnp.sum(p, axis=-1, keepdims=True) + jnp.exp(sink - m)
    return p * (1.0 / denom)


def _attn_p_kernel(sinks_ref, q_ref, k_ref, v_ref, c_ref, sa_ref, sb_ref, y_ref, kr_ref,
                   kprev, vprev, *, layer):
    n = pl.program_id(1)

    @pl.when(n == 0)
    def _():
        kprev[...] = jnp.zeros(kprev.shape, f32)
        vprev[...] = jnp.zeros(vprev.shape, f32)

    c, sa, sb = c_ref[...], sa_ref[...], sb_ref[...]
    q = _rope(q_ref[...], c, sa, sb)
    kc = _rope(k_ref[...], c, sa, sb)
    vc = v_ref[...]
    kr_ref[...] = kc
    kall = jnp.concatenate([kprev[...], kc], axis=0).astype(bf16)
    vall = jnp.concatenate([vprev[...], vc], axis=0).astype(bf16)
    kprev[...] = kc
    vprev[...] = vc

    rows = GQA_GROUP * WINDOW
    qi = lax.broadcasted_iota(jnp.int32, (rows, 2 * WINDOW), 0) % WINDOW
    kj = lax.broadcasted_iota(jnp.int32, (rows, 2 * WINDOW), 1)
    kmin = jnp.where(n > 0, 0, WINDOW)
    mask = (kj > qi) & (kj <= qi + WINDOW) & (kj >= kmin)
    rg = lax.broadcasted_iota(jnp.int32, (rows, 1), 0) // WINDOW
    for h in range(KV_HEADS):
        qh = jnp.concatenate(
            [q[:, (GQA_GROUP * h + g) * HEAD_DIM:(GQA_GROUP * h + g + 1) * HEAD_DIM]
             for g in range(GQA_GROUP)], axis=0).astype(bf16)
        kh = kall[:, HEAD_DIM * h:HEAD_DIM * (h + 1)]
        vh = vall[:, HEAD_DIM * h:HEAD_DIM * (h + 1)]
        s = lax.dot_general(qh, kh, NT_DIMS, preferred_element_type=f32) * (HEAD_DIM ** -0.5)
        s = jnp.where(mask, s, NEG_INF)
        sink = jnp.zeros((rows, 1), f32)
        for g in range(GQA_GROUP):
            sink = jnp.where(rg == g, sinks_ref[layer, GQA_GROUP * h + g], sink)
        pn = _sink_softmax(s, sink).astype(bf16)
        o = jnp.dot(pn, vh, preferred_element_type=f32)
        for g in range(GQA_GROUP):
            hd = GQA_GROUP * h + g
            y_ref[:, hd * HEAD_DIM:(hd + 1) * HEAD_DIM] = o[WINDOW * g:WINDOW * (g + 1), :].astype(bf16)


def _attn_prompt(z, sinks, tables, layer):
    nblk = SEQ // WINDOW
    row = lambda b, n: b * nblk + n
    return pl.pallas_call(
        functools.partial(_attn_p_kernel, layer=layer),
        grid=(BATCH, nblk),
        in_specs=[pl.BlockSpec(memory_space=pltpu.SMEM),
                  pl.BlockSpec((WINDOW, ATTN_WIDTH), lambda b, n: (row(b, n), OFF_Q // ATTN_WIDTH)),
                  pl.BlockSpec((WINDOW, KV_WIDTH), lambda b, n: (row(b, n), OFF_K // KV_WIDTH)),
                  pl.BlockSpec((WINDOW, KV_WIDTH), lambda b, n: (row(b, n), OFF_V // KV_WIDTH)),
                  pl.BlockSpec((WINDOW, LANE), lambda b, n: (n, 0)),
                  pl.BlockSpec((WINDOW, LANE), lambda b, n: (n, 0)),
                  pl.BlockSpec((WINDOW, LANE), lambda b, n: (n, 0))],
        out_specs=[pl.BlockSpec((WINDOW, ATTN_WIDTH), lambda b, n: (row(b, n), 0)),
                   pl.BlockSpec((WINDOW, KV_WIDTH), lambda b, n: (row(b, n), 0))],
        out_shape=[jax.ShapeDtypeStruct((ROWS_P, ATTN_WIDTH), bf16),
                   jax.ShapeDtypeStruct((ROWS_P, KV_WIDTH), f32)],
        scratch_shapes=[pltpu.VMEM((WINDOW, KV_WIDTH), f32), pltpu.VMEM((WINDOW, KV_WIDTH), f32)],
        compiler_params=_params(2, 32),
        name="attn_prompt",
    )(sinks, z, z, z, *tables)


SB_ATTN = 8
KALL = WINDOW + SUBLANE


def _attn_s_kernel(sinks_ref, z_ref, ck_ref, cv_ref, c_ref, sa_ref, sb_ref, y_ref, nk_ref, nv_ref,
                   kall, vall, qs, *, layer):
    c, sa, sb = c_ref[...], sa_ref[...], sb_ref[...]
    rows = GQA_GROUP * DEC_SEQ
    kall[WINDOW:KALL, :] = jnp.zeros((SUBLANE, KV_WIDTH), f32)
    vall[WINDOW:KALL, :] = jnp.zeros((SUBLANE, KV_WIDTH), f32)
    qt = lax.broadcasted_iota(jnp.int32, (rows, KALL), 0) % DEC_SEQ
    kj = lax.broadcasted_iota(jnp.int32, (rows, KALL), 1)
    mask = jnp.where(kj < WINDOW, kj - qt, qt - (kj - WINDOW) + 1) > 0
    rg = lax.broadcasted_iota(jnp.int32, (rows, 1), 0) // DEC_SEQ
    for bb in range(SB_ATTN):
        zb = z_ref[bb]
        q = _rope(zb[:, OFF_Q:OFF_Q + ATTN_WIDTH], c, sa, sb)
        kn = _rope(zb[:, OFF_K:OFF_K + KV_WIDTH], c, sa, sb)
        vn = zb[:, OFF_V:OFF_V + KV_WIDTH]
        kall[0:WINDOW, :] = ck_ref[bb]
        vall[0:WINDOW, :] = cv_ref[bb]
        kall[WINDOW:WINDOW + DEC_SEQ, :] = kn
        vall[WINDOW:WINDOW + DEC_SEQ, :] = vn
        nk_ref[bb] = kall[DEC_SEQ:DEC_SEQ + WINDOW, :]
        nv_ref[bb] = vall[DEC_SEQ:DEC_SEQ + WINDOW, :]
        for h in range(KV_HEADS):
            for g in range(GQA_GROUP):
                hd = GQA_GROUP * h + g
                qs[DEC_SEQ * g:DEC_SEQ * (g + 1), :] = q[:, hd * HEAD_DIM:(hd + 1) * HEAD_DIM]
            qh = qs[...].astype(bf16)
            kh = kall[:, HEAD_DIM * h:HEAD_DIM * (h + 1)].astype(bf16)
            vh = vall[:, HEAD_DIM * h:HEAD_DIM * (h + 1)].astype(bf16)
            s = lax.dot_general(qh, kh, NT_DIMS, preferred_element_type=f32) * (HEAD_DIM ** -0.5)
            s = jnp.where(mask, s, NEG_INF)
            sink = jnp.zeros((rows, 1), f32)
            for g in range(GQA_GROUP):
                sink = jnp.where(rg == g, sinks_ref[layer, GQA_GROUP * h + g], sink)
            pn = _sink_softmax(s, sink).astype(bf16)
            o = jnp.dot(pn, vh, preferred_element_type=f32)
            for g in range(GQA_GROUP):
                hd = GQA_GROUP * h + g
                y_ref[bb, :, hd * HEAD_DIM:(hd + 1) * HEAD_DIM] = o[DEC_SEQ * g:DEC_SEQ * (g + 1), :]


def _attn_sample(zs3, cache_k, cache_v, sinks, tables, layer):
    qkv = ATTN_WIDTH + 2 * KV_WIDTH
    return pl.pallas_call(
        functools.partial(_attn_s_kernel, layer=layer),
        grid=(DEC_BATCH // SB_ATTN,),
        in_specs=[pl.BlockSpec(memory_space=pltpu.SMEM),
                  pl.BlockSpec((SB_ATTN, DEC_SEQ, qkv), lambda i: (i, 0, 0)),
                  pl.BlockSpec((None, SB_ATTN, WINDOW, KV_WIDTH), lambda i: (layer, i, 0, 0)),
                  pl.BlockSpec((None, SB_ATTN, WINDOW, KV_WIDTH), lambda i: (layer, i, 0, 0)),
                  pl.BlockSpec((DEC_SEQ, LANE), lambda i: (0, 0)),
                  pl.BlockSpec((DEC_SEQ, LANE), lambda i: (0, 0)),
                  pl.BlockSpec((DEC_SEQ, LANE), lambda i: (0, 0))],
        out_specs=[pl.BlockSpec((SB_ATTN, DEC_SEQ, ATTN_WIDTH), lambda i: (i, 0, 0)),
                   pl.BlockSpec((SB_ATTN, WINDOW, KV_WIDTH), lambda i: (i, 0, 0)),
                   pl.BlockSpec((SB_ATTN, WINDOW, KV_WIDTH), lambda i: (i, 0, 0))],
        out_shape=[jax.ShapeDtypeStruct((DEC_BATCH, DEC_SEQ, ATTN_WIDTH), f32),
                   jax.ShapeDtypeStruct((DEC_BATCH, WINDOW, KV_WIDTH), f32),
                   jax.ShapeDtypeStruct((DEC_BATCH, WINDOW, KV_WIDTH), f32)],
        scratch_shapes=[pltpu.VMEM((KALL, KV_WIDTH), f32), pltpu.VMEM((KALL, KV_WIDTH), f32),
                        pltpu.VMEM((GQA_GROUP * DEC_SEQ, HEAD_DIM), f32)],
        compiler_params=_params(1, 32),
        name="attn_sample",
    )(sinks, zs3, cache_k, cache_v, *tables)


def _sconv_p_kernel(cb_ref, cc_ref, ch_ref, w_ref, y_ref, st_ref, ubuf):
    t = pl.program_id(2)
    tt = cc_ref.shape[0]

    @pl.when(t == 0)
    def _():
        ubuf[0:SUBLANE, :] = jnp.zeros((SUBLANE, CW), f32)

    ubuf[SUBLANE:SUBLANE + tt, :] = cc_ref[...] * ch_ref[...]
    w = w_ref[...]
    yc = ubuf[SUBLANE - 2:SUBLANE - 2 + tt, :] * w[0:1]
    yc = yc + ubuf[SUBLANE - 1:SUBLANE - 1 + tt, :] * w[1:2]
    yc = yc + ubuf[SUBLANE:SUBLANE + tt, :] * w[2:3]
    y_ref[...] = (cb_ref[...] * yc).astype(bf16)
    last = ubuf[tt:tt + SUBLANE, :]
    st_ref[...] = last
    ubuf[0:SUBLANE, :] = last


def _sconv_prompt(z, w_sconv, layer):
    nt = SEQ // TT_CONV
    zspec = lambda off: pl.BlockSpec((TT_CONV, CW), lambda b, c, t: (b * nt + t, off // CW + c))
    return pl.pallas_call(
        _sconv_p_kernel,
        grid=(BATCH, CONV_WIDTH // CW, nt),
        in_specs=[zspec(OFF_CB), zspec(OFF_CC), zspec(OFF_CH),
                  pl.BlockSpec((None, SCONV_K, CW), lambda b, c, t: (layer, 0, c))],
        out_specs=[pl.BlockSpec((TT_CONV, CW), lambda b, c, t: (b * nt + t, c)),
                   pl.BlockSpec((None, SUBLANE, CW), lambda b, c, t: (b, 0, c))],
        out_shape=[jax.ShapeDtypeStruct((ROWS_P, CONV_WIDTH), bf16),
                   jax.ShapeDtypeStruct((BATCH, SUBLANE, CONV_WIDTH), f32)],
        scratch_shapes=[pltpu.VMEM((TT_CONV + SUBLANE, CW), f32)],
        compiler_params=_params(3, 40),
        name="sconv_prompt",
    )(z, z, z, w_sconv)


LRU_BLOCKS_PER_TILE = CW // LRU_BLOCK_DIM


def _build_block_diag(w_ref, bd_ref):
    bd_ref[...] = jnp.zeros(bd_ref.shape, bf16)
    for n in range(LRU_BLOCKS_PER_TILE):
        lo, hi = LRU_BLOCK_DIM * n, LRU_BLOCK_DIM * (n + 1)
        bd_ref[lo:hi, lo:hi] = w_ref[n].astype(bf16)


def _lru_gates(xc, wa_bd, wx_bd, ba, bx, lam):
    xcb = xc.astype(bf16)
    r = jax.nn.sigmoid(jnp.dot(xcb, wa_bd, preferred_element_type=f32) + ba)
    i = jax.nn.sigmoid(jnp.dot(xcb, wx_bd, preferred_element_type=f32) + bx)
    log_a = -LRU_C * r * jax.nn.softplus(-lam)
    a = jnp.exp(log_a)
    mult = jnp.sqrt(-_expm1(2.0 * log_a))
    return a, mult * (i * xc)


def _lru_p_kernel(xl_ref, wc_ref, bc_ref, wa_ref, ba_ref, wx_ref, bx_ref, lam_ref, y_ref, h_ref,
                  xbuf, wa_bd, wx_bd, a_s, b_s, hcar):
    t = pl.program_id(2)
    tt = xl_ref.shape[0]

    @pl.when(t == 0)
    def _():
        xbuf[0:SUBLANE, :] = jnp.zeros((SUBLANE, CW), f32)
        hcar[...] = jnp.zeros(hcar.shape, f32)
        _build_block_diag(wa_ref, wa_bd)
        _build_block_diag(wx_ref, wx_bd)

    xbuf[SUBLANE:SUBLANE + tt, :] = xl_ref[...]
    w = wc_ref[...]
    xc = xbuf[SUBLANE - 3:SUBLANE - 3 + tt, :] * w[0:1]
    for j in range(1, LRU_CONV_K):
        xc = xc + xbuf[SUBLANE - 3 + j:SUBLANE - 3 + j + tt, :] * w[j:j + 1]
    xc = xc + bc_ref[...]
    xbuf[0:SUBLANE, :] = xbuf[tt:tt + SUBLANE, :]

    a, bx = _lru_gates(xc, wa_bd[...], wx_bd[...], ba_ref[...], bx_ref[...], lam_ref[...])

    r8 = lax.broadcasted_iota(jnp.int32, (tt, CW), 0) % SUBLANE
    for s in (1, 2, 4):
        keep = r8 >= s
        a_sh = jnp.where(keep, pltpu.roll(a, s, 0), 1.0)
        b_sh = jnp.where(keep, pltpu.roll(bx, s, 0), 0.0)
        bx = bx + a * b_sh
        a = a * a_sh
    a_s[...] = a
    b_s[...] = bx

    def tile_step(j, h):
        r0 = pl.multiple_of(j * SUBLANE, SUBLANE)
        ht = b_s[pl.ds(r0, SUBLANE), :] + a_s[pl.ds(r0, SUBLANE), :] * h
        b_s[pl.ds(r0, SUBLANE), :] = ht
        return ht[SUBLANE - 1:SUBLANE, :]

    h_last = lax.fori_loop(0, tt // SUBLANE, tile_step, hcar[...])
    hcar[...] = h_last
    y_ref[...] = b_s[...].astype(bf16)
    h_ref[...] = b_s[tt - SUBLANE:tt, :]


def _lru_specs(layer, idx):
    return [pl.BlockSpec((None, LRU_CONV_K, CW), lambda *g: (layer, 0, idx(*g))),
            pl.BlockSpec((None, 1, CW), lambda *g: (layer, 0, idx(*g))),
            pl.BlockSpec((None, LRU_BLOCKS_PER_TILE, LRU_BLOCK_DIM, LRU_BLOCK_DIM),
                         lambda *g: (layer, idx(*g), 0, 0)),
            pl.BlockSpec((None, 1, CW), lambda *g: (layer, 0, idx(*g))),
            pl.BlockSpec((None, LRU_BLOCKS_PER_TILE, LRU_BLOCK_DIM, LRU_BLOCK_DIM),
                         lambda *g: (layer, idx(*g), 0, 0)),
            pl.BlockSpec((None, 1, CW), lambda *g: (layer, 0, idx(*g))),
            pl.BlockSpec((None, 1, CW), lambda *g: (layer, 0, idx(*g)))]


def _lru_prompt(z, lru_w, layer):
    nt = SEQ // TT_LRU
    return pl.pallas_call(
        _lru_p_kernel,
        grid=(BATCH, LRU_WIDTH // CW, nt),
        in_specs=[pl.BlockSpec((TT_LRU, CW), lambda b, c, t: (b * nt + t, OFF_XL // CW + c))]
                 + _lru_specs(layer, lambda b, c, t: c),
        out_specs=[pl.BlockSpec((TT_LRU, CW), lambda b, c, t: (b * nt + t, c)),
                   pl.BlockSpec((None, SUBLANE, CW), lambda b, c, t: (b, 0, c))],
        out_shape=[jax.ShapeDtypeStruct((ROWS_P, LRU_WIDTH), bf16),
                   jax.ShapeDtypeStruct((BATCH, SUBLANE, LRU_WIDTH), f32)],
        scratch_shapes=[pltpu.VMEM((TT_LRU + SUBLANE, CW), f32),
                        pltpu.VMEM((CW, CW), bf16), pltpu.VMEM((CW, CW), bf16),
                        pltpu.VMEM((TT_LRU, CW), f32), pltpu.VMEM((TT_LRU, CW), f32),
                        pltpu.VMEM((1, CW), f32)],
        compiler_params=_params(3, 40),
        name="lru_prompt",
    )(z, *lru_w)


def _seq_s_kernel(cb_ref, cc_ref, ch_ref, xl_ref, sc_ref, lc_ref, h0_ref, wsc_ref,
                  wc_ref, bc_ref, wa_ref, ba_ref, wx_ref, bx_ref, lam_ref,
                  yc_ref, yl_ref, scn_ref, hn_ref, wa_bd, wx_bd):
    nb = DEC_BATCH
    rows = lambda v, t: v[nb * t:nb * (t + 1), :]
    u = cc_ref[...] * ch_ref[...]
    cb = cb_ref[...]
    up = [sc_ref[j] for j in range(SCONV_K - 1)] + [rows(u, t) for t in range(DEC_SEQ)]
    w = wsc_ref[...]
    for t in range(DEC_SEQ):
        yc = up[t] * w[0:1]
        for j in range(1, SCONV_K):
            yc = yc + up[t + j] * w[j:j + 1]
        yc_ref[nb * t:nb * (t + 1), :] = (rows(cb, t) * yc).astype(bf16)
    for j in range(SCONV_K - 1):
        scn_ref[j] = up[DEC_SEQ + j]

    _build_block_diag(wa_ref, wa_bd)
    _build_block_diag(wx_ref, wx_bd)
    xl = xl_ref[...]
    xp = [lc_ref[j] for j in range(LRU_CONV_K - 1)] + [rows(xl, t) for t in range(DEC_SEQ)]
    wl = wc_ref[...]
    xcs = []
    for t in range(DEC_SEQ):
        xc = xp[t] * wl[0:1]
        for j in range(1, LRU_CONV_K):
            xc = xc + xp[t + j] * wl[j:j + 1]
        xcs.append(xc + bc_ref[...])
    xc = jnp.concatenate(xcs, axis=0)
    a, bx = _lru_gates(xc, wa_bd[...], wx_bd[...], ba_ref[...], bx_ref[...], lam_ref[...])
    h = h0_ref[...]
    for t in range(DEC_SEQ):
        h = rows(a, t) * h + rows(bx, t)
        yl_ref[nb * t:nb * (t + 1), :] = h.astype(bf16)
    hn_ref[...] = h


def _seq_sample(z, st_sconv_t, st_lconv_t, st_h, w_sconv, lru_w, layer):
    rblk = ROWS_P // ROWS_S
    zspec = lambda off: pl.BlockSpec((ROWS_S, CW), lambda c: (rblk, off // CW + c))
    return pl.pallas_call(
        _seq_s_kernel,
        grid=(CONV_WIDTH // CW,),
        in_specs=[zspec(OFF_CB), zspec(OFF_CC), zspec(OFF_CH), zspec(OFF_XL),
                  pl.BlockSpec((None, SCONV_K - 1, DEC_BATCH, CW), lambda c: (layer, 0, 0, c)),
                  pl.BlockSpec((None, LRU_CONV_K - 1, DEC_BATCH, CW), lambda c: (layer, 0, 0, c)),
                  pl.BlockSpec((None, DEC_BATCH, CW), lambda c: (layer, 0, c)),
                  pl.BlockSpec((None, SCONV_K, CW), lambda c: (layer, 0, c))]
                 + _lru_specs(layer, lambda c: c),
        out_specs=[pl.BlockSpec((ROWS_S, CW), lambda c: (0, c)),
                   pl.BlockSpec((ROWS_S, CW), lambda c: (0, c)),
                   pl.BlockSpec((SCONV_K - 1, DEC_BATCH, CW), lambda c: (0, 0, c)),
                   pl.BlockSpec((DEC_BATCH, CW), lambda c: (0, c))],
        out_shape=[jax.ShapeDtypeStruct((ROWS_S, CONV_WIDTH), bf16),
                   jax.ShapeDtypeStruct((ROWS_S, LRU_WIDTH), bf16),
                   jax.ShapeDtypeStruct((SCONV_K - 1, DEC_BATCH, CONV_WIDTH), f32),
                   jax.ShapeDtypeStruct((DEC_BATCH, LRU_WIDTH), f32)],
        scratch_shapes=[pltpu.VMEM((CW, CW), bf16), pltpu.VMEM((CW, CW), bf16)],
        compiler_params=_params(1, 32),
        name="seq_sample",
    )(z, z, z, z, st_sconv_t, st_lconv_t, st_h, w_sconv, *lru_w)


def _softmax_rows(s):
    m = jnp.max(s, axis=-1, keepdims=True)
    p = jnp.exp(s - m)
    return p * (1.0 / jnp.sum(p, axis=-1, keepdims=True))


def _mem_p_kernel(q_ref, k_ref, v_ref, y_ref):
    s = lax.dot_general(q_ref[...].astype(bf16), k_ref[...].astype(bf16), NT_DIMS,
                        preferred_element_type=f32) * (MEM_HEAD_DIM ** -0.5)
    pn = _softmax_rows(s).astype(bf16)
    y_ref[...] = jnp.dot(pn, v_ref[...].astype(bf16), preferred_element_type=f32).astype(bf16)


def _mem_prompt(z, mkv):
    nt = SEQ // TQ_MEM
    hw = MEM_HEAD_DIM
    return pl.pallas_call(
        _mem_p_kernel,
        grid=(BATCH, MEM_HEADS, nt),
        in_specs=[pl.BlockSpec((TQ_MEM, hw), lambda b, h, t: (b * nt + t, OFF_QM // hw + h)),
                  pl.BlockSpec((MEM_LEN, hw), lambda b, h, t: (b, h)),
                  pl.BlockSpec((MEM_LEN, hw), lambda b, h, t: (b, MEM_HEADS + h))],
        out_specs=pl.BlockSpec((TQ_MEM, hw), lambda b, h, t: (b * nt + t, h)),
        out_shape=jax.ShapeDtypeStruct((ROWS_P, MEM_WIDTH), bf16),
        compiler_params=_params(3, 32),
        name="mem_prompt",
    )(z, mkv, mkv)


SB_MEM = 4
QEXP = MEM_HEADS * DEC_SEQ


def _mem_s_kernel(q_ref, k_ref, v_ref, y_ref, qexp):
    hw = MEM_HEAD_DIM
    qexp[...] = jnp.zeros(qexp.shape, f32)
    for bb in range(SB_MEM):
        qb = q_ref[bb]
        for h in range(MEM_HEADS):
            qexp[DEC_SEQ * h:DEC_SEQ * (h + 1), hw * h:hw * (h + 1)] = qb[:, hw * h:hw * (h + 1)]
        s = lax.dot_general(qexp[...].astype(bf16), k_ref[bb].astype(bf16), NT_DIMS,
                            preferred_element_type=f32) * (MEM_HEAD_DIM ** -0.5)
        pn = _softmax_rows(s).astype(bf16)
        o = jnp.dot(pn, v_ref[bb].astype(bf16), preferred_element_type=f32)
        for h in range(MEM_HEADS):
            y_ref[bb, :, hw * h:hw * (h + 1)] = o[DEC_SEQ * h:DEC_SEQ * (h + 1), hw * h:hw * (h + 1)]


def _mem_sample(qm3, cache_k, cache_v, layer):
    return pl.pallas_call(
        _mem_s_kernel,
        grid=(DEC_BATCH // SB_MEM,),
        in_specs=[pl.BlockSpec((SB_MEM, DEC_SEQ, MEM_WIDTH), lambda i: (i, 0, 0)),
                  pl.BlockSpec((None, SB_MEM, MEM_LEN, MEM_WIDTH), lambda i: (layer, i, 0, 0)),
                  pl.BlockSpec((None, SB_MEM, MEM_LEN, MEM_WIDTH), lambda i: (layer, i, 0, 0))],
        out_specs=pl.BlockSpec((SB_MEM, DEC_SEQ, MEM_WIDTH), lambda i: (i, 0, 0)),
        out_shape=jax.ShapeDtypeStruct((DEC_BATCH, DEC_SEQ, MEM_WIDTH), f32),
        scratch_shapes=[pltpu.VMEM((QEXP, MEM_WIDTH), f32)],
        compiler_params=_params(1, 40),
        name="mem_sample",
    )(qm3, cache_k, cache_v)


def _merge_kernel(y0, y1, y2, y3, g0, g1, g2, g3, wb_ref, o_ref):
    acc = None
    for b, (y, g) in enumerate(((y0, g0), (y1, g1), (y2, g2), (y3, g3))):
        proj = jnp.dot(y[...], wb_ref[b], preferred_element_type=f32)
        term = jax.nn.sigmoid(g[...]) * proj
        acc = term if acc is None else acc + term
    o_ref[...] = acc.astype(bf16)


def _merge(ys, z, w_branch, layer):
    tn = TN_IN
    yspec = pl.BlockSpec((TM, BRANCH_WIDTH), lambda i, j: (i, 0))
    gspec = lambda b: pl.BlockSpec((TM, tn), lambda i, j: (i, (OFF_G + b * D_MODEL) // tn + j))
    return pl.pallas_call(
        _merge_kernel,
        grid=(z.shape[0] // TM, D_MODEL // tn),
        in_specs=[yspec] * N_BRANCH + [gspec(b) for b in range(N_BRANCH)]
                 + [pl.BlockSpec((None, N_BRANCH, BRANCH_WIDTH, tn), lambda i, j: (layer, 0, 0, j))],
        out_specs=pl.BlockSpec((TM, tn), lambda i, j: (i, j)),
        out_shape=jax.ShapeDtypeStruct((z.shape[0], D_MODEL), bf16),
        compiler_params=_params(2, 48),
        name="merge",
    )(*ys, z, z, z, z, w_branch)


def _oproj_kernel(m_ref, w_ref, x_ref, g_ref, b_ref, o_ref):
    d = jnp.dot(m_ref[...], w_ref[...], preferred_element_type=f32)
    o_ref[...] = _layer_norm(ALPHA * x_ref[...] + d, g_ref[0:1, :], b_ref[0:1, :])


def _oproj(merged, w_o, x, ln_g, ln_b, layer):
    tm = TM_OPROJ
    return pl.pallas_call(
        _oproj_kernel,
        grid=(x.shape[0] // tm,),
        in_specs=[pl.BlockSpec((tm, D_MODEL), lambda i: (i, 0)),
                  pl.BlockSpec((None, D_MODEL, D_MODEL), lambda i: (layer, 0, 0)),
                  pl.BlockSpec((tm, D_MODEL), lambda i: (i, 0)),
                  pl.BlockSpec((None, 2, D_MODEL), lambda i: (layer, 0, 0)),
                  pl.BlockSpec((None, 2, D_MODEL), lambda i: (layer, 0, 0))],
        out_specs=pl.BlockSpec((tm, D_MODEL), lambda i: (i, 0)),
        out_shape=jax.ShapeDtypeStruct(x.shape, f32),
        compiler_params=_params(1, 48),
        name="oproj_ln",
    )(merged, w_o, x, ln_g, ln_b)


def _router_kernel(x_ref, w_ref, c_ref):
    logits = jnp.dot(x_ref[...], w_ref[...], precision=lax.Precision.HIGHEST,
                     preferred_element_type=f32)
    lane = lax.broadcasted_iota(jnp.int32, logits.shape, 1)
    m1 = jnp.max(logits, axis=-1, keepdims=True)
    i1 = jnp.min(jnp.where(logits == m1, lane, N_EXPERTS), axis=-1, keepdims=True)
    rest = jnp.where(lane == i1, -jnp.inf, logits)
    m2 = jnp.max(rest, axis=-1, keepdims=True)
    i2 = jnp.min(jnp.where(rest == m2, lane, N_EXPERTS), axis=-1, keepdims=True)
    e = jnp.exp(m2 - m1)
    p1 = 1.0 / (1.0 + e)
    p2 = e / (1.0 + e)
    c_ref[...] = jnp.where(lane == i1, p1, 0.0) + jnp.where(lane == i2, p2, 0.0)


def _router(x, w_router, j):
    return pl.pallas_call(
        _router_kernel,
        grid=(x.shape[0] // TM,),
        in_specs=[pl.BlockSpec((TM, D_MODEL), lambda i: (i, 0)),
                  pl.BlockSpec((None, D_MODEL, N_EXPERTS), lambda i: (j, 0, 0))],
        out_specs=pl.BlockSpec((TM, N_EXPERTS), lambda i: (i, 0)),
        out_shape=jax.ShapeDtypeStruct((x.shape[0], N_EXPERTS), f32),
        compiler_params=_params(1, 32),
        name="router",
    )(x, w_router)


def _ffn_kernel(*refs, moe):
    if moe:
        x_ref, c_ref, wg_ref, wu_ref, wd_ref, g_ref, b_ref, o_ref, xb_ref = refs
    else:
        x_ref, wg_ref, wu_ref, wd_ref, g_ref, b_ref, o_ref, xb_ref = refs
    e = pl.program_id(1)
    f = pl.program_id(2)
    first = jnp.logical_and(e == 0, f == 0)
    last = jnp.logical_and(e == pl.num_programs(1) - 1, f == pl.num_programs(2) - 1)

    @pl.when(first)
    def _():
        xb_ref[...] = x_ref[...].astype(bf16)

    xb = xb_ref[...]
    g = jnp.dot(xb, wg_ref[...], preferred_element_type=f32)
    u = jnp.dot(xb, wu_ref[...], preferred_element_type=f32)
    h = (jax.nn.silu(g) * u).astype(bf16)
    d = jnp.dot(h, wd_ref[...], preferred_element_type=f32)
    if moe:
        c = c_ref[...]
        lane = lax.broadcasted_iota(jnp.int32, c.shape, 1)
        d = jnp.sum(jnp.where(lane == e, c, 0.0), axis=-1, keepdims=True) * d

    @pl.when(first)
    def _():
        o_ref[...] = d

    @pl.when(jnp.logical_not(first))
    def _():
        o_ref[...] += d

    @pl.when(last)
    def _():
        o_ref[...] = _layer_norm(ALPHA * x_ref[...] + o_ref[...], g_ref[1:2, :], b_ref[1:2, :])


def _ffn(x, combine, wg, wu, wd, ln_g, ln_b, layer, j):
    moe = combine is not None
    n_e = N_EXPERTS if moe else 1
    if moe:
        w_in = pl.BlockSpec((None, None, D_MODEL, TF), lambda i, e, f: (j, e, 0, f))
        w_out = pl.BlockSpec((None, None, TF, D_MODEL), lambda i, e, f: (j, e, f, 0))
    else:
        w_in = pl.BlockSpec((None, D_MODEL, TF), lambda i, e, f: (j, 0, f))
        w_out = pl.BlockSpec((None, TF, D_MODEL), lambda i, e, f: (j, f, 0))
    tm = TM_FFN
    xspec = pl.BlockSpec((tm, D_MODEL), lambda i, e, f: (i, 0))
    lnspec = pl.BlockSpec((None, 2, D_MODEL), lambda i, e, f: (layer, 0, 0))
    in_specs = [xspec] + ([pl.BlockSpec((tm, N_EXPERTS), lambda i, e, f: (i, 0))] if moe else []) \
        + [w_in, w_in, w_out, lnspec, lnspec]
    args = (x,) + ((combine,) if moe else ()) + (wg, wu, wd, ln_g, ln_b)
    return pl.pallas_call(
        functools.partial(_ffn_kernel, moe=moe),
        grid=(x.shape[0] // tm, n_e, D_FF // TF),
        in_specs=in_specs,
        out_specs=xspec,
        out_shape=jax.ShapeDtypeStruct(x.shape, f32),
        scratch_shapes=[pltpu.VMEM((tm, D_MODEL), bf16)],
        compiler_params=_params(3, 52),
        name="ffn_moe" if moe else "ffn_dense",
    )(*args)


def kernel(x_prompt, x_sample, mem_prompt, cache_win_k, cache_win_v, cache_mem_k, cache_mem_v,
           state_sconv, state_lru_conv, state_lru_h, w_mix_in, sinks, w_sconv, w_lru_conv, b_lru_conv,
           w_lru_a, b_lru_a, w_lru_x, b_lru_x, lru_lambda, w_mem_kv, w_branch, w_o, ln_g, ln_b,
           w_router, w_ffn_gate, w_ffn_up, w_ffn_down, w_exp_gate, w_exp_up, w_exp_down):
    x = jnp.concatenate([x_prompt.reshape(ROWS_P, D_MODEL),
                         x_sample.transpose(1, 0, 2).reshape(ROWS_S, D_MODEL)], axis=0)
    mem = mem_prompt.reshape(BATCH * MEM_LEN, D_MODEL)
    w_in_b = w_mix_in.astype(bf16)
    w_mkv_b = w_mem_kv.astype(bf16)
    w_br_b = w_branch.astype(bf16)
    w_o_b = w_o.astype(bf16)
    wfg, wfu, wfd = w_ffn_gate.astype(bf16), w_ffn_up.astype(bf16), w_ffn_down.astype(bf16)
    weg, weu, wed = w_exp_gate.astype(bf16), w_exp_up.astype(bf16), w_exp_down.astype(bf16)
    ck = cache_win_k.reshape(DEPTH, DEC_BATCH, WINDOW, KV_WIDTH)
    cv = cache_win_v.reshape(DEPTH, DEC_BATCH, WINDOW, KV_WIDTH)
    cmk = cache_mem_k.reshape(DEPTH, DEC_BATCH, MEM_LEN, MEM_WIDTH)
    cmv = cache_mem_v.reshape(DEPTH, DEC_BATCH, MEM_LEN, MEM_WIDTH)
    st_sc_t = state_sconv.transpose(0, 2, 1, 3)
    st_lc_t = state_lru_conv.transpose(0, 2, 1, 3)
    lru_w = (w_lru_conv, b_lru_conv.reshape(DEPTH, 1, LRU_WIDTH), w_lru_a,
             b_lru_a.reshape(DEPTH, 1, LRU_WIDTH), w_lru_x, b_lru_x.reshape(DEPTH, 1, LRU_WIDTH),
             lru_lambda.reshape(DEPTH, 1, LRU_WIDTH))
    tab_p = _rope_tables(jnp.arange(SEQ, dtype=jnp.int32))
    tab_s = _rope_tables(PAST_LEN + jnp.arange(DEC_SEQ, dtype=jnp.int32))

    def to_b_major(v):
        return v.reshape(DEC_SEQ, DEC_BATCH, v.shape[-1]).transpose(1, 0, 2)

    def to_t_major(v):
        return v.transpose(1, 0, 2).reshape(ROWS_S, v.shape[-1])

    outs = [[] for _ in range(12)]
    for l in range(DEPTH):
        z = _matmul(x, w_in_b, l, TM, TN_IN)
        mkv = _matmul(mem, w_mkv_b, l, BATCH * MEM_LEN, TN_IN)
        zs = z[ROWS_P:]

        ya_p, krot_p = _attn_prompt(z, sinks, tab_p, l)
        ya_s, nk_s, nv_s = _attn_sample(to_b_major(zs[:, :OFF_CB]), ck, cv, sinks, tab_s, l)
        yc_p, sc_p = _sconv_prompt(z, w_sconv, l)
        yl_p, h_p = _lru_prompt(z, lru_w, l)
        yc_s, yl_s, sc_s, h_s = _seq_sample(z, st_sc_t, st_lc_t, state_lru_h, w_sconv, lru_w, l)
        ym_p = _mem_prompt(z, mkv)
        ym_s = _mem_sample(to_b_major(zs[:, OFF_QM:OFF_G]), cmk, cmv, l)

        ys = (jnp.concatenate([ya_p, to_t_major(ya_s).astype(bf16)], axis=0),
              jnp.concatenate([yc_p, yc_s], axis=0),
              jnp.concatenate([yl_p, yl_s], axis=0),
              jnp.concatenate([ym_p, to_t_major(ym_s).astype(bf16)], axis=0))
        merged = _merge(ys, z, w_br_b, l)
        x = _oproj(merged, w_o_b, x, ln_g, ln_b, l)
        j = l // 2
        if l % 2 == 0:
            x = _ffn(x, None, wfg, wfu, wfd, ln_g, ln_b, l, j)
        else:
            x = _ffn(x, _router(x, w_router, j), weg, weu, wed, ln_g, ln_b, l, j)

        zp = z[:ROWS_P].reshape(BATCH, SEQ, MIX_IN)
        kv_shape = (BATCH, WINDOW, KV_HEADS, HEAD_DIM)
        outs[0].append(krot_p.reshape(BATCH, SEQ, KV_WIDTH)[:, SEQ - WINDOW:].reshape(kv_shape))
        outs[1].append(zp[:, SEQ - WINDOW:, OFF_V:OFF_CB].reshape(kv_shape))
        mem_shape = (BATCH, MEM_LEN, MEM_HEADS, MEM_HEAD_DIM)
        outs[2].append(mkv[:, :MEM_WIDTH].reshape(mem_shape))
        outs[3].append(mkv[:, MEM_WIDTH:].reshape(mem_shape))
        outs[4].append(sc_p[:, SUBLANE - (SCONV_K - 1):])
        outs[5].append(zp[:, SEQ - (LRU_CONV_K - 1):, OFF_XL:OFF_QM])
        outs[6].append(h_p[:, SUBLANE - 1])
        kvs_shape = (DEC_BATCH, WINDOW, KV_HEADS, HEAD_DIM)
        outs[7].append(nk_s.reshape(kvs_shape))
        outs[8].append(nv_s.reshape(kvs_shape))
        outs[9].append(sc_s.transpose(1, 0, 2))
        outs[10].append(zs[DEC_BATCH:, OFF_XL:OFF_QM].reshape(LRU_CONV_K - 1, DEC_BATCH, LRU_WIDTH
                                                              ).transpose(1, 0, 2))
        outs[11].append(h_s)

    y_prompt = x[:ROWS_P].reshape(BATCH, SEQ, D_MODEL)
    y_sample = to_b_major(x[ROWS_P:])
    return (y_prompt, y_sample) + tuple(jnp.stack(o) for o in outs)
```

```python
import functools

import jax
import jax.numpy as jnp
from jax import lax
from jax.experimental import pallas as pl
from jax.experimental.pallas import tpu as pltpu

f32 = jnp.float32
bf16 = jnp.bfloat16

D_MODEL = 2048
BATCH = 2
SEQ = 4096
DEPTH = 4
DEC_BATCH = 32
DEC_SEQ = 4
PAST_LEN = 16384
N_HEADS = 16
KV_HEADS = 4
GQA_GROUP = N_HEADS // KV_HEADS
HEAD_DIM = 64
ROPE_DIM = HEAD_DIM // 4
ROPE_THETA = 500000.0
WINDOW = 128
ATTN_WIDTH = N_HEADS * HEAD_DIM
KV_WIDTH = KV_HEADS * HEAD_DIM
CONV_WIDTH = D_MODEL // 2
SCONV_K = 3
LRU_WIDTH = D_MODEL // 2
LRU_BLOCKS = 16
LRU_BLOCK_DIM = LRU_WIDTH // LRU_BLOCKS
LRU_CONV_K = 4
LRU_C = 8.0
MEM_LEN = 256
MEM_HEADS = 4
MEM_HEAD_DIM = 256
MEM_WIDTH = MEM_HEADS * MEM_HEAD_DIM
N_BRANCH = 4
BRANCH_WIDTH = D_MODEL // 2
D_FF = 5632
N_EXPERTS = 8
TOP_K = 2
ALPHA = (2 * DEPTH) ** 0.25
LN_EPS = 1e-5
NEG_INF = -1e30

ROWS_P = BATCH * SEQ
ROWS_S = DEC_BATCH * DEC_SEQ
ROWS = ROWS_P + ROWS_S

OFF_Q = 0
OFF_K = OFF_Q + ATTN_WIDTH
OFF_V = OFF_K + KV_WIDTH
OFF_CB = OFF_V + KV_WIDTH
OFF_CC = OFF_CB + CONV_WIDTH
OFF_CH = OFF_CC + CONV_WIDTH
OFF_XL = OFF_CH + CONV_WIDTH
OFF_QM = OFF_XL + LRU_WIDTH
OFF_G = OFF_QM + MEM_WIDTH
MIX_IN = OFF_G + N_BRANCH * D_MODEL

LANE = 128
SUBLANE = 8
MIB = 1024 * 1024
TM = 832
TN_IN = 512
TM_OPROJ = 416
TM_FFN = 640
TF = 512
CW = 512
TT_CONV = 1024
TT_LRU = 512
TQ_MEM = 2048

NT_DIMS = (((1,), (1,)), ((), ()))


def _params(n_axes, vmem_mib):
    return pltpu.CompilerParams(dimension_semantics=("arbitrary",) * n_axes,
                                vmem_limit_bytes=vmem_mib * MIB)


def _layer_norm(v, g, b):
    mu = jnp.mean(v, axis=-1, keepdims=True)
    vc = v - mu
    var = jnp.mean(vc * vc, axis=-1, keepdims=True)
    return vc * lax.rsqrt(var + LN_EPS) * g + b


def _expm1(x):
    p = 1.0 + x * (1.0 / 13.0)
    for k in range(12, 1, -1):
        p = 1.0 + (x * (1.0 / k)) * p
    return jnp.where(jnp.abs(x) < 0.5, x * p, jnp.exp(x) - 1.0)


def _mm_kernel(x_ref, w_ref, o_ref, xb_ref):
    @pl.when(pl.program_id(1) == 0)
    def _():
        xb_ref[...] = x_ref[...].astype(bf16)

    o_ref[...] = jnp.dot(xb_ref[...], w_ref[...], preferred_element_type=f32)


def _matmul(x, w_stack, layer, tm, tn):
    m, k = x.shape
    n = w_stack.shape[-1]
    return pl.pallas_call(
        _mm_kernel,
        grid=(m // tm, n // tn),
        in_specs=[pl.BlockSpec((tm, k), lambda i, j: (i, 0)),
                  pl.BlockSpec((None, k, tn), lambda i, j: (layer, 0, j))],
        out_specs=pl.BlockSpec((tm, tn), lambda i, j: (i, j)),
        out_shape=jax.ShapeDtypeStruct((m, n), f32),
        scratch_shapes=[pltpu.VMEM((tm, k), bf16)],
        compiler_params=_params(2, 40),
        name="matmul",
    )(x, w_stack)


def _rope_tables(pos):
    half = ROPE_DIM // 2
    inv_freq = ROPE_THETA ** (-jnp.arange(half, dtype=f32) * (2.0 / ROPE_DIM))
    ang = pos.astype(f32)[:, None] * inv_freq[None, :]
    cos = jnp.cos(ang)
    sin = jnp.sin(ang)
    p = pos.shape[0]
    rest = HEAD_DIM - ROPE_DIM
    c = jnp.concatenate([cos, cos, jnp.ones((p, rest), f32)], axis=1)
    sa = jnp.concatenate([-sin, jnp.zeros((p, HEAD_DIM - half), f32)], axis=1)
    sb = jnp.concatenate([jnp.zeros((p, half), f32), sin, jnp.zeros((p, rest), f32)], axis=1)
    rep = LANE // HEAD_DIM
    return jnp.tile(c, (1, rep)), jnp.tile(sa, (1, rep)), jnp.tile(sb, (1, rep))


def _rope(x, c, sa, sb):
    half = ROPE_DIM // 2
    chunks = []
    for j in range(x.shape[1] // LANE):
        xc = x[:, LANE * j:LANE * (j + 1)]
        chunks.append(xc * c + pltpu.roll(xc, LANE - half, 1) * sa + pltpu.roll(xc, half, 1) * sb)
    return chunks[0] if len(chunks) == 1 else jnp.concatenate(chunks, axis=1)


def _sink_softmax(s, sink):
    m = jnp.maximum(jnp.max(s, axis=-1, keepdims=True), sink)
    p = jnp.exp(s - m)
    denom = jnp.sum(p, axis=-1, keepdims=True) + jnp.exp(sink - m)
    return p * (1.0 / denom)


def _attn_p_kernel(y_init, sinks_ref, q_ref, k_ref, v_ref, c_ref, sa_ref, sb_ref, y_ref, kr_ref,
                   kprev, vprev, *, layer):
    del y_init
    n = pl.program_id(1)

    @pl.when(n == 0)
    def _():
        kprev[...] = jnp.zeros(kprev.shape, f32)
        vprev[...] = jnp.zeros(vprev.shape, f32)

    c, sa, sb = c_ref[...], sa_ref[...], sb_ref[...]
    q = _rope(q_ref[...], c, sa, sb)
    kc = _rope(k_ref[...], c, sa, sb)
    vc = v_ref[...]
    kr_ref[...] = kc
    kall = jnp.concatenate([kprev[...], kc], axis=0).astype(bf16)
    vall = jnp.concatenate([vprev[...], vc], axis=0).astype(bf16)
    kprev[...] = kc
    vprev[...] = vc

    rows = GQA_GROUP * WINDOW
    qi = lax.broadcasted_iota(jnp.int32, (rows, 2 * WINDOW), 0) % WINDOW
    kj = lax.broadcasted_iota(jnp.int32, (rows, 2 * WINDOW), 1)
    kmin = jnp.where(n > 0, 0, WINDOW)
    mask = (kj > qi) & (kj <= qi + WINDOW) & (kj >= kmin)
    rg = lax.broadcasted_iota(jnp.int32, (rows, 1), 0) // WINDOW
    for h in range(KV_HEADS):
        qh = jnp.concatenate(
            [q[:, (GQA_GROUP * h + g) * HEAD_DIM:(GQA_GROUP * h + g + 1) * HEAD_DIM]
             for g in range(GQA_GROUP)], axis=0).astype(bf16)
        kh = kall[:, HEAD_DIM * h:HEAD_DIM * (h + 1)]
        vh = vall[:, HEAD_DIM * h:HEAD_DIM * (h + 1)]
        s = lax.dot_general(qh, kh, NT_DIMS, preferred_element_type=f32) * (HEAD_DIM ** -0.5)
        s = jnp.where(mask, s, NEG_INF)
        sink = jnp.zeros((rows, 1), f32)
        for g in range(GQA_GROUP):
            sink = jnp.where(rg == g, sinks_ref[layer, GQA_GROUP * h + g], sink)
        pn = _sink_softmax(s, sink).astype(bf16)
        o = jnp.dot(pn, vh, preferred_element_type=f32)
        for g in range(GQA_GROUP):
            hd = GQA_GROUP * h + g
            y_ref[:, hd * HEAD_DIM:(hd + 1) * HEAD_DIM] = o[WINDOW * g:WINDOW * (g + 1), :].astype(bf16)


def _attn_prompt(y_init, z, sinks, tables, layer):
    nblk = SEQ // WINDOW
    row = lambda b, n: b * nblk + n
    return pl.pallas_call(
        functools.partial(_attn_p_kernel, layer=layer),
        grid=(BATCH, nblk),
        in_specs=[pl.BlockSpec(memory_space=pl.ANY),
                  pl.BlockSpec(memory_space=pltpu.SMEM),
                  pl.BlockSpec((WINDOW, ATTN_WIDTH), lambda b, n: (row(b, n), OFF_Q // ATTN_WIDTH)),
                  pl.BlockSpec((WINDOW, KV_WIDTH), lambda b, n: (row(b, n), OFF_K // KV_WIDTH)),
                  pl.BlockSpec((WINDOW, KV_WIDTH), lambda b, n: (row(b, n), OFF_V // KV_WIDTH)),
                  pl.BlockSpec((WINDOW, LANE), lambda b, n: (n, 0)),
                  pl.BlockSpec((WINDOW, LANE), lambda b, n: (n, 0)),
                  pl.BlockSpec((WINDOW, LANE), lambda b, n: (n, 0))],
        out_specs=[pl.BlockSpec((WINDOW, ATTN_WIDTH), lambda b, n: (row(b, n), 0)),
                   pl.BlockSpec((WINDOW, KV_WIDTH), lambda b, n: (row(b, n), 0))],
        out_shape=[jax.ShapeDtypeStruct((ROWS, ATTN_WIDTH), bf16),
                   jax.ShapeDtypeStruct((ROWS_P, KV_WIDTH), f32)],
        scratch_shapes=[pltpu.VMEM((WINDOW, KV_WIDTH), f32), pltpu.VMEM((WINDOW, KV_WIDTH), f32)],
        input_output_aliases={0: 0},
        compiler_params=_params(2, 32),
        name="attn_prompt",
    )(y_init, sinks, z, z, z, *tables)


SB_ATTN = 8
KALL = WINDOW + SUBLANE


def _attn_s_kernel(sinks_ref, z_ref, ck_ref, cv_ref, c_ref, sa_ref, sb_ref, y_ref, nk_ref, nv_ref,
                   kall, vall, qs, *, layer):
    c, sa, sb = c_ref[...], sa_ref[...], sb_ref[...]
    rows = GQA_GROUP * DEC_SEQ
    kall[WINDOW:KALL, :] = jnp.zeros((SUBLANE, KV_WIDTH), f32)
    vall[WINDOW:KALL, :] = jnp.zeros((SUBLANE, KV_WIDTH), f32)
    qt = lax.broadcasted_iota(jnp.int32, (rows, KALL), 0) % DEC_SEQ
    kj = lax.broadcasted_iota(jnp.int32, (rows, KALL), 1)
    mask = jnp.where(kj < WINDOW, kj - qt, qt - (kj - WINDOW) + 1) > 0
    rg = lax.broadcasted_iota(jnp.int32, (rows, 1), 0) // DEC_SEQ
    for bb in range(SB_ATTN):
        zb = z_ref[bb]
        q = _rope(zb[:, OFF_Q:OFF_Q + ATTN_WIDTH], c, sa, sb)
        kn = _rope(zb[:, OFF_K:OFF_K + KV_WIDTH], c, sa, sb)
        vn = zb[:, OFF_V:OFF_V + KV_WIDTH]
        kall[0:WINDOW, :] = ck_ref[bb]
        vall[0:WINDOW, :] = cv_ref[bb]
        kall[WINDOW:WINDOW + DEC_SEQ, :] = kn
        vall[WINDOW:WINDOW + DEC_SEQ, :] = vn
        nk_ref[bb] = kall[DEC_SEQ:DEC_SEQ + WINDOW, :]
        nv_ref[bb] = vall[DEC_SEQ:DEC_SEQ + WINDOW, :]
        for h in range(KV_HEADS):
            for g in range(GQA_GROUP):
                hd = GQA_GROUP * h + g
                qs[DEC_SEQ * g:DEC_SEQ * (g + 1), :] = q[:, hd * HEAD_DIM:(hd + 1) * HEAD_DIM]
            qh = qs[...].astype(bf16)
            kh = kall[:, HEAD_DIM * h:HEAD_DIM * (h + 1)].astype(bf16)
            vh = vall[:, HEAD_DIM * h:HEAD_DIM * (h + 1)].astype(bf16)
            s = lax.dot_general(qh, kh, NT_DIMS, preferred_element_type=f32) * (HEAD_DIM ** -0.5)
            s = jnp.where(mask, s, NEG_INF)
            sink = jnp.zeros((rows, 1), f32)
            for g in range(GQA_GROUP):
                sink = jnp.where(rg == g, sinks_ref[layer, GQA_GROUP * h + g], sink)
            pn = _sink_softmax(s, sink).astype(bf16)
            o = jnp.dot(pn, vh, preferred_element_type=f32)
            for g in range(GQA_GROUP):
                hd = GQA_GROUP * h + g
                y_ref[bb, :, hd * HEAD_DIM:(hd + 1) * HEAD_DIM] = o[DEC_SEQ * g:DEC_SEQ * (g + 1), :]


def _attn_sample(zs3, cache_k, cache_v, sinks, tables, layer):
    qkv = ATTN_WIDTH + 2 * KV_WIDTH
    return pl.pallas_call(
        functools.partial(_attn_s_kernel, layer=layer),
        grid=(DEC_BATCH // SB_ATTN,),
        in_specs=[pl.BlockSpec(memory_space=pltpu.SMEM),
                  pl.BlockSpec((SB_ATTN, DEC_SEQ, qkv), lambda i: (i, 0, 0)),
                  pl.BlockSpec((None, SB_ATTN, WINDOW, KV_WIDTH), lambda i: (layer, i, 0, 0)),
                  pl.BlockSpec((None, SB_ATTN, WINDOW, KV_WIDTH), lambda i: (layer, i, 0, 0)),
                  pl.BlockSpec((DEC_SEQ, LANE), lambda i: (0, 0)),
                  pl.BlockSpec((DEC_SEQ, LANE), lambda i: (0, 0)),
                  pl.BlockSpec((DEC_SEQ, LANE), lambda i: (0, 0))],
        out_specs=[pl.BlockSpec((SB_ATTN, DEC_SEQ, ATTN_WIDTH), lambda i: (i, 0, 0)),
                   pl.BlockSpec((SB_ATTN, WINDOW, KV_WIDTH), lambda i: (i, 0, 0)),
                   pl.BlockSpec((SB_ATTN, WINDOW, KV_WIDTH), lambda i: (i, 0, 0))],
        out_shape=[jax.ShapeDtypeStruct((DEC_BATCH, DEC_SEQ, ATTN_WIDTH), f32),
                   jax.ShapeDtypeStruct((DEC_BATCH, WINDOW, KV_WIDTH), f32),
                   jax.ShapeDtypeStruct((DEC_BATCH, WINDOW, KV_WIDTH), f32)],
        scratch_shapes=[pltpu.VMEM((KALL, KV_WIDTH), f32), pltpu.VMEM((KALL, KV_WIDTH), f32),
                        pltpu.VMEM((GQA_GROUP * DEC_SEQ, HEAD_DIM), f32)],
        compiler_params=_params(1, 32),
        name="attn_sample",
    )(sinks, zs3, cache_k, cache_v, *tables)


def _sconv_p_kernel(y_init, cb_ref, cc_ref, ch_ref, w_ref, y_ref, st_ref, ubuf):
    del y_init
    t = pl.program_id(2)
    tt = cc_ref.shape[0]

    @pl.when(t == 0)
    def _():
        ubuf[0:SUBLANE, :] = jnp.zeros((SUBLANE, CW), f32)

    ubuf[SUBLANE:SUBLANE + tt, :] = cc_ref[...] * ch_ref[...]
    w = w_ref[...]
    yc = ubuf[SUBLANE - 2:SUBLANE - 2 + tt, :] * w[0:1]
    yc = yc + ubuf[SUBLANE - 1:SUBLANE - 1 + tt, :] * w[1:2]
    yc = yc + ubuf[SUBLANE:SUBLANE + tt, :] * w[2:3]
    y_ref[...] = (cb_ref[...] * yc).astype(bf16)
    last = ubuf[tt:tt + SUBLANE, :]
    st_ref[...] = last
    ubuf[0:SUBLANE, :] = last


def _sconv_prompt(y_init, z, w_sconv, layer):
    nt = SEQ // TT_CONV
    zspec = lambda off: pl.BlockSpec((TT_CONV, CW), lambda b, c, t: (b * nt + t, off // CW + c))
    return pl.pallas_call(
        _sconv_p_kernel,
        grid=(BATCH, CONV_WIDTH // CW, nt),
        in_specs=[pl.BlockSpec(memory_space=pl.ANY), zspec(OFF_CB), zspec(OFF_CC), zspec(OFF_CH),
                  pl.BlockSpec((None, SCONV_K, CW), lambda b, c, t: (layer, 0, c))],
        out_specs=[pl.BlockSpec((TT_CONV, CW), lambda b, c, t: (b * nt + t, c)),
                   pl.BlockSpec((None, SUBLANE, CW), lambda b, c, t: (b, 0, c))],
        out_shape=[jax.ShapeDtypeStruct((ROWS, CONV_WIDTH), bf16),
                   jax.ShapeDtypeStruct((BATCH, SUBLANE, CONV_WIDTH), f32)],
        scratch_shapes=[pltpu.VMEM((TT_CONV + SUBLANE, CW), f32)],
        input_output_aliases={0: 0},
        compiler_params=_params(3, 40),
        name="sconv_prompt",
    )(y_init, z, z, z, w_sconv)


LRU_BLOCKS_PER_TILE = CW // LRU_BLOCK_DIM


def _build_block_diag(w_ref, bd_ref):
    bd_ref[...] = jnp.zeros(bd_ref.shape, bf16)
    for n in range(LRU_BLOCKS_PER_TILE):
        lo, hi = LRU_BLOCK_DIM * n, LRU_BLOCK_DIM * (n + 1)
        bd_ref[lo:hi, lo:hi] = w_ref[n].astype(bf16)


def _lru_gates(xc, wa_bd, wx_bd, ba, bx, lam):
    xcb = xc.astype(bf16)
    r = jax.nn.sigmoid(jnp.dot(xcb, wa_bd, preferred_element_type=f32) + ba)
    i = jax.nn.sigmoid(jnp.dot(xcb, wx_bd, preferred_element_type=f32) + bx)
    log_a = -LRU_C * r * jax.nn.softplus(-lam)
    a = jnp.exp(log_a)
    mult = jnp.sqrt(-_expm1(2.0 * log_a))
    return a, mult * (i * xc)


def _lru_p_kernel(y_init, xl_ref, wc_ref, bc_ref, wa_ref, ba_ref, wx_ref, bx_ref, lam_ref,
                  y_ref, h_ref, xbuf, wa_bd, wx_bd, a_s, b_s, hcar):
    del y_init
    t = pl.program_id(2)
    tt = xl_ref.shape[0]

    @pl.when(t == 0)
    def _():
        xbuf[0:SUBLANE, :] = jnp.zeros((SUBLANE, CW), f32)
        hcar[...] = jnp.zeros(hcar.shape, f32)
        _build_block_diag(wa_ref, wa_bd)
        _build_block_diag(wx_ref, wx_bd)

    xbuf[SUBLANE:SUBLANE + tt, :] = xl_ref[...]
    w = wc_ref[...]
    xc = xbuf[SUBLANE - 3:SUBLANE - 3 + tt, :] * w[0:1]
    for j in range(1, LRU_CONV_K):
        xc = xc + xbuf[SUBLANE - 3 + j:SUBLANE - 3 + j + tt, :] * w[j:j + 1]
    xc = xc + bc_ref[...]
    xbuf[0:SUBLANE, :] = xbuf[tt:tt + SUBLANE, :]

    a, bx = _lru_gates(xc, wa_bd[...], wx_bd[...], ba_ref[...], bx_ref[...], lam_ref[...])

    r8 = lax.broadcasted_iota(jnp.int32, (tt, CW), 0) % SUBLANE
    for s in (1, 2, 4):
        keep = r8 >= s
        a_sh = jnp.where(keep, pltpu.roll(a, s, 0), 1.0)
        b_sh = jnp.where(keep, pltpu.roll(bx, s, 0), 0.0)
        bx = bx + a * b_sh
        a = a * a_sh
    a_s[...] = a
    b_s[...] = bx

    def tile_step(j, h):
        r0 = pl.multiple_of(j * SUBLANE, SUBLANE)
        ht = b_s[pl.ds(r0, SUBLANE), :] + a_s[pl.ds(r0, SUBLANE), :] * h
        b_s[pl.ds(r0, SUBLANE), :] = ht
        return ht[SUBLANE - 1:SUBLANE, :]

    h_last = lax.fori_loop(0, tt // SUBLANE, tile_step, hcar[...])
    hcar[...] = h_last
    y_ref[...] = b_s[...].astype(bf16)
    h_ref[...] = b_s[tt - SUBLANE:tt, :]


def _lru_specs(layer, idx):
    return [pl.BlockSpec((None, LRU_CONV_K, CW), lambda *g: (layer, 0, idx(*g))),
            pl.BlockSpec((None, 1, CW), lambda *g: (layer, 0, idx(*g))),
            pl.BlockSpec((None, LRU_BLOCKS_PER_TILE, LRU_BLOCK_DIM, LRU_BLOCK_DIM),
                         lambda *g: (layer, idx(*g), 0, 0)),
            pl.BlockSpec((None, 1, CW), lambda *g: (layer, 0, idx(*g))),
            pl.BlockSpec((None, LRU_BLOCKS_PER_TILE, LRU_BLOCK_DIM, LRU_BLOCK_DIM),
                         lambda *g: (layer, idx(*g), 0, 0)),
            pl.BlockSpec((None, 1, CW), lambda *g: (layer, 0, idx(*g))),
            pl.BlockSpec((None, 1, CW), lambda *g: (layer, 0, idx(*g)))]


def _lru_prompt(y_init, z, lru_w, layer):
    nt = SEQ // TT_LRU
    return pl.pallas_call(
        _lru_p_kernel,
        grid=(BATCH, LRU_WIDTH // CW, nt),
        in_specs=[pl.BlockSpec(memory_space=pl.ANY),
                  pl.BlockSpec((TT_LRU, CW), lambda b, c, t: (b * nt + t, OFF_XL // CW + c))]
                 + _lru_specs(layer, lambda b, c, t: c),
        out_specs=[pl.BlockSpec((TT_LRU, CW), lambda b, c, t: (b * nt + t, c)),
                   pl.BlockSpec((None, SUBLANE, CW), lambda b, c, t: (b, 0, c))],
        out_shape=[jax.ShapeDtypeStruct((ROWS, LRU_WIDTH), bf16),
                   jax.ShapeDtypeStruct((BATCH, SUBLANE, LRU_WIDTH), f32)],
        input_output_aliases={0: 0},
        scratch_shapes=[pltpu.VMEM((TT_LRU + SUBLANE, CW), f32),
                        pltpu.VMEM((CW, CW), bf16), pltpu.VMEM((CW, CW), bf16),
                        pltpu.VMEM((TT_LRU, CW), f32), pltpu.VMEM((TT_LRU, CW), f32),
                        pltpu.VMEM((1, CW), f32)],
        compiler_params=_params(3, 40),
        name="lru_prompt",
    )(y_init, z, *lru_w)


def _seq_s_kernel(cb_ref, cc_ref, ch_ref, xl_ref, sc_ref, lc_ref, h0_ref, wsc_ref,
                  wc_ref, bc_ref, wa_ref, ba_ref, wx_ref, bx_ref, lam_ref,
                  yc_ref, yl_ref, scn_ref, hn_ref, wa_bd, wx_bd):
    nb = DEC_BATCH
    rows = lambda v, t: v[nb * t:nb * (t + 1), :]
    u = cc_ref[...] * ch_ref[...]
    cb = cb_ref[...]
    up = [sc_ref[j] for j in range(SCONV_K - 1)] + [rows(u, t) for t in range(DEC_SEQ)]
    w = wsc_ref[...]
    for t in range(DEC_SEQ):
        yc = up[t] * w[0:1]
        for j in range(1, SCONV_K):
            yc = yc + up[t + j] * w[j:j + 1]
        yc_ref[nb * t:nb * (t + 1), :] = (rows(cb, t) * yc).astype(bf16)
    for j in range(SCONV_K - 1):
        scn_ref[j] = up[DEC_SEQ + j]

    _build_block_diag(wa_ref, wa_bd)
    _build_block_diag(wx_ref, wx_bd)
    xl = xl_ref[...]
    xp = [lc_ref[j] for j in range(LRU_CONV_K - 1)] + [rows(xl, t) for t in range(DEC_SEQ)]
    wl = wc_ref[...]
    xcs = []
    for t in range(DEC_SEQ):
        xc = xp[t] * wl[0:1]
        for j in range(1, LRU_CONV_K):
            xc = xc + xp[t + j] * wl[j:j + 1]
        xcs.append(xc + bc_ref[...])
    xc = jnp.concatenate(xcs, axis=0)
    a, bx = _lru_gates(xc, wa_bd[...], wx_bd[...], ba_ref[...], bx_ref[...], lam_ref[...])
    h = h0_ref[...]
    for t in range(DEC_SEQ):
        h = rows(a, t) * h + rows(bx, t)
        yl_ref[nb * t:nb * (t + 1), :] = h.astype(bf16)
    hn_ref[...] = h


def _seq_sample(z, st_sconv_t, st_lconv_t, st_h, w_sconv, lru_w, layer):
    rblk = ROWS_P // ROWS_S
    zspec = lambda off: pl.BlockSpec((ROWS_S, CW), lambda c: (rblk, off // CW + c))
    return pl.pallas_call(
        _seq_s_kernel,
        grid=(CONV_WIDTH // CW,),
        in_specs=[zspec(OFF_CB), zspec(OFF_CC), zspec(OFF_CH), zspec(OFF_XL),
                  pl.BlockSpec((None, SCONV_K - 1, DEC_BATCH, CW), lambda c: (layer, 0, 0, c)),
                  pl.BlockSpec((None, LRU_CONV_K - 1, DEC_BATCH, CW), lambda c: (layer, 0, 0, c)),
                  pl.BlockSpec((None, DEC_BATCH, CW), lambda c: (layer, 0, c)),
                  pl.BlockSpec((None, SCONV_K, CW), lambda c: (layer, 0, c))]
                 + _lru_specs(layer, lambda c: c),
        out_specs=[pl.BlockSpec((ROWS_S, CW), lambda c: (0, c)),
                   pl.BlockSpec((ROWS_S, CW), lambda c: (0, c)),
                   pl.BlockSpec((SCONV_K - 1, DEC_BATCH, CW), lambda c: (0, 0, c)),
                   pl.BlockSpec((DEC_BATCH, CW), lambda c: (0, c))],
        out_shape=[jax.ShapeDtypeStruct((ROWS_S, CONV_WIDTH), bf16),
                   jax.ShapeDtypeStruct((ROWS_S, LRU_WIDTH), bf16),
                   jax.ShapeDtypeStruct((SCONV_K - 1, DEC_BATCH, CONV_WIDTH), f32),
                   jax.ShapeDtypeStruct((DEC_BATCH, LRU_WIDTH), f32)],
        scratch_shapes=[pltpu.VMEM((CW, CW), bf16), pltpu.VMEM((CW, CW), bf16)],
        compiler_params=_params(1, 32),
        name="seq_sample",
    )(z, z, z, z, st_sconv_t, st_lconv_t, st_h, w_sconv, *lru_w)


def _softmax_rows(s):
    m = jnp.max(s, axis=-1, keepdims=True)
    p = jnp.exp(s - m)
    return p * (1.0 / jnp.sum(p, axis=-1, keepdims=True))


def _mem_p_kernel(y_init, q_ref, k_ref, v_ref, y_ref):
    del y_init
    s = lax.dot_general(q_ref[...].astype(bf16), k_ref[...].astype(bf16), NT_DIMS,
                        preferred_element_type=f32) * (MEM_HEAD_DIM ** -0.5)
    pn = _softmax_rows(s).astype(bf16)
    y_ref[...] = jnp.dot(pn, v_ref[...].astype(bf16), preferred_element_type=f32).astype(bf16)


def _mem_prompt(y_init, z, mkv):
    nt = SEQ // TQ_MEM
    hw = MEM_HEAD_DIM
    return pl.pallas_call(
        _mem_p_kernel,
        grid=(BATCH, MEM_HEADS, nt),
        in_specs=[pl.BlockSpec(memory_space=pl.ANY),
                  pl.BlockSpec((TQ_MEM, hw), lambda b, h, t: (b * nt + t, OFF_QM // hw + h)),
                  pl.BlockSpec((MEM_LEN, hw), lambda b, h, t: (b, h)),
                  pl.BlockSpec((MEM_LEN, hw), lambda b, h, t: (b, MEM_HEADS + h))],
        out_specs=pl.BlockSpec((TQ_MEM, hw), lambda b, h, t: (b * nt + t, h)),
        out_shape=jax.ShapeDtypeStruct((ROWS, MEM_WIDTH), bf16),
        input_output_aliases={0: 0},
        compiler_params=_params(3, 32),
        name="mem_prompt",
    )(y_init, z, mkv, mkv)


SB_MEM = 4
QEXP = MEM_HEADS * DEC_SEQ


def _mem_s_kernel(q_ref, k_ref, v_ref, y_ref, qexp):
    hw = MEM_HEAD_DIM
    qexp[...] = jnp.zeros(qexp.shape, f32)
    for bb in range(SB_MEM):
        qb = q_ref[bb]
        for h in range(MEM_HEADS):
            qexp[DEC_SEQ * h:DEC_SEQ * (h + 1), hw * h:hw * (h + 1)] = qb[:, hw * h:hw * (h + 1)]
        s = lax.dot_general(qexp[...].astype(bf16), k_ref[bb].astype(bf16), NT_DIMS,
                            preferred_element_type=f32) * (MEM_HEAD_DIM ** -0.5)
        pn = _softmax_rows(s).astype(bf16)
        o = jnp.dot(pn, v_ref[bb].astype(bf16), preferred_element_type=f32)
        for h in range(MEM_HEADS):
            y_ref[bb, :, hw * h:hw * (h + 1)] = o[DEC_SEQ * h:DEC_SEQ * (h + 1), hw * h:hw * (h + 1)]


def _mem_sample(qm3, cache_k, cache_v, layer):
    return pl.pallas_call(
        _mem_s_kernel,
        grid=(DEC_BATCH // SB_MEM,),
        in_specs=[pl.BlockSpec((SB_MEM, DEC_SEQ, MEM_WIDTH), lambda i: (i, 0, 0)),
                  pl.BlockSpec((None, SB_MEM, MEM_LEN, MEM_WIDTH), lambda i: (layer, i, 0, 0)),
                  pl.BlockSpec((None, SB_MEM, MEM_LEN, MEM_WIDTH), lambda i: (layer, i, 0, 0))],
        out_specs=pl.BlockSpec((SB_MEM, DEC_SEQ, MEM_WIDTH), lambda i: (i, 0, 0)),
        out_shape=jax.ShapeDtypeStruct((DEC_BATCH, DEC_SEQ, MEM_WIDTH), f32),
        scratch_shapes=[pltpu.VMEM((QEXP, MEM_WIDTH), f32)],
        compiler_params=_params(1, 40),
        name="mem_sample",
    )(qm3, cache_k, cache_v)


def _merge_kernel(y0, y1, y2, y3, g0, g1, g2, g3, wb_ref, o_ref):
    acc = None
    for b, (y, g) in enumerate(((y0, g0), (y1, g1), (y2, g2), (y3, g3))):
        proj = jnp.dot(y[...], wb_ref[b], preferred_element_type=f32)
        term = jax.nn.sigmoid(g[...]) * proj
        acc = term if acc is None else acc + term
    o_ref[...] = acc.astype(bf16)


def _merge(ys, z, w_branch, layer):
    tn = TN_IN
    yspec = pl.BlockSpec((TM, BRANCH_WIDTH), lambda i, j: (i, 0))
    gspec = lambda b: pl.BlockSpec((TM, tn), lambda i, j: (i, (OFF_G + b * D_MODEL) // tn + j))
    return pl.pallas_call(
        _merge_kernel,
        grid=(z.shape[0] // TM, D_MODEL // tn),
        in_specs=[yspec] * N_BRANCH + [gspec(b) for b in range(N_BRANCH)]
                 + [pl.BlockSpec((None, N_BRANCH, BRANCH_WIDTH, tn), lambda i, j: (layer, 0, 0, j))],
        out_specs=pl.BlockSpec((TM, tn), lambda i, j: (i, j)),
        out_shape=jax.ShapeDtypeStruct((z.shape[0], D_MODEL), bf16),
        compiler_params=_params(2, 48),
        name="merge",
    )(*ys, z, z, z, z, w_branch)


def _oproj_kernel(m_ref, w_ref, x_ref, g_ref, b_ref, o_ref):
    d = jnp.dot(m_ref[...], w_ref[...], preferred_element_type=f32)
    o_ref[...] = _layer_norm(ALPHA * x_ref[...] + d, g_ref[0:1, :], b_ref[0:1, :])


def _oproj(merged, w_o, x, ln_g, ln_b, layer):
    tm = TM_OPROJ
    return pl.pallas_call(
        _oproj_kernel,
        grid=(x.shape[0] // tm,),
        in_specs=[pl.BlockSpec((tm, D_MODEL), lambda i: (i, 0)),
                  pl.BlockSpec((None, D_MODEL, D_MODEL), lambda i: (layer, 0, 0)),
                  pl.BlockSpec((tm, D_MODEL), lambda i: (i, 0)),
                  pl.BlockSpec((None, 2, D_MODEL), lambda i: (layer, 0, 0)),
                  pl.BlockSpec((None, 2, D_MODEL), lambda i: (layer, 0, 0))],
        out_specs=pl.BlockSpec((tm, D_MODEL), lambda i: (i, 0)),
        out_shape=jax.ShapeDtypeStruct(x.shape, f32),
        compiler_params=_params(1, 48),
        name="oproj_ln",
    )(merged, w_o, x, ln_g, ln_b)


def _swiglu_partial(xb, wg_ref, wu_ref, wd_ref):
    g = jnp.dot(xb, wg_ref[...], preferred_element_type=f32)
    u = jnp.dot(xb, wu_ref[...], preferred_element_type=f32)
    h = (jax.nn.silu(g) * u).astype(bf16)
    return jnp.dot(h, wd_ref[...], preferred_element_type=f32)


def _ffn_kernel(x_ref, wg_ref, wu_ref, wd_ref, g_ref, b_ref, o_ref, xb_ref):
    f = pl.program_id(1)

    @pl.when(f == 0)
    def _():
        xb_ref[...] = x_ref[...].astype(bf16)

    d = _swiglu_partial(xb_ref[...], wg_ref, wu_ref, wd_ref)

    @pl.when(f == 0)
    def _():
        o_ref[...] = d

    @pl.when(f > 0)
    def _():
        o_ref[...] += d

    @pl.when(f == pl.num_programs(1) - 1)
    def _():
        o_ref[...] = _layer_norm(ALPHA * x_ref[...] + o_ref[...], g_ref[1:2, :], b_ref[1:2, :])


def _ffn(x, wg, wu, wd, ln_g, ln_b, layer, j):
    tm = TM_FFN
    w_in = pl.BlockSpec((None, D_MODEL, TF), lambda i, f: (j, 0, f))
    w_out = pl.BlockSpec((None, TF, D_MODEL), lambda i, f: (j, f, 0))
    xspec = pl.BlockSpec((tm, D_MODEL), lambda i, f: (i, 0))
    lnspec = pl.BlockSpec((None, 2, D_MODEL), lambda i, f: (layer, 0, 0))
    return pl.pallas_call(
        _ffn_kernel,
        grid=(x.shape[0] // tm, D_FF // TF),
        in_specs=[xspec, w_in, w_in, w_out, lnspec, lnspec],
        out_specs=xspec,
        out_shape=jax.ShapeDtypeStruct(x.shape, f32),
        scratch_shapes=[pltpu.VMEM((tm, D_MODEL), bf16)],
        compiler_params=_params(2, 52),
        name="ffn_dense",
    )(x, wg, wu, wd, ln_g, ln_b)


TG = 640


def _max_row_tiles(n_tokens):
    return (n_tokens * TOP_K + N_EXPERTS * (TG - 1)) // TG


def _router_kernel(x_ref, w_ref, p_ref, i_ref, cnt_ref, carry):
    @pl.when(pl.program_id(0) == 0)
    def _():
        carry[...] = jnp.zeros(carry.shape, f32)

    tm = x_ref.shape[0]
    logits = jnp.dot(x_ref[...], w_ref[...], precision=lax.Precision.HIGHEST,
                     preferred_element_type=f32)
    lane = lax.broadcasted_iota(jnp.int32, logits.shape, 1)
    logits = jnp.where(lane < N_EXPERTS, logits, -jnp.inf)
    m1 = jnp.max(logits, axis=-1, keepdims=True)
    i1 = jnp.min(jnp.where(logits == m1, lane, LANE), axis=-1, keepdims=True)
    rest = jnp.where(lane == i1, -jnp.inf, logits)
    m2 = jnp.max(rest, axis=-1, keepdims=True)
    i2 = jnp.min(jnp.where(rest == m2, lane, LANE), axis=-1, keepdims=True)
    e = jnp.exp(m2 - m1)
    p1 = 1.0 / (1.0 + e)
    p2 = e / (1.0 + e)
    sel = jnp.where(lane == i1, 1.0, jnp.where(lane == i2, 1.0, 0.0))
    r_i = lax.broadcasted_iota(jnp.int32, (tm, tm), 0)
    c_i = lax.broadcasted_iota(jnp.int32, (tm, tm), 1)
    tri = jnp.where(r_i > c_i, 1.0, 0.0).astype(bf16)
    before = jnp.dot(tri, sel.astype(bf16), preferred_element_type=f32) + carry[...]
    r1 = jnp.sum(jnp.where(lane == i1, before, 0.0), axis=-1, keepdims=True).astype(jnp.int32)
    r2 = jnp.sum(jnp.where(lane == i2, before, 0.0), axis=-1, keepdims=True).astype(jnp.int32)
    carry[...] = carry[...] + jnp.sum(sel, axis=0, keepdims=True)
    cnt_ref[...] = carry[...]
    p_ref[...] = jnp.where(lane == 0, p1, jnp.where(lane == 1, p2, 0.0))
    i_ref[...] = jnp.where(lane == 0, i1, jnp.where(lane == 1, i2,
                           jnp.where(lane == 2, r1, jnp.where(lane == 3, r2, 0))))


def _router(x, w_router_pad, j):
    n = x.shape[0]
    tm = TM_FFN
    return pl.pallas_call(
        _router_kernel,
        grid=(n // tm,),
        in_specs=[pl.BlockSpec((tm, D_MODEL), lambda i: (i, 0)),
                  pl.BlockSpec((None, D_MODEL, LANE), lambda i: (j, 0, 0))],
        out_specs=[pl.BlockSpec((tm, LANE), lambda i: (i, 0)),
                   pl.BlockSpec((tm, LANE), lambda i: (i, 0)),
                   pl.BlockSpec((1, LANE), lambda i: (0, 0))],
        out_shape=[jax.ShapeDtypeStruct((n, LANE), f32),
                   jax.ShapeDtypeStruct((n, LANE), jnp.int32),
                   jax.ShapeDtypeStruct((1, LANE), f32)],
        scratch_shapes=[pltpu.VMEM((1, LANE), f32)],
        compiler_params=_params(1, 40),
        name="router",
    )(x, w_router_pad)


def _routing_plan(iinfo, counts):
    cnt = counts[0, :N_EXPERTS].astype(jnp.int32)
    tiles = (cnt + (TG - 1)) // TG
    end_tile = jnp.cumsum(tiles)
    offset = (end_tile - tiles) * TG
    pos1 = offset[iinfo[:, 0]] + iinfo[:, 2]
    pos2 = offset[iinfo[:, 1]] + iinfo[:, 3]
    tile_ids = jnp.arange(_max_row_tiles(iinfo.shape[0]), dtype=jnp.int32)
    tile_expert = jnp.minimum(jnp.sum(tile_ids[:, None] >= end_tile[None, :], axis=1),
                              N_EXPERTS - 1).astype(jnp.int32)
    return pos1, pos2, tile_expert, end_tile[N_EXPERTS - 1:].astype(jnp.int32)


def _row_copy(src, s, dst, d, sem):
    return pltpu.make_async_copy(src.at[pl.ds(s, 1)], dst.at[pl.ds(d, 1)], sem)


def _dispatch_kernel(pos1_ref, pos2_ref, x_ref, xs_in, xs_ref, sem):
    del xs_in
    tm = x_ref.shape[0]
    base = pl.program_id(0) * tm

    def issue(r, c):
        _row_copy(x_ref, r, xs_ref, pos1_ref[base + r], sem).start()
        _row_copy(x_ref, r, xs_ref, pos2_ref[base + r], sem).start()
        return c

    def drain(r, c):
        _row_copy(x_ref, r, xs_ref, pos1_ref[base + r], sem).wait()
        _row_copy(x_ref, r, xs_ref, pos2_ref[base + r], sem).wait()
        return c

    lax.fori_loop(0, tm, issue, 0)
    lax.fori_loop(0, tm, drain, 0)


def _dispatch(x, pos1, pos2):
    tm = TM_FFN
    n_rows = _max_row_tiles(x.shape[0]) * TG
    xs0 = jnp.zeros((n_rows, D_MODEL), f32)
    return pl.pallas_call(
        _dispatch_kernel,
        grid_spec=pltpu.PrefetchScalarGridSpec(
            num_scalar_prefetch=2,
            grid=(x.shape[0] // tm,),
            in_specs=[pl.BlockSpec((tm, D_MODEL), lambda i, p1, p2: (i, 0)),
                      pl.BlockSpec(memory_space=pl.ANY)],
            out_specs=pl.BlockSpec(memory_space=pl.ANY),
            scratch_shapes=[pltpu.SemaphoreType.DMA(())]),
        out_shape=jax.ShapeDtypeStruct((n_rows, D_MODEL), f32),
        input_output_aliases={3: 0},
        compiler_params=_params(1, 32),
        name="moe_dispatch",
    )(pos1, pos2, x, xs0)


def _gffn_kernel(te_ref, nt_ref, x_ref, wg_ref, wu_ref, wd_ref, o_ref, xb_ref):
    del te_ref
    f = pl.program_id(1)
    used = pl.program_id(0) < nt_ref[0]

    @pl.when(jnp.logical_and(jnp.logical_not(used), f == 0))
    def _():
        o_ref[...] = jnp.zeros(o_ref.shape, f32)

    @pl.when(used)
    def _():
        @pl.when(f == 0)
        def _():
            xb_ref[...] = x_ref[...].astype(bf16)

        d = _swiglu_partial(xb_ref[...], wg_ref, wu_ref, wd_ref)

        @pl.when(f == 0)
        def _():
            o_ref[...] = d

        @pl.when(f > 0)
        def _():
            o_ref[...] += d


def _grouped_ffn(xs, tile_expert, n_tiles, wg, wu, wd, j):
    n_f = D_FF // TF
    row = lambda i, nt: jnp.minimum(i, nt[0] - 1)
    fcol = lambda i, f, nt: jnp.where(i < nt[0], f, n_f - 1)
    xspec = pl.BlockSpec((TG, D_MODEL), lambda i, f, te, nt: (row(i, nt), 0))
    w_in = pl.BlockSpec((None, None, D_MODEL, TF),
                        lambda i, f, te, nt: (j, te[row(i, nt)], 0, fcol(i, f, nt)))
    w_out = pl.BlockSpec((None, None, TF, D_MODEL),
                         lambda i, f, te, nt: (j, te[row(i, nt)], fcol(i, f, nt), 0))
    return pl.pallas_call(
        _gffn_kernel,
        grid_spec=pltpu.PrefetchScalarGridSpec(
            num_scalar_prefetch=2,
            grid=(xs.shape[0] // TG, n_f),
            in_specs=[xspec, w_in, w_in, w_out],
            out_specs=pl.BlockSpec((TG, D_MODEL), lambda i, f, te, nt: (i, 0)),
            scratch_shapes=[pltpu.VMEM((TG, D_MODEL), bf16)]),
        out_shape=jax.ShapeDtypeStruct(xs.shape, f32),
        compiler_params=_params(2, 52),
        name="ffn_experts",
    )(tile_expert, n_tiles, xs, wg, wu, wd)


def _combine_kernel(pos1_ref, pos2_ref, x_ref, p_ref, y_ref, g_ref, b_ref, o_ref, buf1, buf2, sem):
    tm = x_ref.shape[0]
    base = pl.program_id(0) * tm

    def issue(r, c):
        _row_copy(y_ref, pos1_ref[base + r], buf1, r, sem).start()
        _row_copy(y_ref, pos2_ref[base + r], buf2, r, sem).start()
        return c

    def drain(r, c):
        _row_copy(y_ref, pos1_ref[base + r], buf1, r, sem).wait()
        _row_copy(y_ref, pos2_ref[base + r], buf2, r, sem).wait()
        return c

    lax.fori_loop(0, tm, issue, 0)
    lax.fori_loop(0, tm, drain, 0)
    p = p_ref[...]
    mixed = p[:, 0:1] * buf1[...] + p[:, 1:2] * buf2[...]
    o_ref[...] = _layer_norm(ALPHA * x_ref[...] + mixed, g_ref[1:2, :], b_ref[1:2, :])


def _combine(x, pinfo, y, pos1, pos2, ln_g, ln_b, layer):
    tm = TM_FFN
    xspec = pl.BlockSpec((tm, D_MODEL), lambda i, p1, p2: (i, 0))
    lnspec = pl.BlockSpec((None, 2, D_MODEL), lambda i, p1, p2: (layer, 0, 0))
    return pl.pallas_call(
        _combine_kernel,
        grid_spec=pltpu.PrefetchScalarGridSpec(
            num_scalar_prefetch=2,
            grid=(x.shape[0] // tm,),
            in_specs=[xspec, pl.BlockSpec((tm, LANE), lambda i, p1, p2: (i, 0)),
                      pl.BlockSpec(memory_space=pl.ANY), lnspec, lnspec],
            out_specs=xspec,
            scratch_shapes=[pltpu.VMEM((tm, D_MODEL), f32), pltpu.VMEM((tm, D_MODEL), f32),
                            pltpu.SemaphoreType.DMA(())]),
        out_shape=jax.ShapeDtypeStruct(x.shape, f32),
        compiler_params=_params(1, 48),
        name="moe_combine",
    )(pos1, pos2, x, pinfo, y, ln_g, ln_b)


def _moe(x, w_router_pad, wg, wu, wd, ln_g, ln_b, layer, j):
    pinfo, iinfo, counts = _router(x, w_router_pad, j)
    pos1, pos2, tile_expert, n_tiles = _routing_plan(iinfo, counts)
    xs = _dispatch(x, pos1, pos2)
    y = _grouped_ffn(xs, tile_expert, n_tiles, wg, wu, wd, j)
    return _combine(x, pinfo, y, pos1, pos2, ln_g, ln_b, layer)


def kernel(x_prompt, x_sample, mem_prompt, cache_win_k, cache_win_v, cache_mem_k, cache_mem_v,
           state_sconv, state_lru_conv, state_lru_h, w_mix_in, sinks, w_sconv, w_lru_conv, b_lru_conv,
           w_lru_a, b_lru_a, w_lru_x, b_lru_x, lru_lambda, w_mem_kv, w_branch, w_o, ln_g, ln_b,
           w_router, w_ffn_gate, w_ffn_up, w_ffn_down, w_exp_gate, w_exp_up, w_exp_down):
    x = jnp.concatenate([x_prompt.reshape(ROWS_P, D_MODEL),
                         x_sample.transpose(1, 0, 2).reshape(ROWS_S, D_MODEL)], axis=0)
    mem = mem_prompt.reshape(BATCH * MEM_LEN, D_MODEL)
    w_in_b = w_mix_in.astype(bf16)
    w_mkv_b = w_mem_kv.astype(bf16)
    w_br_b = w_branch.astype(bf16)
    w_o_b = w_o.astype(bf16)
    wfg, wfu, wfd = w_ffn_gate.astype(bf16), w_ffn_up.astype(bf16), w_ffn_down.astype(bf16)
    weg, weu, wed = w_exp_gate.astype(bf16), w_exp_up.astype(bf16), w_exp_down.astype(bf16)
    w_router_pad = jnp.pad(w_router, ((0, 0), (0, 0), (0, LANE - N_EXPERTS)))
    ck = cache_win_k.reshape(DEPTH, DEC_BATCH, WINDOW, KV_WIDTH)
    cv = cache_win_v.reshape(DEPTH, DEC_BATCH, WINDOW, KV_WIDTH)
    cmk = cache_mem_k.reshape(DEPTH, DEC_BATCH, MEM_LEN, MEM_WIDTH)
    cmv = cache_mem_v.reshape(DEPTH, DEC_BATCH, MEM_LEN, MEM_WIDTH)
    st_sc_t = state_sconv.transpose(0, 2, 1, 3)
    st_lc_t = state_lru_conv.transpose(0, 2, 1, 3)
    lru_w = (w_lru_conv, b_lru_conv.reshape(DEPTH, 1, LRU_WIDTH), w_lru_a,
             b_lru_a.reshape(DEPTH, 1, LRU_WIDTH), w_lru_x, b_lru_x.reshape(DEPTH, 1, LRU_WIDTH),
             lru_lambda.reshape(DEPTH, 1, LRU_WIDTH))
    tab_p = _rope_tables(jnp.arange(SEQ, dtype=jnp.int32))
    tab_s = _rope_tables(PAST_LEN + jnp.arange(DEC_SEQ, dtype=jnp.int32))

    def to_b_major(v):
        return v.reshape(DEC_SEQ, DEC_BATCH, v.shape[-1]).transpose(1, 0, 2)

    def to_t_major(v):
        return v.transpose(1, 0, 2).reshape(ROWS_S, v.shape[-1])

    outs = [[] for _ in range(12)]
    for l in range(DEPTH):
        z = _matmul(x, w_in_b, l, TM, TN_IN)
        mkv = _matmul(mem, w_mkv_b, l, BATCH * MEM_LEN, TN_IN)
        zs = z[ROWS_P:]

        def all_rows(y_s):
            return lax.dynamic_update_slice(jnp.zeros((ROWS, y_s.shape[-1]), bf16), y_s, (ROWS_P, 0))

        ya_s, nk_s, nv_s = _attn_sample(to_b_major(zs[:, :OFF_CB]), ck, cv, sinks, tab_s, l)
        yc_s, yl_s, sc_s, h_s = _seq_sample(z, st_sc_t, st_lc_t, state_lru_h, w_sconv, lru_w, l)
        ym_s = _mem_sample(to_b_major(zs[:, OFF_QM:OFF_G]), cmk, cmv, l)
        ya, krot_p = _attn_prompt(all_rows(to_t_major(ya_s).astype(bf16)), z, sinks, tab_p, l)
        yc, sc_p = _sconv_prompt(all_rows(yc_s), z, w_sconv, l)
        yl, h_p = _lru_prompt(all_rows(yl_s), z, lru_w, l)
        ym = _mem_prompt(all_rows(to_t_major(ym_s).astype(bf16)), z, mkv)
        merged = _merge((ya, yc, yl, ym), z, w_br_b, l)
        x = _oproj(merged, w_o_b, x, ln_g, ln_b, l)
        j = l // 2
        if l % 2 == 0:
            x = _ffn(x, wfg, wfu, wfd, ln_g, ln_b, l, j)
        else:
            x = _moe(x, w_router_pad, weg, weu, wed, ln_g, ln_b, l, j)

        def tail(nrows, lo, hi):
            return jnp.stack([z[(b + 1) * SEQ - nrows:(b + 1) * SEQ, lo:hi] for b in range(BATCH)])

        kv_shape = (BATCH, WINDOW, KV_HEADS, HEAD_DIM)
        outs[0].append(krot_p.reshape(BATCH, SEQ, KV_WIDTH)[:, SEQ - WINDOW:].reshape(kv_shape))
        outs[1].append(tail(WINDOW, OFF_V, OFF_CB).reshape(kv_shape))
        mem_shape = (BATCH, MEM_LEN, MEM_HEADS, MEM_HEAD_DIM)
        outs[2].append(mkv[:, :MEM_WIDTH].reshape(mem_shape))
        outs[3].append(mkv[:, MEM_WIDTH:].reshape(mem_shape))
        outs[4].append(sc_p[:, SUBLANE - (SCONV_K - 1):])
        outs[5].append(tail(LRU_CONV_K - 1, OFF_XL, OFF_QM))
        outs[6].append(h_p[:, SUBLANE - 1])
        kvs_shape = (DEC_BATCH, WINDOW, KV_HEADS, HEAD_DIM)
        outs[7].append(nk_s.reshape(kvs_shape))
        outs[8].append(nv_s.reshape(kvs_shape))
        outs[9].append(sc_s.transpose(1, 0, 2))
        outs[10].append(zs[DEC_BATCH:, OFF_XL:OFF_QM].reshape(LRU_CONV_K - 1, DEC_BATCH, LRU_WIDTH
                                                              ).transpose(1, 0, 2))
        outs[11].append(h_s)

    y_prompt = x[:ROWS_P].reshape(BATCH, SEQ, D_MODEL)
    y_sample = to_b_major(x[ROWS_P:])
    return (y_prompt, y_sample) + tuple(jnp.stack(o) for o in outs)
```

```python
import functools

import jax
import jax.numpy as jnp
from jax import lax
from jax.experimental import pallas as pl
from jax.experimental.pallas import tpu as pltpu

f32 = jnp.float32
bf16 = jnp.bfloat16

D_MODEL = 2048
BATCH = 2
SEQ = 4096
DEPTH = 4
DEC_BATCH = 32
DEC_SEQ = 4
PAST_LEN = 16384
N_HEADS = 16
KV_HEADS = 4
GQA_GROUP = N_HEADS // KV_HEADS
HEAD_DIM = 64
ROPE_DIM = HEAD_DIM // 4
ROPE_THETA = 500000.0
WINDOW = 128
ATTN_WIDTH = N_HEADS * HEAD_DIM
KV_WIDTH = KV_HEADS * HEAD_DIM
CONV_WIDTH = D_MODEL // 2
SCONV_K = 3
LRU_WIDTH = D_MODEL // 2
LRU_BLOCKS = 16
LRU_BLOCK_DIM = LRU_WIDTH // LRU_BLOCKS
LRU_CONV_K = 4
LRU_C = 8.0
MEM_LEN = 256
MEM_HEADS = 4
MEM_HEAD_DIM = 256
MEM_WIDTH = MEM_HEADS * MEM_HEAD_DIM
N_BRANCH = 4
BRANCH_WIDTH = D_MODEL // 2
D_FF = 5632
N_EXPERTS = 8
TOP_K = 2
ALPHA = (2 * DEPTH) ** 0.25
LN_EPS = 1e-5
NEG_INF = -1e30

ROWS_P = BATCH * SEQ
ROWS_S = DEC_BATCH * DEC_SEQ
ROWS = ROWS_P + ROWS_S

OFF_Q = 0
OFF_K = OFF_Q + ATTN_WIDTH
OFF_V = OFF_K + KV_WIDTH
OFF_CB = OFF_V + KV_WIDTH
OFF_CC = OFF_CB + CONV_WIDTH
OFF_CH = OFF_CC + CONV_WIDTH
OFF_XL = OFF_CH + CONV_WIDTH
OFF_QM = OFF_XL + LRU_WIDTH
OFF_G = OFF_QM + MEM_WIDTH
MIX_IN = OFF_G + N_BRANCH * D_MODEL

LANE = 128
SUBLANE = 8
MIB = 1024 * 1024
TM = 832
TN_MIX = 1664
TN_GATE = 1024
TN_MKV = 512
TN_MERGE = 512
TM_OPROJ = 416
TM_FFN = 640
TF = 512
CW = 512
TT_CONV = 1024
TT_LRU = 512
TQ_MEM = 2048

NT_DIMS = (((1,), (1,)), ((), ()))


def _params(n_axes, vmem_mib):
    return pltpu.CompilerParams(dimension_semantics=("arbitrary",) * n_axes,
                                vmem_limit_bytes=vmem_mib * MIB)


def _layer_norm(v, g, b):
    mu = jnp.mean(v, axis=-1, keepdims=True)
    vc = v - mu
    var = jnp.mean(vc * vc, axis=-1, keepdims=True)
    return vc * lax.rsqrt(var + LN_EPS) * g + b


def _expm1(x):
    p = 1.0 + x * (1.0 / 13.0)
    for k in range(12, 1, -1):
        p = 1.0 + (x * (1.0 / k)) * p
    return jnp.where(jnp.abs(x) < 0.5, x * p, jnp.exp(x) - 1.0)


def _mm_kernel(x_ref, w_ref, o_ref, xb_ref):
    @pl.when(pl.program_id(1) == 0)
    def _():
        xb_ref[...] = x_ref[...].astype(bf16)

    o_ref[...] = jnp.dot(xb_ref[...], w_ref[...], preferred_element_type=f32)


def _matmul(x, w_stack, layer, tm, tn):
    m, k = x.shape
    n = w_stack.shape[-1]
    vmem = 2 * (tm * k * 4 + k * tn * 2 + tm * tn * 4) + tm * k * 2
    return pl.pallas_call(
        _mm_kernel,
        grid=(m // tm, n // tn),
        in_specs=[pl.BlockSpec((tm, k), lambda i, j: (i, 0)),
                  pl.BlockSpec((None, k, tn), lambda i, j: (layer, 0, j))],
        out_specs=pl.BlockSpec((tm, tn), lambda i, j: (i, j)),
        out_shape=jax.ShapeDtypeStruct((m, n), f32),
        scratch_shapes=[pltpu.VMEM((tm, k), bf16)],
        compiler_params=_params(2, vmem // MIB + 8),
        name="matmul",
    )(x, w_stack)


def _rope_tables(pos):
    half = ROPE_DIM // 2
    inv_freq = ROPE_THETA ** (-jnp.arange(half, dtype=f32) * (2.0 / ROPE_DIM))
    ang = pos.astype(f32)[:, None] * inv_freq[None, :]
    cos = jnp.cos(ang)
    sin = jnp.sin(ang)
    p = pos.shape[0]
    rest = HEAD_DIM - ROPE_DIM
    c = jnp.concatenate([cos, cos, jnp.ones((p, rest), f32)], axis=1)
    sa = jnp.concatenate([-sin, jnp.zeros((p, HEAD_DIM - half), f32)], axis=1)
    sb = jnp.concatenate([jnp.zeros((p, half), f32), sin, jnp.zeros((p, rest), f32)], axis=1)
    rep = LANE // HEAD_DIM
    return jnp.tile(c, (1, rep)), jnp.tile(sa, (1, rep)), jnp.tile(sb, (1, rep))


def _rope(x, c, sa, sb):
    half = ROPE_DIM // 2
    chunks = []
    for j in range(x.shape[1] // LANE):
        xc = x[:, LANE * j:LANE * (j + 1)]
        chunks.append(xc * c + pltpu.roll(xc, LANE - half, 1) * sa + pltpu.roll(xc, half, 1) * sb)
    return chunks[0] if len(chunks) == 1 else jnp.concatenate(chunks, axis=1)


def _sink_softmax(s, sink):
    m = jnp.maximum(jnp.max(s, axis=-1, keepdims=True), sink)
    p = jnp.exp(s - m)
    denom = jnp.sum(p, axis=-1, keepdims=True) + jnp.exp(sink - m)
    return p * (1.0 / denom)


def _attn_p_kernel(y_init, sinks_ref, q_ref, k_ref, v_ref, c_ref, sa_ref, sb_ref, y_ref, kr_ref,
                   kprev, vprev, *, layer):
    del y_init
    n = pl.program_id(1)

    @pl.when(n == 0)
    def _():
        kprev[...] = jnp.zeros(kprev.shape, f32)
        vprev[...] = jnp.zeros(vprev.shape, f32)

    c, sa, sb = c_ref[...], sa_ref[...], sb_ref[...]
    q = _rope(q_ref[...], c, sa, sb)
    kc = _rope(k_ref[...], c, sa, sb)
    vc = v_ref[...]
    kr_ref[...] = kc
    kall = jnp.concatenate([kprev[...], kc], axis=0).astype(bf16)
    vall = jnp.concatenate([vprev[...], vc], axis=0).astype(bf16)
    kprev[...] = kc
    vprev[...] = vc

    rows = GQA_GROUP * WINDOW
    qi = lax.broadcasted_iota(jnp.int32, (rows, 2 * WINDOW), 0) % WINDOW
    kj = lax.broadcasted_iota(jnp.int32, (rows, 2 * WINDOW), 1)
    kmin = jnp.where(n > 0, 0, WINDOW)
    mask = (kj > qi) & (kj <= qi + WINDOW) & (kj >= kmin)
    rg = lax.broadcasted_iota(jnp.int32, (rows, 1), 0) // WINDOW
    for h in range(KV_HEADS):
        qh = jnp.concatenate(
            [q[:, (GQA_GROUP * h + g) * HEAD_DIM:(GQA_GROUP * h + g + 1) * HEAD_DIM]
             for g in range(GQA_GROUP)], axis=0).astype(bf16)
        kh = kall[:, HEAD_DIM * h:HEAD_DIM * (h + 1)]
        vh = vall[:, HEAD_DIM * h:HEAD_DIM * (h + 1)]
        s = lax.dot_general(qh, kh, NT_DIMS, preferred_element_type=f32) * (HEAD_DIM ** -0.5)
        s = jnp.where(mask, s, NEG_INF)
        sink = jnp.zeros((rows, 1), f32)
        for g in range(GQA_GROUP):
            sink = jnp.where(rg == g, sinks_ref[layer, GQA_GROUP * h + g], sink)
        pn = _sink_softmax(s, sink).astype(bf16)
        o = jnp.dot(pn, vh, preferred_element_type=f32)
        for g in range(GQA_GROUP):
            hd = GQA_GROUP * h + g
            y_ref[:, hd * HEAD_DIM:(hd + 1) * HEAD_DIM] = o[WINDOW * g:WINDOW * (g + 1), :].astype(bf16)


def _attn_prompt(y_init, z, sinks, tables, layer):
    nblk = SEQ // WINDOW
    row = lambda b, n: b * nblk + n
    return pl.pallas_call(
        functools.partial(_attn_p_kernel, layer=layer),
        grid=(BATCH, nblk),
        in_specs=[pl.BlockSpec(memory_space=pl.ANY),
                  pl.BlockSpec(memory_space=pltpu.SMEM),
                  pl.BlockSpec((WINDOW, ATTN_WIDTH), lambda b, n: (row(b, n), OFF_Q // ATTN_WIDTH)),
                  pl.BlockSpec((WINDOW, KV_WIDTH), lambda b, n: (row(b, n), OFF_K // KV_WIDTH)),
                  pl.BlockSpec((WINDOW, KV_WIDTH), lambda b, n: (row(b, n), OFF_V // KV_WIDTH)),
                  pl.BlockSpec((WINDOW, LANE), lambda b, n: (n, 0)),
                  pl.BlockSpec((WINDOW, LANE), lambda b, n: (n, 0)),
                  pl.BlockSpec((WINDOW, LANE), lambda b, n: (n, 0))],
        out_specs=[pl.BlockSpec((WINDOW, ATTN_WIDTH), lambda b, n: (row(b, n), 0)),
                   pl.BlockSpec((WINDOW, KV_WIDTH), lambda b, n: (row(b, n), 0))],
        out_shape=[jax.ShapeDtypeStruct((ROWS, ATTN_WIDTH), bf16),
                   jax.ShapeDtypeStruct((ROWS_P, KV_WIDTH), f32)],
        scratch_shapes=[pltpu.VMEM((WINDOW, KV_WIDTH), f32), pltpu.VMEM((WINDOW, KV_WIDTH), f32)],
        input_output_aliases={0: 0},
        compiler_params=_params(2, 32),
        name="attn_prompt",
    )(y_init, sinks, z, z, z, *tables)


SB_ATTN = 8
KALL = WINDOW + SUBLANE


def _attn_s_kernel(sinks_ref, z_ref, ck_ref, cv_ref, c_ref, sa_ref, sb_ref, y_ref, nk_ref, nv_ref,
                   kall, vall, qs, *, layer):
    c, sa, sb = c_ref[...], sa_ref[...], sb_ref[...]
    rows = GQA_GROUP * DEC_SEQ
    kall[WINDOW:KALL, :] = jnp.zeros((SUBLANE, KV_WIDTH), f32)
    vall[WINDOW:KALL, :] = jnp.zeros((SUBLANE, KV_WIDTH), f32)
    qt = lax.broadcasted_iota(jnp.int32, (rows, KALL), 0) % DEC_SEQ
    kj = lax.broadcasted_iota(jnp.int32, (rows, KALL), 1)
    mask = jnp.where(kj < WINDOW, kj - qt, qt - (kj - WINDOW) + 1) > 0
    rg = lax.broadcasted_iota(jnp.int32, (rows, 1), 0) // DEC_SEQ
    for bb in range(SB_ATTN):
        zb = z_ref[bb]
        q = _rope(zb[:, OFF_Q:OFF_Q + ATTN_WIDTH], c, sa, sb)
        kn = _rope(zb[:, OFF_K:OFF_K + KV_WIDTH], c, sa, sb)
        vn = zb[:, OFF_V:OFF_V + KV_WIDTH]
        kall[0:WINDOW, :] = ck_ref[bb]
        vall[0:WINDOW, :] = cv_ref[bb]
        kall[WINDOW:WINDOW + DEC_SEQ, :] = kn
        vall[WINDOW:WINDOW + DEC_SEQ, :] = vn
        nk_ref[bb] = kall[DEC_SEQ:DEC_SEQ + WINDOW, :]
        nv_ref[bb] = vall[DEC_SEQ:DEC_SEQ + WINDOW, :]
        for h in range(KV_HEADS):
            for g in range(GQA_GROUP):
                hd = GQA_GROUP * h + g
                qs[DEC_SEQ * g:DEC_SEQ * (g + 1), :] = q[:, hd * HEAD_DIM:(hd + 1) * HEAD_DIM]
            qh = qs[...].astype(bf16)
            kh = kall[:, HEAD_DIM * h:HEAD_DIM * (h + 1)].astype(bf16)
            vh = vall[:, HEAD_DIM * h:HEAD_DIM * (h + 1)].astype(bf16)
            s = lax.dot_general(qh, kh, NT_DIMS, preferred_element_type=f32) * (HEAD_DIM ** -0.5)
            s = jnp.where(mask, s, NEG_INF)
            sink = jnp.zeros((rows, 1), f32)
            for g in range(GQA_GROUP):
                sink = jnp.where(rg == g, sinks_ref[layer, GQA_GROUP * h + g], sink)
            pn = _sink_softmax(s, sink).astype(bf16)
            o = jnp.dot(pn, vh, preferred_element_type=f32)
            for g in range(GQA_GROUP):
                hd = GQA_GROUP * h + g
                y_ref[bb, :, hd * HEAD_DIM:(hd + 1) * HEAD_DIM] = o[DEC_SEQ * g:DEC_SEQ * (g + 1), :]


def _attn_sample(zs3, cache_k, cache_v, sinks, tables, layer):
    qkv = ATTN_WIDTH + 2 * KV_WIDTH
    return pl.pallas_call(
        functools.partial(_attn_s_kernel, layer=layer),
        grid=(DEC_BATCH // SB_ATTN,),
        in_specs=[pl.BlockSpec(memory_space=pltpu.SMEM),
                  pl.BlockSpec((SB_ATTN, DEC_SEQ, qkv), lambda i: (i, 0, 0)),
                  pl.BlockSpec((None, SB_ATTN, WINDOW, KV_WIDTH), lambda i: (layer, i, 0, 0)),
                  pl.BlockSpec((None, SB_ATTN, WINDOW, KV_WIDTH), lambda i: (layer, i, 0, 0)),
                  pl.BlockSpec((DEC_SEQ, LANE), lambda i: (0, 0)),
                  pl.BlockSpec((DEC_SEQ, LANE), lambda i: (0, 0)),
                  pl.BlockSpec((DEC_SEQ, LANE), lambda i: (0, 0))],
        out_specs=[pl.BlockSpec((SB_ATTN, DEC_SEQ, ATTN_WIDTH), lambda i: (i, 0, 0)),
                   pl.BlockSpec((SB_ATTN, WINDOW, KV_WIDTH), lambda i: (i, 0, 0)),
                   pl.BlockSpec((SB_ATTN, WINDOW, KV_WIDTH), lambda i: (i, 0, 0))],
        out_shape=[jax.ShapeDtypeStruct((DEC_BATCH, DEC_SEQ, ATTN_WIDTH), f32),
                   jax.ShapeDtypeStruct((DEC_BATCH, WINDOW, KV_WIDTH), f32),
                   jax.ShapeDtypeStruct((DEC_BATCH, WINDOW, KV_WIDTH), f32)],
        scratch_shapes=[pltpu.VMEM((KALL, KV_WIDTH), f32), pltpu.VMEM((KALL, KV_WIDTH), f32),
                        pltpu.VMEM((GQA_GROUP * DEC_SEQ, HEAD_DIM), f32)],
        compiler_params=_params(1, 32),
        name="attn_sample",
    )(sinks, zs3, cache_k, cache_v, *tables)


def _sconv_p_kernel(y_init, cb_ref, cc_ref, ch_ref, w_ref, y_ref, st_ref, ubuf):
    del y_init
    t = pl.program_id(2)
    tt = cc_ref.shape[0]

    @pl.when(t == 0)
    def _():
        ubuf[0:SUBLANE, :] = jnp.zeros((SUBLANE, CW), f32)

    ubuf[SUBLANE:SUBLANE + tt, :] = cc_ref[...] * ch_ref[...]
    w = w_ref[...]
    yc = ubuf[SUBLANE - 2:SUBLANE - 2 + tt, :] * w[0:1]
    yc = yc + ubuf[SUBLANE - 1:SUBLANE - 1 + tt, :] * w[1:2]
    yc = yc + ubuf[SUBLANE:SUBLANE + tt, :] * w[2:3]
    y_ref[...] = (cb_ref[...] * yc).astype(bf16)
    last = ubuf[tt:tt + SUBLANE, :]
    st_ref[...] = last
    ubuf[0:SUBLANE, :] = last


def _sconv_prompt(y_init, z, w_sconv, layer):
    nt = SEQ // TT_CONV
    zspec = lambda off: pl.BlockSpec((TT_CONV, CW), lambda b, c, t: (b * nt + t, off // CW + c))
    return pl.pallas_call(
        _sconv_p_kernel,
        grid=(BATCH, CONV_WIDTH // CW, nt),
        in_specs=[pl.BlockSpec(memory_space=pl.ANY), zspec(OFF_CB), zspec(OFF_CC), zspec(OFF_CH),
                  pl.BlockSpec((None, SCONV_K, CW), lambda b, c, t: (layer, 0, c))],
        out_specs=[pl.BlockSpec((TT_CONV, CW), lambda b, c, t: (b * nt + t, c)),
                   pl.BlockSpec((None, SUBLANE, CW), lambda b, c, t: (b, 0, c))],
        out_shape=[jax.ShapeDtypeStruct((ROWS, CONV_WIDTH), bf16),
                   jax.ShapeDtypeStruct((BATCH, SUBLANE, CONV_WIDTH), f32)],
        scratch_shapes=[pltpu.VMEM((TT_CONV + SUBLANE, CW), f32)],
        input_output_aliases={0: 0},
        compiler_params=_params(3, 40),
        name="sconv_prompt",
    )(y_init, z, z, z, w_sconv)


LRU_BLOCKS_PER_TILE = CW // LRU_BLOCK_DIM


def _build_block_diag(w_ref, bd_ref):
    bd_ref[...] = jnp.zeros(bd_ref.shape, bf16)
    for n in range(LRU_BLOCKS_PER_TILE):
        lo, hi = LRU_BLOCK_DIM * n, LRU_BLOCK_DIM * (n + 1)
        bd_ref[lo:hi, lo:hi] = w_ref[n].astype(bf16)


def _lru_gates(xc, wa_bd, wx_bd, ba, bx, lam):
    xcb = xc.astype(bf16)
    r = jax.nn.sigmoid(jnp.dot(xcb, wa_bd, preferred_element_type=f32) + ba)
    i = jax.nn.sigmoid(jnp.dot(xcb, wx_bd, preferred_element_type=f32) + bx)
    log_a = -LRU_C * r * jax.nn.softplus(-lam)
    a = jnp.exp(log_a)
    mult = jnp.sqrt(-_expm1(2.0 * log_a))
    return a, mult * (i * xc)


def _lru_p_kernel(y_init, xl_ref, wc_ref, bc_ref, wa_ref, ba_ref, wx_ref, bx_ref, lam_ref,
                  y_ref, h_ref, xbuf, wa_bd, wx_bd, a_s, b_s, hcar):
    del y_init
    t = pl.program_id(2)
    tt = xl_ref.shape[0]

    @pl.when(t == 0)
    def _():
        xbuf[0:SUBLANE, :] = jnp.zeros((SUBLANE, CW), f32)
        hcar[...] = jnp.zeros(hcar.shape, f32)
        _build_block_diag(wa_ref, wa_bd)
        _build_block_diag(wx_ref, wx_bd)

    xbuf[SUBLANE:SUBLANE + tt, :] = xl_ref[...]
    w = wc_ref[...]
    xc = xbuf[SUBLANE - 3:SUBLANE - 3 + tt, :] * w[0:1]
    for j in range(1, LRU_CONV_K):
        xc = xc + xbuf[SUBLANE - 3 + j:SUBLANE - 3 + j + tt, :] * w[j:j + 1]
    xc = xc + bc_ref[...]
    xbuf[0:SUBLANE, :] = xbuf[tt:tt + SUBLANE, :]

    a, bx = _lru_gates(xc, wa_bd[...], wx_bd[...], ba_ref[...], bx_ref[...], lam_ref[...])

    r8 = lax.broadcasted_iota(jnp.int32, (tt, CW), 0) % SUBLANE
    for s in (1, 2, 4):
        keep = r8 >= s
        a_sh = jnp.where(keep, pltpu.roll(a, s, 0), 1.0)
        b_sh = jnp.where(keep, pltpu.roll(bx, s, 0), 0.0)
        bx = bx + a * b_sh
        a = a * a_sh
    a_s[...] = a
    b_s[...] = bx

    def tile_step(j, h):
        r0 = pl.multiple_of(j * SUBLANE, SUBLANE)
        ht = b_s[pl.ds(r0, SUBLANE), :] + a_s[pl.ds(r0, SUBLANE), :] * h
        b_s[pl.ds(r0, SUBLANE), :] = ht
        return ht[SUBLANE - 1:SUBLANE, :]

    h_last = lax.fori_loop(0, tt // SUBLANE, tile_step, hcar[...])
    hcar[...] = h_last
    y_ref[...] = b_s[...].astype(bf16)
    h_ref[...] = b_s[tt - SUBLANE:tt, :]


def _lru_specs(layer, idx):
    return [pl.BlockSpec((None, LRU_CONV_K, CW), lambda *g: (layer, 0, idx(*g))),
            pl.BlockSpec((None, 1, CW), lambda *g: (layer, 0, idx(*g))),
            pl.BlockSpec((None, LRU_BLOCKS_PER_TILE, LRU_BLOCK_DIM, LRU_BLOCK_DIM),
                         lambda *g: (layer, idx(*g), 0, 0)),
            pl.BlockSpec((None, 1, CW), lambda *g: (layer, 0, idx(*g))),
            pl.BlockSpec((None, LRU_BLOCKS_PER_TILE, LRU_BLOCK_DIM, LRU_BLOCK_DIM),
                         lambda *g: (layer, idx(*g), 0, 0)),
            pl.BlockSpec((None, 1, CW), lambda *g: (layer, 0, idx(*g))),
            pl.BlockSpec((None, 1, CW), lambda *g: (layer, 0, idx(*g)))]


def _lru_prompt(y_init, z, lru_w, layer):
    nt = SEQ // TT_LRU
    return pl.pallas_call(
        _lru_p_kernel,
        grid=(BATCH, LRU_WIDTH // CW, nt),
        in_specs=[pl.BlockSpec(memory_space=pl.ANY),
                  pl.BlockSpec((TT_LRU, CW), lambda b, c, t: (b * nt + t, OFF_XL // CW + c))]
                 + _lru_specs(layer, lambda b, c, t: c),
        out_specs=[pl.BlockSpec((TT_LRU, CW), lambda b, c, t: (b * nt + t, c)),
                   pl.BlockSpec((None, SUBLANE, CW), lambda b, c, t: (b, 0, c))],
        out_shape=[jax.ShapeDtypeStruct((ROWS, LRU_WIDTH), bf16),
                   jax.ShapeDtypeStruct((BATCH, SUBLANE, LRU_WIDTH), f32)],
        input_output_aliases={0: 0},
        scratch_shapes=[pltpu.VMEM((TT_LRU + SUBLANE, CW), f32),
                        pltpu.VMEM((CW, CW), bf16), pltpu.VMEM((CW, CW), bf16),
                        pltpu.VMEM((TT_LRU, CW), f32), pltpu.VMEM((TT_LRU, CW), f32),
                        pltpu.VMEM((1, CW), f32)],
        compiler_params=_params(3, 40),
        name="lru_prompt",
    )(y_init, z, *lru_w)


def _seq_s_kernel(cb_ref, cc_ref, ch_ref, xl_ref, sc_ref, lc_ref, h0_ref, wsc_ref,
                  wc_ref, bc_ref, wa_ref, ba_ref, wx_ref, bx_ref, lam_ref,
                  yc_ref, yl_ref, scn_ref, hn_ref, wa_bd, wx_bd):
    nb = DEC_BATCH
    rows = lambda v, t: v[nb * t:nb * (t + 1), :]
    u = cc_ref[...] * ch_ref[...]
    cb = cb_ref[...]
    up = [sc_ref[j] for j in range(SCONV_K - 1)] + [rows(u, t) for t in range(DEC_SEQ)]
    w = wsc_ref[...]
    for t in range(DEC_SEQ):
        yc = up[t] * w[0:1]
        for j in range(1, SCONV_K):
            yc = yc + up[t + j] * w[j:j + 1]
        yc_ref[nb * t:nb * (t + 1), :] = (rows(cb, t) * yc).astype(bf16)
    for j in range(SCONV_K - 1):
        scn_ref[j] = up[DEC_SEQ + j]

    _build_block_diag(wa_ref, wa_bd)
    _build_block_diag(wx_ref, wx_bd)
    xl = xl_ref[...]
    xp = [lc_ref[j] for j in range(LRU_CONV_K - 1)] + [rows(xl, t) for t in range(DEC_SEQ)]
    wl = wc_ref[...]
    xcs = []
    for t in range(DEC_SEQ):
        xc = xp[t] * wl[0:1]
        for j in range(1, LRU_CONV_K):
            xc = xc + xp[t + j] * wl[j:j + 1]
        xcs.append(xc + bc_ref[...])
    xc = jnp.concatenate(xcs, axis=0)
    a, bx = _lru_gates(xc, wa_bd[...], wx_bd[...], ba_ref[...], bx_ref[...], lam_ref[...])
    h = h0_ref[...]
    for t in range(DEC_SEQ):
        h = rows(a, t) * h + rows(bx, t)
        yl_ref[nb * t:nb * (t + 1), :] = h.astype(bf16)
    hn_ref[...] = h


def _seq_sample(z, st_sconv_t, st_lconv_t, st_h, w_sconv, lru_w, layer):
    rblk = ROWS_P // ROWS_S
    zspec = lambda off: pl.BlockSpec((ROWS_S, CW), lambda c: (rblk, off // CW + c))
    return pl.pallas_call(
        _seq_s_kernel,
        grid=(CONV_WIDTH // CW,),
        in_specs=[zspec(OFF_CB), zspec(OFF_CC), zspec(OFF_CH), zspec(OFF_XL),
                  pl.BlockSpec((None, SCONV_K - 1, DEC_BATCH, CW), lambda c: (layer, 0, 0, c)),
                  pl.BlockSpec((None, LRU_CONV_K - 1, DEC_BATCH, CW), lambda c: (layer, 0, 0, c)),
                  pl.BlockSpec((None, DEC_BATCH, CW), lambda c: (layer, 0, c)),
                  pl.BlockSpec((None, SCONV_K, CW), lambda c: (layer, 0, c))]
                 + _lru_specs(layer, lambda c: c),
        out_specs=[pl.BlockSpec((ROWS_S, CW), lambda c: (0, c)),
                   pl.BlockSpec((ROWS_S, CW), lambda c: (0, c)),
                   pl.BlockSpec((SCONV_K - 1, DEC_BATCH, CW), lambda c: (0, 0, c)),
                   pl.BlockSpec((DEC_BATCH, CW), lambda c: (0, c))],
        out_shape=[jax.ShapeDtypeStruct((ROWS_S, CONV_WIDTH), bf16),
                   jax.ShapeDtypeStruct((ROWS_S, LRU_WIDTH), bf16),
                   jax.ShapeDtypeStruct((SCONV_K - 1, DEC_BATCH, CONV_WIDTH), f32),
                   jax.ShapeDtypeStruct((DEC_BATCH, LRU_WIDTH), f32)],
        scratch_shapes=[pltpu.VMEM((CW, CW), bf16), pltpu.VMEM((CW, CW), bf16)],
        compiler_params=_params(1, 32),
        name="seq_sample",
    )(z, z, z, z, st_sconv_t, st_lconv_t, st_h, w_sconv, *lru_w)


def _softmax_rows(s):
    m = jnp.max(s, axis=-1, keepdims=True)
    p = jnp.exp(s - m)
    return p * (1.0 / jnp.sum(p, axis=-1, keepdims=True))


def _mem_p_kernel(y_init, q_ref, k_ref, v_ref, y_ref):
    del y_init
    s = lax.dot_general(q_ref[...].astype(bf16), k_ref[...].astype(bf16), NT_DIMS,
                        preferred_element_type=f32) * (MEM_HEAD_DIM ** -0.5)
    pn = _softmax_rows(s).astype(bf16)
    y_ref[...] = jnp.dot(pn, v_ref[...].astype(bf16), preferred_element_type=f32).astype(bf16)


def _mem_prompt(y_init, z, mkv):
    nt = SEQ // TQ_MEM
    hw = MEM_HEAD_DIM
    return pl.pallas_call(
        _mem_p_kernel,
        grid=(BATCH, MEM_HEADS, nt),
        in_specs=[pl.BlockSpec(memory_space=pl.ANY),
                  pl.BlockSpec((TQ_MEM, hw), lambda b, h, t: (b * nt + t, OFF_QM // hw + h)),
                  pl.BlockSpec((MEM_LEN, hw), lambda b, h, t: (b, h)),
                  pl.BlockSpec((MEM_LEN, hw), lambda b, h, t: (b, MEM_HEADS + h))],
        out_specs=pl.BlockSpec((TQ_MEM, hw), lambda b, h, t: (b * nt + t, h)),
        out_shape=jax.ShapeDtypeStruct((ROWS, MEM_WIDTH), bf16),
        input_output_aliases={0: 0},
        compiler_params=_params(3, 32),
        name="mem_prompt",
    )(y_init, z, mkv, mkv)


SB_MEM = 4
QEXP = MEM_HEADS * DEC_SEQ


def _mem_s_kernel(q_ref, k_ref, v_ref, y_ref, qexp):
    hw = MEM_HEAD_DIM
    qexp[...] = jnp.zeros(qexp.shape, f32)
    for bb in range(SB_MEM):
        qb = q_ref[bb]
        for h in range(MEM_HEADS):
            qexp[DEC_SEQ * h:DEC_SEQ * (h + 1), hw * h:hw * (h + 1)] = qb[:, hw * h:hw * (h + 1)]
        s = lax.dot_general(qexp[...].astype(bf16), k_ref[bb].astype(bf16), NT_DIMS,
                            preferred_element_type=f32) * (MEM_HEAD_DIM ** -0.5)
        pn = _softmax_rows(s).astype(bf16)
        o = jnp.dot(pn, v_ref[bb].astype(bf16), preferred_element_type=f32)
        for h in range(MEM_HEADS):
            y_ref[bb, :, hw * h:hw * (h + 1)] = o[DEC_SEQ * h:DEC_SEQ * (h + 1), hw * h:hw * (h + 1)]


def _mem_sample(qm3, cache_k, cache_v, layer):
    return pl.pallas_call(
        _mem_s_kernel,
        grid=(DEC_BATCH // SB_MEM,),
        in_specs=[pl.BlockSpec((SB_MEM, DEC_SEQ, MEM_WIDTH), lambda i: (i, 0, 0)),
                  pl.BlockSpec((None, SB_MEM, MEM_LEN, MEM_WIDTH), lambda i: (layer, i, 0, 0)),
                  pl.BlockSpec((None, SB_MEM, MEM_LEN, MEM_WIDTH), lambda i: (layer, i, 0, 0))],
        out_specs=pl.BlockSpec((SB_MEM, DEC_SEQ, MEM_WIDTH), lambda i: (i, 0, 0)),
        out_shape=jax.ShapeDtypeStruct((DEC_BATCH, DEC_SEQ, MEM_WIDTH), f32),
        scratch_shapes=[pltpu.VMEM((QEXP, MEM_WIDTH), f32)],
        compiler_params=_params(1, 40),
        name="mem_sample",
    )(qm3, cache_k, cache_v)


def _merge_kernel(y0, y1, y2, y3, g0, g1, g2, g3, wb_ref, o_ref):
    acc = None
    for b, (y, g) in enumerate(((y0, g0), (y1, g1), (y2, g2), (y3, g3))):
        proj = jnp.dot(y[...], wb_ref[b], preferred_element_type=f32)
        term = jax.nn.sigmoid(g[...]) * proj
        acc = term if acc is None else acc + term
    o_ref[...] = acc.astype(bf16)


def _merge(ys, zg, w_branch, layer):
    tn = TN_MERGE
    yspec = pl.BlockSpec((TM, BRANCH_WIDTH), lambda i, j: (i, 0))
    gspec = lambda b: pl.BlockSpec((TM, tn), lambda i, j: (i, (b * D_MODEL) // tn + j))
    return pl.pallas_call(
        _merge_kernel,
        grid=(zg.shape[0] // TM, D_MODEL // tn),
        in_specs=[yspec] * N_BRANCH + [gspec(b) for b in range(N_BRANCH)]
                 + [pl.BlockSpec((None, N_BRANCH, BRANCH_WIDTH, tn), lambda i, j: (layer, 0, 0, j))],
        out_specs=pl.BlockSpec((TM, tn), lambda i, j: (i, j)),
        out_shape=jax.ShapeDtypeStruct((zg.shape[0], D_MODEL), bf16),
        compiler_params=_params(2, 48),
        name="merge",
    )(*ys, zg, zg, zg, zg, w_branch)


def _oproj_kernel(m_ref, w_ref, x_ref, g_ref, b_ref, o_ref):
    d = jnp.dot(m_ref[...], w_ref[...], preferred_element_type=f32)
    o_ref[...] = _layer_norm(ALPHA * x_ref[...] + d, g_ref[0:1, :], b_ref[0:1, :])


def _oproj(merged, w_o, x, ln_g, ln_b, layer):
    tm = TM_OPROJ
    return pl.pallas_call(
        _oproj_kernel,
        grid=(x.shape[0] // tm,),
        in_specs=[pl.BlockSpec((tm, D_MODEL), lambda i: (i, 0)),
                  pl.BlockSpec((None, D_MODEL, D_MODEL), lambda i: (layer, 0, 0)),
                  pl.BlockSpec((tm, D_MODEL), lambda i: (i, 0)),
                  pl.BlockSpec((None, 2, D_MODEL), lambda i: (layer, 0, 0)),
                  pl.BlockSpec((None, 2, D_MODEL), lambda i: (layer, 0, 0))],
        out_specs=pl.BlockSpec((tm, D_MODEL), lambda i: (i, 0)),
        out_shape=jax.ShapeDtypeStruct(x.shape, f32),
        compiler_params=_params(1, 48),
        name="oproj_ln",
    )(merged, w_o, x, ln_g, ln_b)


def _swiglu_partial(xb, wg_ref, wu_ref, wd_ref):
    g = jnp.dot(xb, wg_ref[...], preferred_element_type=f32)
    u = jnp.dot(xb, wu_ref[...], preferred_element_type=f32)
    h = (jax.nn.silu(g) * u).astype(bf16)
    return jnp.dot(h, wd_ref[...], preferred_element_type=f32)


def _ffn_kernel(x_ref, wg_ref, wu_ref, wd_ref, g_ref, b_ref, o_ref, xb_ref):
    f = pl.program_id(1)

    @pl.when(f == 0)
    def _():
        xb_ref[...] = x_ref[...].astype(bf16)
        o_ref[...] = jnp.zeros(o_ref.shape, f32)

    o_ref[...] += _swiglu_partial(xb_ref[...], wg_ref, wu_ref, wd_ref)

    @pl.when(f == pl.num_programs(1) - 1)
    def _():
        o_ref[...] = _layer_norm(ALPHA * x_ref[...] + o_ref[...], g_ref[1:2, :], b_ref[1:2, :])


def _ffn(x, wg, wu, wd, ln_g, ln_b, layer, j):
    tm = TM_FFN
    w_in = pl.BlockSpec((None, D_MODEL, TF), lambda i, f: (j, 0, f))
    w_out = pl.BlockSpec((None, TF, D_MODEL), lambda i, f: (j, f, 0))
    xspec = pl.BlockSpec((tm, D_MODEL), lambda i, f: (i, 0))
    lnspec = pl.BlockSpec((None, 2, D_MODEL), lambda i, f: (layer, 0, 0))
    return pl.pallas_call(
        _ffn_kernel,
        grid=(x.shape[0] // tm, D_FF // TF),
        in_specs=[xspec, w_in, w_in, w_out, lnspec, lnspec],
        out_specs=xspec,
        out_shape=jax.ShapeDtypeStruct(x.shape, f32),
        scratch_shapes=[pltpu.VMEM((tm, D_MODEL), bf16)],
        compiler_params=_params(2, 52),
        name="ffn_dense",
    )(x, wg, wu, wd, ln_g, ln_b)


TG = 640


def _max_row_tiles(n_tokens):
    return (n_tokens * TOP_K + N_EXPERTS * (TG - 1)) // TG


def _router_kernel(x_ref, w_ref, p_ref, i_ref, cnt_ref, carry):
    @pl.when(pl.program_id(0) == 0)
    def _():
        carry[...] = jnp.zeros(carry.shape, f32)

    tm = x_ref.shape[0]
    logits = jnp.dot(x_ref[...], w_ref[...], precision=lax.Precision.HIGHEST,
                     preferred_element_type=f32)
    lane = lax.broadcasted_iota(jnp.int32, logits.shape, 1)
    logits = jnp.where(lane < N_EXPERTS, logits, -jnp.inf)
    m1 = jnp.max(logits, axis=-1, keepdims=True)
    i1 = jnp.min(jnp.where(logits == m1, lane, LANE), axis=-1, keepdims=True)
    rest = jnp.where(lane == i1, -jnp.inf, logits)
    m2 = jnp.max(rest, axis=-1, keepdims=True)
    i2 = jnp.min(jnp.where(rest == m2, lane, LANE), axis=-1, keepdims=True)
    e = jnp.exp(m2 - m1)
    p1 = 1.0 / (1.0 + e)
    p2 = e / (1.0 + e)
    sel = jnp.where(lane == i1, 1.0, jnp.where(lane == i2, 1.0, 0.0))
    r_i = lax.broadcasted_iota(jnp.int32, (tm, tm), 0)
    c_i = lax.broadcasted_iota(jnp.int32, (tm, tm), 1)
    tri = jnp.where(r_i > c_i, 1.0, 0.0).astype(bf16)
    before = jnp.dot(tri, sel.astype(bf16), preferred_element_type=f32) + carry[...]
    r1 = jnp.sum(jnp.where(lane == i1, before, 0.0), axis=-1, keepdims=True).astype(jnp.int32)
    r2 = jnp.sum(jnp.where(lane == i2, before, 0.0), axis=-1, keepdims=True).astype(jnp.int32)
    carry[...] = carry[...] + jnp.sum(sel, axis=0, keepdims=True)
    cnt_ref[...] = carry[...]
    p_ref[...] = jnp.where(lane == 0, p1, jnp.where(lane == 1, p2, 0.0))
    i_ref[...] = jnp.where(lane == 0, i1, jnp.where(lane == 1, i2,
                           jnp.where(lane == 2, r1, jnp.where(lane == 3, r2, 0))))


def _router(x, w_router_pad, j):
    n = x.shape[0]
    tm = TM_FFN
    return pl.pallas_call(
        _router_kernel,
        grid=(n // tm,),
        in_specs=[pl.BlockSpec((tm, D_MODEL), lambda i: (i, 0)),
                  pl.BlockSpec((None, D_MODEL, LANE), lambda i: (j, 0, 0))],
        out_specs=[pl.BlockSpec((tm, LANE), lambda i: (i, 0)),
                   pl.BlockSpec((tm, LANE), lambda i: (i, 0)),
                   pl.BlockSpec((1, LANE), lambda i: (0, 0))],
        out_shape=[jax.ShapeDtypeStruct((n, LANE), f32),
                   jax.ShapeDtypeStruct((n, LANE), jnp.int32),
                   jax.ShapeDtypeStruct((1, LANE), f32)],
        scratch_shapes=[pltpu.VMEM((1, LANE), f32)],
        compiler_params=_params(1, 40),
        name="router",
    )(x, w_router_pad)


def _routing_plan(iinfo, counts):
    cnt = counts[0, :N_EXPERTS].astype(jnp.int32)
    tiles = (cnt + (TG - 1)) // TG
    end_tile = jnp.cumsum(tiles)
    offset = (end_tile - tiles) * TG
    pos1 = offset[iinfo[:, 0]] + iinfo[:, 2]
    pos2 = offset[iinfo[:, 1]] + iinfo[:, 3]
    tile_ids = jnp.arange(_max_row_tiles(iinfo.shape[0]), dtype=jnp.int32)
    tile_expert = jnp.minimum(jnp.sum(tile_ids[:, None] >= end_tile[None, :], axis=1),
                              N_EXPERTS - 1).astype(jnp.int32)
    return pos1, pos2, tile_expert, end_tile[N_EXPERTS - 1:].astype(jnp.int32)


DMA_UNROLL = 8


def _row_copy(src, s, dst, d, sem):
    return pltpu.make_async_copy(src.at[pl.ds(s, 1)], dst.at[pl.ds(d, 1)], sem)


def _dispatch_kernel(pos1_ref, pos2_ref, x_ref, xs_in, xs_ref, sem):
    del xs_in
    tm = x_ref.shape[0]
    base = pl.program_id(0) * tm

    def issue(r, c):
        _row_copy(x_ref, r, xs_ref, pos1_ref[base + r], sem).start()
        _row_copy(x_ref, r, xs_ref, pos2_ref[base + r], sem).start()
        return c

    def drain(r, c):
        _row_copy(x_ref, r, xs_ref, pos1_ref[base + r], sem).wait()
        _row_copy(x_ref, r, xs_ref, pos2_ref[base + r], sem).wait()
        return c

    lax.fori_loop(0, tm, issue, 0, unroll=DMA_UNROLL)
    lax.fori_loop(0, tm, drain, 0, unroll=DMA_UNROLL)


def _dispatch(x, pos1, pos2):
    tm = TM_FFN
    n_rows = _max_row_tiles(x.shape[0]) * TG
    xs0 = jnp.zeros((n_rows, D_MODEL), f32)
    return pl.pallas_call(
        _dispatch_kernel,
        grid_spec=pltpu.PrefetchScalarGridSpec(
            num_scalar_prefetch=2,
            grid=(x.shape[0] // tm,),
            in_specs=[pl.BlockSpec((tm, D_MODEL), lambda i, p1, p2: (i, 0)),
                      pl.BlockSpec(memory_space=pl.ANY)],
            out_specs=pl.BlockSpec(memory_space=pl.ANY),
            scratch_shapes=[pltpu.SemaphoreType.DMA(())]),
        out_shape=jax.ShapeDtypeStruct((n_rows, D_MODEL), f32),
        input_output_aliases={3: 0},
        compiler_params=_params(1, 32),
        name="moe_dispatch",
    )(pos1, pos2, x, xs0)


def _gffn_kernel(te_ref, nt_ref, x_ref, wg_ref, wu_ref, wd_ref, o_ref, xb_ref):
    del te_ref
    f = pl.program_id(1)
    used = pl.program_id(0) < nt_ref[0]

    @pl.when(jnp.logical_and(jnp.logical_not(used), f == 0))
    def _():
        o_ref[...] = jnp.zeros(o_ref.shape, f32)

    @pl.when(jnp.logical_and(used, f == 0))
    def _():
        xb_ref[...] = x_ref[...].astype(bf16)
        o_ref[...] = jnp.zeros(o_ref.shape, f32)

    @pl.when(used)
    def _():
        o_ref[...] += _swiglu_partial(xb_ref[...], wg_ref, wu_ref, wd_ref)


def _grouped_ffn(xs, tile_expert, n_tiles, wg, wu, wd, j):
    n_f = D_FF // TF
    row = lambda i, nt: jnp.minimum(i, nt[0] - 1)
    fcol = lambda i, f, nt: jnp.where(i < nt[0], f, n_f - 1)
    xspec = pl.BlockSpec((TG, D_MODEL), lambda i, f, te, nt: (row(i, nt), 0))
    w_in = pl.BlockSpec((None, None, D_MODEL, TF),
                        lambda i, f, te, nt: (j, te[row(i, nt)], 0, fcol(i, f, nt)))
    w_out = pl.BlockSpec((None, None, TF, D_MODEL),
                         lambda i, f, te, nt: (j, te[row(i, nt)], fcol(i, f, nt), 0))
    return pl.pallas_call(
        _gffn_kernel,
        grid_spec=pltpu.PrefetchScalarGridSpec(
            num_scalar_prefetch=2,
            grid=(xs.shape[0] // TG, n_f),
            in_specs=[xspec, w_in, w_in, w_out],
            out_specs=pl.BlockSpec((TG, D_MODEL), lambda i, f, te, nt: (i, 0)),
            scratch_shapes=[pltpu.VMEM((TG, D_MODEL), bf16)]),
        out_shape=jax.ShapeDtypeStruct(xs.shape, f32),
        compiler_params=_params(2, 52),
        name="ffn_experts",
    )(tile_expert, n_tiles, xs, wg, wu, wd)


def _combine_kernel(pos1_ref, pos2_ref, x_ref, p_ref, y_ref, g_ref, b_ref, o_ref, buf1, buf2, sem):
    tm = x_ref.shape[0]
    base = pl.program_id(0) * tm

    def issue(r, c):
        _row_copy(y_ref, pos1_ref[base + r], buf1, r, sem).start()
        _row_copy(y_ref, pos2_ref[base + r], buf2, r, sem).start()
        return c

    def drain(r, c):
        _row_copy(y_ref, pos1_ref[base + r], buf1, r, sem).wait()
        _row_copy(y_ref, pos2_ref[base + r], buf2, r, sem).wait()
        return c

    lax.fori_loop(0, tm, issue, 0, unroll=DMA_UNROLL)
    lax.fori_loop(0, tm, drain, 0, unroll=DMA_UNROLL)
    p = p_ref[...]
    mixed = p[:, 0:1] * buf1[...] + p[:, 1:2] * buf2[...]
    o_ref[...] = _layer_norm(ALPHA * x_ref[...] + mixed, g_ref[1:2, :], b_ref[1:2, :])


def _combine(x, pinfo, y, pos1, pos2, ln_g, ln_b, layer):
    tm = TM_FFN
    xspec = pl.BlockSpec((tm, D_MODEL), lambda i, p1, p2: (i, 0))
    lnspec = pl.BlockSpec((None, 2, D_MODEL), lambda i, p1, p2: (layer, 0, 0))
    return pl.pallas_call(
        _combine_kernel,
        grid_spec=pltpu.PrefetchScalarGridSpec(
            num_scalar_prefetch=2,
            grid=(x.shape[0] // tm,),
            in_specs=[xspec, pl.BlockSpec((tm, LANE), lambda i, p1, p2: (i, 0)),
                      pl.BlockSpec(memory_space=pl.ANY), lnspec, lnspec],
            out_specs=xspec,
            scratch_shapes=[pltpu.VMEM((tm, D_MODEL), f32), pltpu.VMEM((tm, D_MODEL), f32),
                            pltpu.SemaphoreType.DMA(())]),
        out_shape=jax.ShapeDtypeStruct(x.shape, f32),
        compiler_params=_params(1, 48),
        name="moe_combine",
    )(pos1, pos2, x, pinfo, y, ln_g, ln_b)


def _moe(x, w_router_pad, wg, wu, wd, ln_g, ln_b, layer, j):
    pinfo, iinfo, counts = _router(x, w_router_pad, j)
    pos1, pos2, tile_expert, n_tiles = _routing_plan(iinfo, counts)
    xs = _dispatch(x, pos1, pos2)
    y = _grouped_ffn(xs, tile_expert, n_tiles, wg, wu, wd, j)
    return _combine(x, pinfo, y, pos1, pos2, ln_g, ln_b, layer)


def kernel(x_prompt, x_sample, mem_prompt, cache_win_k, cache_win_v, cache_mem_k, cache_mem_v,
           state_sconv, state_lru_conv, state_lru_h, w_mix_in, sinks, w_sconv, w_lru_conv, b_lru_conv,
           w_lru_a, b_lru_a, w_lru_x, b_lru_x, lru_lambda, w_mem_kv, w_branch, w_o, ln_g, ln_b,
           w_router, w_ffn_gate, w_ffn_up, w_ffn_down, w_exp_gate, w_exp_up, w_exp_down):
    x = jnp.concatenate([x_prompt.reshape(ROWS_P, D_MODEL),
                         x_sample.transpose(1, 0, 2).reshape(ROWS_S, D_MODEL)], axis=0)
    mem = mem_prompt.reshape(BATCH * MEM_LEN, D_MODEL)
    w_mix_b = w_mix_in[:, :, :OFF_G].astype(bf16)
    w_gate_b = w_mix_in[:, :, OFF_G:].astype(bf16)
    w_mkv_b = w_mem_kv.astype(bf16)
    w_br_b = w_branch.astype(bf16)
    w_o_b = w_o.astype(bf16)
    wfg, wfu, wfd = w_ffn_gate.astype(bf16), w_ffn_up.astype(bf16), w_ffn_down.astype(bf16)
    weg, weu, wed = w_exp_gate.astype(bf16), w_exp_up.astype(bf16), w_exp_down.astype(bf16)
    w_router_pad = jnp.pad(w_router, ((0, 0), (0, 0), (0, LANE - N_EXPERTS)))
    ck = cache_win_k.reshape(DEPTH, DEC_BATCH, WINDOW, KV_WIDTH)
    cv = cache_win_v.reshape(DEPTH, DEC_BATCH, WINDOW, KV_WIDTH)
    cmk = cache_mem_k.reshape(DEPTH, DEC_BATCH, MEM_LEN, MEM_WIDTH)
    cmv = cache_mem_v.reshape(DEPTH, DEC_BATCH, MEM_LEN, MEM_WIDTH)
    st_sc_t = state_sconv.transpose(0, 2, 1, 3)
    st_lc_t = state_lru_conv.transpose(0, 2, 1, 3)
    lru_w = (w_lru_conv, b_lru_conv.reshape(DEPTH, 1, LRU_WIDTH), w_lru_a,
             b_lru_a.reshape(DEPTH, 1, LRU_WIDTH), w_lru_x, b_lru_x.reshape(DEPTH, 1, LRU_WIDTH),
             lru_lambda.reshape(DEPTH, 1, LRU_WIDTH))
    tab_p = _rope_tables(jnp.arange(SEQ, dtype=jnp.int32))
    tab_s = _rope_tables(PAST_LEN + jnp.arange(DEC_SEQ, dtype=jnp.int32))

    def to_b_major(v):
        return v.reshape(DEC_SEQ, DEC_BATCH, v.shape[-1]).transpose(1, 0, 2)

    def to_t_major(v):
        return v.transpose(1, 0, 2).reshape(ROWS_S, v.shape[-1])

    outs = [[] for _ in range(12)]
    for l in range(DEPTH):
        z = _matmul(x, w_mix_b, l, TM, TN_MIX)
        zg = _matmul(x, w_gate_b, l, TM, TN_GATE)
        mkv = _matmul(mem, w_mkv_b, l, BATCH * MEM_LEN, TN_MKV)
        zs = z[ROWS_P:]

        def all_rows(y_s):
            return lax.dynamic_update_slice(jnp.zeros((ROWS, y_s.shape[-1]), bf16), y_s, (ROWS_P, 0))

        ya_s, nk_s, nv_s = _attn_sample(to_b_major(zs[:, :OFF_CB]), ck, cv, sinks, tab_s, l)
        yc_s, yl_s, sc_s, h_s = _seq_sample(z, st_sc_t, st_lc_t, state_lru_h, w_sconv, lru_w, l)
        ym_s = _mem_sample(to_b_major(zs[:, OFF_QM:OFF_G]), cmk, cmv, l)
        ya, krot_p = _attn_prompt(all_rows(to_t_major(ya_s).astype(bf16)), z, sinks, tab_p, l)
        yc, sc_p = _sconv_prompt(all_rows(yc_s), z, w_sconv, l)
        yl, h_p = _lru_prompt(all_rows(yl_s), z, lru_w, l)
        ym = _mem_prompt(all_rows(to_t_major(ym_s).astype(bf16)), z, mkv)
        merged = _merge((ya, yc, yl, ym), zg, w_br_b, l)
        x = _oproj(merged, w_o_b, x, ln_g, ln_b, l)
        j = l // 2
        if l % 2 == 0:
            x = _ffn(x, wfg, wfu, wfd, ln_g, ln_b, l, j)
        else:
            x = _moe(x, w_router_pad, weg, weu, wed, ln_g, ln_b, l, j)

        def tail(nrows, lo, hi):
            return jnp.stack([z[(b + 1) * SEQ - nrows:(b + 1) * SEQ, lo:hi] for b in range(BATCH)])

        kv_shape = (BATCH, WINDOW, KV_HEADS, HEAD_DIM)
        outs[0].append(krot_p.reshape(BATCH, SEQ, KV_WIDTH)[:, SEQ - WINDOW:].reshape(kv_shape))
        outs[1].append(tail(WINDOW, OFF_V, OFF_CB).reshape(kv_shape))
        mem_shape = (BATCH, MEM_LEN, MEM_HEADS, MEM_HEAD_DIM)
        outs[2].append(mkv[:, :MEM_WIDTH].reshape(mem_shape))
        outs[3].append(mkv[:, MEM_WIDTH:].reshape(mem_shape))
        outs[4].append(sc_p[:, SUBLANE - (SCONV_K - 1):])
        outs[5].append(tail(LRU_CONV_K - 1, OFF_XL, OFF_QM))
        outs[6].append(h_p[:, SUBLANE - 1])
        kvs_shape = (DEC_BATCH, WINDOW, KV_HEADS, HEAD_DIM)
        outs[7].append(nk_s.reshape(kvs_shape))
        outs[8].append(nv_s.reshape(kvs_shape))
        outs[9].append(sc_s.transpose(1, 0, 2))
        outs[10].append(zs[DEC_BATCH:, OFF_XL:OFF_QM].reshape(LRU_CONV_K - 1, DEC_BATCH, LRU_WIDTH
                                                              ).transpose(1, 0, 2))
        outs[11].append(h_s)

    y_prompt = x[:ROWS_P].reshape(BATCH, SEQ, D_MODEL)
    y_sample = to_b_major(x[ROWS_P:])
    return (y_prompt, y_sample) + tuple(jnp.stack(o) for o in outs)
```

```python
import functools

import jax
import jax.numpy as jnp
from jax import lax
from jax.experimental import pallas as pl
from jax.experimental.pallas import tpu as pltpu

f32 = jnp.float32
bf16 = jnp.bfloat16

D_MODEL = 2048
BATCH = 2
SEQ = 4096
DEPTH = 4
DEC_BATCH = 32
DEC_SEQ = 4
PAST_LEN = 16384
N_HEADS = 16
KV_HEADS = 4
GQA_GROUP = N_HEADS // KV_HEADS
HEAD_DIM = 64
ROPE_DIM = HEAD_DIM // 4
ROPE_THETA = 500000.0
WINDOW = 128
ATTN_WIDTH = N_HEADS * HEAD_DIM
KV_WIDTH = KV_HEADS * HEAD_DIM
CONV_WIDTH = D_MODEL // 2
SCONV_K = 3
LRU_WIDTH = D_MODEL // 2
LRU_BLOCKS = 16
LRU_BLOCK_DIM = LRU_WIDTH // LRU_BLOCKS
LRU_CONV_K = 4
LRU_C = 8.0
MEM_LEN = 256
MEM_HEADS = 4
MEM_HEAD_DIM = 256
MEM_WIDTH = MEM_HEADS * MEM_HEAD_DIM
N_BRANCH = 4
BRANCH_WIDTH = D_MODEL // 2
D_FF = 5632
N_EXPERTS = 8
TOP_K = 2
ALPHA = (2 * DEPTH) ** 0.25
LN_EPS = 1e-5
NEG_INF = -1e30

ROWS_P = BATCH * SEQ
ROWS_S = DEC_BATCH * DEC_SEQ
ROWS = ROWS_P + ROWS_S

OFF_Q = 0
OFF_K = OFF_Q + ATTN_WIDTH
OFF_V = OFF_K + KV_WIDTH
OFF_CB = OFF_V + KV_WIDTH
OFF_CC = OFF_CB + CONV_WIDTH
OFF_CH = OFF_CC + CONV_WIDTH
OFF_XL = OFF_CH + CONV_WIDTH
OFF_QM = OFF_XL + LRU_WIDTH
OFF_G = OFF_QM + MEM_WIDTH
MIX_IN = OFF_G + N_BRANCH * D_MODEL

LANE = 128
SUBLANE = 8
MIB = 1024 * 1024
TM = 832
TN_MIX = 1664
TN_GATE = 1024
TN_MKV = 512
TN_MERGE = 512
TM_OPROJ = 416
TM_FFN = 640
TF = 512
CW = 512
TT_CONV = 1024
TT_LRU = 512
TQ_MEM = 2048

NT_DIMS = (((1,), (1,)), ((), ()))


def _params(n_axes, vmem_mib):
    return pltpu.CompilerParams(dimension_semantics=("arbitrary",) * n_axes,
                                vmem_limit_bytes=vmem_mib * MIB)


def _layer_norm(v, g, b):
    mu = jnp.mean(v, axis=-1, keepdims=True)
    vc = v - mu
    var = jnp.mean(vc * vc, axis=-1, keepdims=True)
    return vc * lax.rsqrt(var + LN_EPS) * g + b


def _expm1(x):
    fact = [1.0]
    for k in range(1, 10):
        fact.append(fact[-1] * k)
    p = 1.0 / fact[9]
    for k in range(8, 0, -1):
        p = p * x + 1.0 / fact[k]
    return jnp.where(jnp.abs(x) < 0.5, x * p, jnp.exp(x) - 1.0)


def _mm_kernel(x_ref, w_ref, o_ref, xb_ref):
    @pl.when(pl.program_id(1) == 0)
    def _():
        xb_ref[...] = x_ref[...].astype(bf16)

    o_ref[...] = jnp.dot(xb_ref[...], w_ref[...], preferred_element_type=f32)


def _matmul(x, w_stack, layer, tm, tn, n=None):
    m, k = x.shape
    n = w_stack.shape[-1] if n is None else n
    vmem = 2 * (tm * k * 4 + k * tn * 2 + tm * tn * 4) + tm * k * 2
    return pl.pallas_call(
        _mm_kernel,
        grid=(m // tm, n // tn),
        in_specs=[pl.BlockSpec((tm, k), lambda i, j: (i, 0)),
                  pl.BlockSpec((None, k, tn), lambda i, j: (layer, 0, j))],
        out_specs=pl.BlockSpec((tm, tn), lambda i, j: (i, j)),
        out_shape=jax.ShapeDtypeStruct((m, n), f32),
        scratch_shapes=[pltpu.VMEM((tm, k), bf16)],
        compiler_params=_params(2, vmem // MIB + 8),
        name="matmul",
    )(x, w_stack)


def _rope_tables(pos):
    half = ROPE_DIM // 2
    inv_freq = ROPE_THETA ** (-jnp.arange(half, dtype=f32) * (2.0 / ROPE_DIM))
    ang = pos.astype(f32)[:, None] * inv_freq[None, :]
    cos = jnp.cos(ang)
    sin = jnp.sin(ang)
    p = pos.shape[0]
    rest = HEAD_DIM - ROPE_DIM
    c = jnp.concatenate([cos, cos, jnp.ones((p, rest), f32)], axis=1)
    sa = jnp.concatenate([-sin, jnp.zeros((p, HEAD_DIM - half), f32)], axis=1)
    sb = jnp.concatenate([jnp.zeros((p, half), f32), sin, jnp.zeros((p, rest), f32)], axis=1)
    rep = LANE // HEAD_DIM
    return jnp.tile(c, (1, rep)), jnp.tile(sa, (1, rep)), jnp.tile(sb, (1, rep))


def _rope(x, c, sa, sb):
    half = ROPE_DIM // 2
    chunks = []
    for j in range(x.shape[1] // LANE):
        xc = x[:, LANE * j:LANE * (j + 1)]
        chunks.append(xc * c + pltpu.roll(xc, LANE - half, 1) * sa + pltpu.roll(xc, half, 1) * sb)
    return chunks[0] if len(chunks) == 1 else jnp.concatenate(chunks, axis=1)


def _sink_softmax(s, sink):
    m = jnp.maximum(jnp.max(s, axis=-1, keepdims=True), sink)
    p = jnp.exp(s - m)
    denom = jnp.sum(p, axis=-1, keepdims=True) + jnp.exp(sink - m)
    return p * (1.0 / denom)


def _attn_p_kernel(y_init, sinks_ref, q_ref, k_ref, v_ref, c_ref, sa_ref, sb_ref, y_ref, kr_ref,
                   kprev, vprev, *, layer):
    del y_init
    n = pl.program_id(1)

    @pl.when(n == 0)
    def _():
        kprev[...] = jnp.zeros(kprev.shape, f32)
        vprev[...] = jnp.zeros(vprev.shape, f32)

    c, sa, sb = c_ref[...], sa_ref[...], sb_ref[...]
    q = _rope(q_ref[...], c, sa, sb)
    kc = _rope(k_ref[...], c, sa, sb)
    vc = v_ref[...]
    kr_ref[...] = kc
    kall = jnp.concatenate([kprev[...], kc], axis=0).astype(bf16)
    vall = jnp.concatenate([vprev[...], vc], axis=0).astype(bf16)
    kprev[...] = kc
    vprev[...] = vc

    rows = GQA_GROUP * WINDOW
    qi = lax.broadcasted_iota(jnp.int32, (rows, 2 * WINDOW), 0) % WINDOW
    kj = lax.broadcasted_iota(jnp.int32, (rows, 2 * WINDOW), 1)
    kmin = jnp.where(n > 0, 0, WINDOW)
    mask = (kj > qi) & (kj <= qi + WINDOW) & (kj >= kmin)
    rg = lax.broadcasted_iota(jnp.int32, (rows, 1), 0) // WINDOW
    for h in range(KV_HEADS):
        qh = jnp.concatenate(
            [q[:, (GQA_GROUP * h + g) * HEAD_DIM:(GQA_GROUP * h + g + 1) * HEAD_DIM]
             for g in range(GQA_GROUP)], axis=0).astype(bf16)
        kh = kall[:, HEAD_DIM * h:HEAD_DIM * (h + 1)]
        vh = vall[:, HEAD_DIM * h:HEAD_DIM * (h + 1)]
        s = lax.dot_general(qh, kh, NT_DIMS, preferred_element_type=f32) * (HEAD_DIM ** -0.5)
        s = jnp.where(mask, s, NEG_INF)
        sink = jnp.zeros((rows, 1), f32)
        for g in range(GQA_GROUP):
            sink = jnp.where(rg == g, sinks_ref[layer, GQA_GROUP * h + g], sink)
        pn = _sink_softmax(s, sink).astype(bf16)
        o = jnp.dot(pn, vh, preferred_element_type=f32)
        for g in range(GQA_GROUP):
            hd = GQA_GROUP * h + g
            y_ref[:, hd * HEAD_DIM:(hd + 1) * HEAD_DIM] = o[WINDOW * g:WINDOW * (g + 1), :].astype(bf16)


def _attn_prompt(y_init, z, sinks, tables, layer):
    nblk = SEQ // WINDOW
    row = lambda b, n: b * nblk + n
    return pl.pallas_call(
        functools.partial(_attn_p_kernel, layer=layer),
        grid=(BATCH, nblk),
        in_specs=[pl.BlockSpec(memory_space=pl.ANY),
                  pl.BlockSpec(memory_space=pltpu.SMEM),
                  pl.BlockSpec((WINDOW, ATTN_WIDTH), lambda b, n: (row(b, n), OFF_Q // ATTN_WIDTH)),
                  pl.BlockSpec((WINDOW, KV_WIDTH), lambda b, n: (row(b, n), OFF_K // KV_WIDTH)),
                  pl.BlockSpec((WINDOW, KV_WIDTH), lambda b, n: (row(b, n), OFF_V // KV_WIDTH)),
                  pl.BlockSpec((WINDOW, LANE), lambda b, n: (n, 0)),
                  pl.BlockSpec((WINDOW, LANE), lambda b, n: (n, 0)),
                  pl.BlockSpec((WINDOW, LANE), lambda b, n: (n, 0))],
        out_specs=[pl.BlockSpec((WINDOW, ATTN_WIDTH), lambda b, n: (row(b, n), 0)),
                   pl.BlockSpec((WINDOW, KV_WIDTH), lambda b, n: (row(b, n), 0))],
        out_shape=[jax.ShapeDtypeStruct((ROWS, ATTN_WIDTH), bf16),
                   jax.ShapeDtypeStruct((ROWS_P, KV_WIDTH), f32)],
        scratch_shapes=[pltpu.VMEM((WINDOW, KV_WIDTH), f32), pltpu.VMEM((WINDOW, KV_WIDTH), f32)],
        input_output_aliases={0: 0},
        compiler_params=_params(2, 32),
        name="attn_prompt",
    )(y_init, sinks, z, z, z, *tables)


SB_ATTN = 8
KALL = WINDOW + SUBLANE


def _attn_s_kernel(sinks_ref, z_ref, ck_ref, cv_ref, c_ref, sa_ref, sb_ref, y_ref, nk_ref, nv_ref,
                   kall, vall, qs, *, layer):
    c, sa, sb = c_ref[...], sa_ref[...], sb_ref[...]
    rows = GQA_GROUP * DEC_SEQ
    kall[WINDOW:KALL, :] = jnp.zeros((SUBLANE, KV_WIDTH), f32)
    vall[WINDOW:KALL, :] = jnp.zeros((SUBLANE, KV_WIDTH), f32)
    qt = lax.broadcasted_iota(jnp.int32, (rows, KALL), 0) % DEC_SEQ
    kj = lax.broadcasted_iota(jnp.int32, (rows, KALL), 1)
    mask = jnp.where(kj < WINDOW, kj - qt, qt - (kj - WINDOW) + 1) > 0
    rg = lax.broadcasted_iota(jnp.int32, (rows, 1), 0) // DEC_SEQ
    for bb in range(SB_ATTN):
        zb = z_ref[bb]
        q = _rope(zb[:, OFF_Q:OFF_Q + ATTN_WIDTH], c, sa, sb)
        kn = _rope(zb[:, OFF_K:OFF_K + KV_WIDTH], c, sa, sb)
        vn = zb[:, OFF_V:OFF_V + KV_WIDTH]
        kall[0:WINDOW, :] = ck_ref[bb]
        vall[0:WINDOW, :] = cv_ref[bb]
        kall[WINDOW:WINDOW + DEC_SEQ, :] = kn
        vall[WINDOW:WINDOW + DEC_SEQ, :] = vn
        nk_ref[bb] = kall[DEC_SEQ:DEC_SEQ + WINDOW, :]
        nv_ref[bb] = vall[DEC_SEQ:DEC_SEQ + WINDOW, :]
        for h in range(KV_HEADS):
            for g in range(GQA_GROUP):
                hd = GQA_GROUP * h + g
                qs[DEC_SEQ * g:DEC_SEQ * (g + 1), :] = q[:, hd * HEAD_DIM:(hd + 1) * HEAD_DIM]
            qh = qs[...].astype(bf16)
            kh = kall[:, HEAD_DIM * h:HEAD_DIM * (h + 1)].astype(bf16)
            vh = vall[:, HEAD_DIM * h:HEAD_DIM * (h + 1)].astype(bf16)
            s = lax.dot_general(qh, kh, NT_DIMS, preferred_element_type=f32) * (HEAD_DIM ** -0.5)
            s = jnp.where(mask, s, NEG_INF)
            sink = jnp.zeros((rows, 1), f32)
            for g in range(GQA_GROUP):
                sink = jnp.where(rg == g, sinks_ref[layer, GQA_GROUP * h + g], sink)
            pn = _sink_softmax(s, sink).astype(bf16)
            o = jnp.dot(pn, vh, preferred_element_type=f32)
            for g in range(GQA_GROUP):
                hd = GQA_GROUP * h + g
                y_ref[bb, :, hd * HEAD_DIM:(hd + 1) * HEAD_DIM] = o[DEC_SEQ * g:DEC_SEQ * (g + 1), :]


def _attn_sample(zs3, cache_k, cache_v, sinks, tables, layer):
    qkv = ATTN_WIDTH + 2 * KV_WIDTH
    return pl.pallas_call(
        functools.partial(_attn_s_kernel, layer=layer),
        grid=(DEC_BATCH // SB_ATTN,),
        in_specs=[pl.BlockSpec(memory_space=pltpu.SMEM),
                  pl.BlockSpec((SB_ATTN, DEC_SEQ, qkv), lambda i: (i, 0, 0)),
                  pl.BlockSpec((None, SB_ATTN, WINDOW, KV_WIDTH), lambda i: (layer, i, 0, 0)),
                  pl.BlockSpec((None, SB_ATTN, WINDOW, KV_WIDTH), lambda i: (layer, i, 0, 0)),
                  pl.BlockSpec((DEC_SEQ, LANE), lambda i: (0, 0)),
                  pl.BlockSpec((DEC_SEQ, LANE), lambda i: (0, 0)),
                  pl.BlockSpec((DEC_SEQ, LANE), lambda i: (0, 0))],
        out_specs=[pl.BlockSpec((SB_ATTN, DEC_SEQ, ATTN_WIDTH), lambda i: (i, 0, 0)),
                   pl.BlockSpec((SB_ATTN, WINDOW, KV_WIDTH), lambda i: (i, 0, 0)),
                   pl.BlockSpec((SB_ATTN, WINDOW, KV_WIDTH), lambda i: (i, 0, 0))],
        out_shape=[jax.ShapeDtypeStruct((DEC_BATCH, DEC_SEQ, ATTN_WIDTH), f32),
                   jax.ShapeDtypeStruct((DEC_BATCH, WINDOW, KV_WIDTH), f32),
                   jax.ShapeDtypeStruct((DEC_BATCH, WINDOW, KV_WIDTH), f32)],
        scratch_shapes=[pltpu.VMEM((KALL, KV_WIDTH), f32), pltpu.VMEM((KALL, KV_WIDTH), f32),
                        pltpu.VMEM((GQA_GROUP * DEC_SEQ, HEAD_DIM), f32)],
        compiler_params=_params(1, 32),
        name="attn_sample",
    )(sinks, zs3, cache_k, cache_v, *tables)


def _sconv_p_kernel(y_init, cb_ref, cc_ref, ch_ref, w_ref, y_ref, st_ref, ubuf):
    del y_init
    t = pl.program_id(2)
    tt = cc_ref.shape[0]

    @pl.when(t == 0)
    def _():
        ubuf[0:SUBLANE, :] = jnp.zeros((SUBLANE, CW), f32)

    ubuf[SUBLANE:SUBLANE + tt, :] = cc_ref[...] * ch_ref[...]
    w = w_ref[...]
    yc = ubuf[SUBLANE - 2:SUBLANE - 2 + tt, :] * w[0:1]
    yc = yc + ubuf[SUBLANE - 1:SUBLANE - 1 + tt, :] * w[1:2]
    yc = yc + ubuf[SUBLANE:SUBLANE + tt, :] * w[2:3]
    y_ref[...] = (cb_ref[...] * yc).astype(bf16)
    last = ubuf[tt:tt + SUBLANE, :]
    st_ref[...] = last
    ubuf[0:SUBLANE, :] = last


def _sconv_prompt(y_init, z, w_sconv, layer):
    nt = SEQ // TT_CONV
    zspec = lambda off: pl.BlockSpec((TT_CONV, CW), lambda b, c, t: (b * nt + t, off // CW + c))
    return pl.pallas_call(
        _sconv_p_kernel,
        grid=(BATCH, CONV_WIDTH // CW, nt),
        in_specs=[pl.BlockSpec(memory_space=pl.ANY), zspec(OFF_CB), zspec(OFF_CC), zspec(OFF_CH),
                  pl.BlockSpec((None, SCONV_K, CW), lambda b, c, t: (layer, 0, c))],
        out_specs=[pl.BlockSpec((TT_CONV, CW), lambda b, c, t: (b * nt + t, c)),
                   pl.BlockSpec((None, SUBLANE, CW), lambda b, c, t: (b, 0, c))],
        out_shape=[jax.ShapeDtypeStruct((ROWS, CONV_WIDTH), bf16),
                   jax.ShapeDtypeStruct((BATCH, SUBLANE, CONV_WIDTH), f32)],
        scratch_shapes=[pltpu.VMEM((TT_CONV + SUBLANE, CW), f32)],
        input_output_aliases={0: 0},
        compiler_params=_params(3, 40),
        name="sconv_prompt",
    )(y_init, z, z, z, w_sconv)


LRU_BLOCKS_PER_TILE = CW // LRU_BLOCK_DIM


def _build_block_diag(w_ref, bd_ref):
    bd_ref[...] = jnp.zeros(bd_ref.shape, bf16)
    for n in range(LRU_BLOCKS_PER_TILE):
        lo, hi = LRU_BLOCK_DIM * n, LRU_BLOCK_DIM * (n + 1)
        bd_ref[lo:hi, lo:hi] = w_ref[n].astype(bf16)


def _lru_gates(xc, wa_bd, wx_bd, ba, bx, lam):
    xcb = xc.astype(bf16)
    r = jax.nn.sigmoid(jnp.dot(xcb, wa_bd, preferred_element_type=f32) + ba)
    i = jax.nn.sigmoid(jnp.dot(xcb, wx_bd, preferred_element_type=f32) + bx)
    log_a = -LRU_C * r * jax.nn.softplus(-lam)
    a = jnp.exp(log_a)
    mult = jnp.sqrt(-_expm1(2.0 * log_a))
    return a, mult * (i * xc)


def _lru_p_kernel(y_init, xl_ref, wc_ref, bc_ref, wa_ref, ba_ref, wx_ref, bx_ref, lam_ref,
                  y_ref, h_ref, xbuf, wa_bd, wx_bd, a_s, b_s, hcar):
    del y_init
    t = pl.program_id(2)
    tt = xl_ref.shape[0]

    @pl.when(t == 0)
    def _():
        xbuf[0:SUBLANE, :] = jnp.zeros((SUBLANE, CW), f32)
        hcar[...] = jnp.zeros(hcar.shape, f32)
        _build_block_diag(wa_ref, wa_bd)
        _build_block_diag(wx_ref, wx_bd)

    xbuf[SUBLANE:SUBLANE + tt, :] = xl_ref[...]
    w = wc_ref[...]
    xc = xbuf[SUBLANE - 3:SUBLANE - 3 + tt, :] * w[0:1]
    for j in range(1, LRU_CONV_K):
        xc = xc + xbuf[SUBLANE - 3 + j:SUBLANE - 3 + j + tt, :] * w[j:j + 1]
    xc = xc + bc_ref[...]
    xbuf[0:SUBLANE, :] = xbuf[tt:tt + SUBLANE, :]

    a, bx = _lru_gates(xc, wa_bd[...], wx_bd[...], ba_ref[...], bx_ref[...], lam_ref[...])

    r8 = lax.broadcasted_iota(jnp.int32, (tt, CW), 0) % SUBLANE
    for s in (1, 2, 4):
        keep = r8 >= s
        a_sh = jnp.where(keep, pltpu.roll(a, s, 0), 1.0)
        b_sh = jnp.where(keep, pltpu.roll(bx, s, 0), 0.0)
        bx = bx + a * b_sh
        a = a * a_sh
    a_s[...] = a
    b_s[...] = bx

    def tile_step(j, h):
        r0 = pl.multiple_of(j * SUBLANE, SUBLANE)
        ht = b_s[pl.ds(r0, SUBLANE), :] + a_s[pl.ds(r0, SUBLANE), :] * h
        b_s[pl.ds(r0, SUBLANE), :] = ht
        return ht[SUBLANE - 1:SUBLANE, :]

    h_last = lax.fori_loop(0, tt // SUBLANE, tile_step, hcar[...])
    hcar[...] = h_last
    y_ref[...] = b_s[...].astype(bf16)
    h_ref[...] = b_s[tt - SUBLANE:tt, :]


def _lru_specs(layer, idx):
    return [pl.BlockSpec((None, LRU_CONV_K, CW), lambda *g: (layer, 0, idx(*g))),
            pl.BlockSpec((None, 1, CW), lambda *g: (layer, 0, idx(*g))),
            pl.BlockSpec((None, LRU_BLOCKS_PER_TILE, LRU_BLOCK_DIM, LRU_BLOCK_DIM),
                         lambda *g: (layer, idx(*g), 0, 0)),
            pl.BlockSpec((None, 1, CW), lambda *g: (layer, 0, idx(*g))),
            pl.BlockSpec((None, LRU_BLOCKS_PER_TILE, LRU_BLOCK_DIM, LRU_BLOCK_DIM),
                         lambda *g: (layer, idx(*g), 0, 0)),
            pl.BlockSpec((None, 1, CW), lambda *g: (layer, 0, idx(*g))),
            pl.BlockSpec((None, 1, CW), lambda *g: (layer, 0, idx(*g)))]


def _lru_prompt(y_init, z, lru_w, layer):
    nt = SEQ // TT_LRU
    return pl.pallas_call(
        _lru_p_kernel,
        grid=(BATCH, LRU_WIDTH // CW, nt),
        in_specs=[pl.BlockSpec(memory_space=pl.ANY),
                  pl.BlockSpec((TT_LRU, CW), lambda b, c, t: (b * nt + t, OFF_XL // CW + c))]
                 + _lru_specs(layer, lambda b, c, t: c),
        out_specs=[pl.BlockSpec((TT_LRU, CW), lambda b, c, t: (b * nt + t, c)),
                   pl.BlockSpec((None, SUBLANE, CW), lambda b, c, t: (b, 0, c))],
        out_shape=[jax.ShapeDtypeStruct((ROWS, LRU_WIDTH), bf16),
                   jax.ShapeDtypeStruct((BATCH, SUBLANE, LRU_WIDTH), f32)],
        input_output_aliases={0: 0},
        scratch_shapes=[pltpu.VMEM((TT_LRU + SUBLANE, CW), f32),
                        pltpu.VMEM((CW, CW), bf16), pltpu.VMEM((CW, CW), bf16),
                        pltpu.VMEM((TT_LRU, CW), f32), pltpu.VMEM((TT_LRU, CW), f32),
                        pltpu.VMEM((1, CW), f32)],
        compiler_params=_params(3, 40),
        name="lru_prompt",
    )(y_init, z, *lru_w)


def _seq_s_kernel(cb_ref, cc_ref, ch_ref, xl_ref, sc_ref, lc_ref, h0_ref, wsc_ref,
                  wc_ref, bc_ref, wa_ref, ba_ref, wx_ref, bx_ref, lam_ref,
                  yc_ref, yl_ref, scn_ref, hn_ref, wa_bd, wx_bd):
    nb = DEC_BATCH
    rows = lambda v, t: v[nb * t:nb * (t + 1), :]
    u = cc_ref[...] * ch_ref[...]
    cb = cb_ref[...]
    up = [sc_ref[j] for j in range(SCONV_K - 1)] + [rows(u, t) for t in range(DEC_SEQ)]
    w = wsc_ref[...]
    for t in range(DEC_SEQ):
        yc = up[t] * w[0:1]
        for j in range(1, SCONV_K):
            yc = yc + up[t + j] * w[j:j + 1]
        yc_ref[nb * t:nb * (t + 1), :] = (rows(cb, t) * yc).astype(bf16)
    for j in range(SCONV_K - 1):
        scn_ref[j] = up[DEC_SEQ + j]

    _build_block_diag(wa_ref, wa_bd)
    _build_block_diag(wx_ref, wx_bd)
    xl = xl_ref[...]
    xp = [lc_ref[j] for j in range(LRU_CONV_K - 1)] + [rows(xl, t) for t in range(DEC_SEQ)]
    wl = wc_ref[...]
    xcs = []
    for t in range(DEC_SEQ):
        xc = xp[t] * wl[0:1]
        for j in range(1, LRU_CONV_K):
            xc = xc + xp[t + j] * wl[j:j + 1]
        xcs.append(xc + bc_ref[...])
    xc = jnp.concatenate(xcs, axis=0)
    a, bx = _lru_gates(xc, wa_bd[...], wx_bd[...], ba_ref[...], bx_ref[...], lam_ref[...])
    h = h0_ref[...]
    for t in range(DEC_SEQ):
        h = rows(a, t) * h + rows(bx, t)
        yl_ref[nb * t:nb * (t + 1), :] = h.astype(bf16)
    hn_ref[...] = h


def _seq_sample(z, st_sconv_t, st_lconv_t, st_h, w_sconv, lru_w, layer):
    rblk = ROWS_P // ROWS_S
    zspec = lambda off: pl.BlockSpec((ROWS_S, CW), lambda c: (rblk, off // CW + c))
    return pl.pallas_call(
        _seq_s_kernel,
        grid=(CONV_WIDTH // CW,),
        in_specs=[zspec(OFF_CB), zspec(OFF_CC), zspec(OFF_CH), zspec(OFF_XL),
                  pl.BlockSpec((None, SCONV_K - 1, DEC_BATCH, CW), lambda c: (layer, 0, 0, c)),
                  pl.BlockSpec((None, LRU_CONV_K - 1, DEC_BATCH, CW), lambda c: (layer, 0, 0, c)),
                  pl.BlockSpec((None, DEC_BATCH, CW), lambda c: (layer, 0, c)),
                  pl.BlockSpec((None, SCONV_K, CW), lambda c: (layer, 0, c))]
                 + _lru_specs(layer, lambda c: c),
        out_specs=[pl.BlockSpec((ROWS_S, CW), lambda c: (0, c)),
                   pl.BlockSpec((ROWS_S, CW), lambda c: (0, c)),
                   pl.BlockSpec((SCONV_K - 1, DEC_BATCH, CW), lambda c: (0, 0, c)),
                   pl.BlockSpec((DEC_BATCH, CW), lambda c: (0, c))],
        out_shape=[jax.ShapeDtypeStruct((ROWS_S, CONV_WIDTH), bf16),
                   jax.ShapeDtypeStruct((ROWS_S, LRU_WIDTH), bf16),
                   jax.ShapeDtypeStruct((SCONV_K - 1, DEC_BATCH, CONV_WIDTH), f32),
                   jax.ShapeDtypeStruct((DEC_BATCH, LRU_WIDTH), f32)],
        scratch_shapes=[pltpu.VMEM((CW, CW), bf16), pltpu.VMEM((CW, CW), bf16)],
        compiler_params=_params(1, 32),
        name="seq_sample",
    )(z, z, z, z, st_sconv_t, st_lconv_t, st_h, w_sconv, *lru_w)


def _softmax_rows(s):
    m = jnp.max(s, axis=-1, keepdims=True)
    p = jnp.exp(s - m)
    return p * (1.0 / jnp.sum(p, axis=-1, keepdims=True))


def _mem_p_kernel(y_init, q_ref, k_ref, v_ref, y_ref):
    del y_init
    s = lax.dot_general(q_ref[...].astype(bf16), k_ref[...].astype(bf16), NT_DIMS,
                        preferred_element_type=f32) * (MEM_HEAD_DIM ** -0.5)
    pn = _softmax_rows(s).astype(bf16)
    y_ref[...] = jnp.dot(pn, v_ref[...].astype(bf16), preferred_element_type=f32).astype(bf16)


def _mem_prompt(y_init, z, mkv):
    nt = SEQ // TQ_MEM
    hw = MEM_HEAD_DIM
    return pl.pallas_call(
        _mem_p_kernel,
        grid=(BATCH, MEM_HEADS, nt),
        in_specs=[pl.BlockSpec(memory_space=pl.ANY),
                  pl.BlockSpec((TQ_MEM, hw), lambda b, h, t: (b * nt + t, OFF_QM // hw + h)),
                  pl.BlockSpec((MEM_LEN, hw), lambda b, h, t: (b, h)),
                  pl.BlockSpec((MEM_LEN, hw), lambda b, h, t: (b, MEM_HEADS + h))],
        out_specs=pl.BlockSpec((TQ_MEM, hw), lambda b, h, t: (b * nt + t, h)),
        out_shape=jax.ShapeDtypeStruct((ROWS, MEM_WIDTH), bf16),
        input_output_aliases={0: 0},
        compiler_params=_params(3, 32),
        name="mem_prompt",
    )(y_init, z, mkv, mkv)


SB_MEM = 4


def _mem_s_kernel(q_ref, k_ref, v_ref, y_ref):
    hw = MEM_HEAD_DIM
    for bb in range(SB_MEM):
        qb = q_ref[bb]
        for h in range(MEM_HEADS):
            qh = qb[:, hw * h:hw * (h + 1)].astype(bf16)
            s = lax.dot_general(qh, k_ref[bb, :, h, :].astype(bf16), NT_DIMS,
                                preferred_element_type=f32) * (MEM_HEAD_DIM ** -0.5)
            pn = _softmax_rows(s).astype(bf16)
            y_ref[bb, :, hw * h:hw * (h + 1)] = jnp.dot(pn, v_ref[bb, :, h, :].astype(bf16),
                                                        preferred_element_type=f32)


def _mem_sample(qm3, cache_k, cache_v, layer):
    cache_spec = pl.BlockSpec((None, SB_MEM, MEM_LEN, MEM_HEADS, MEM_HEAD_DIM),
                              lambda i: (layer, i, 0, 0, 0))
    return pl.pallas_call(
        _mem_s_kernel,
        grid=(DEC_BATCH // SB_MEM,),
        in_specs=[pl.BlockSpec((SB_MEM, DEC_SEQ, MEM_WIDTH), lambda i: (i, 0, 0)),
                  cache_spec, cache_spec],
        out_specs=pl.BlockSpec((SB_MEM, DEC_SEQ, MEM_WIDTH), lambda i: (i, 0, 0)),
        out_shape=jax.ShapeDtypeStruct((DEC_BATCH, DEC_SEQ, MEM_WIDTH), f32),
        compiler_params=_params(1, 48),
        name="mem_sample",
    )(qm3, cache_k, cache_v)


def _merge_kernel(y0, y1, y2, y3, g0, g1, g2, g3, wb_ref, o_ref):
    acc = None
    for b, (y, g) in enumerate(((y0, g0), (y1, g1), (y2, g2), (y3, g3))):
        proj = jnp.dot(y[...], wb_ref[b], preferred_element_type=f32)
        term = jax.nn.sigmoid(g[...]) * proj
        acc = term if acc is None else acc + term
    o_ref[...] = acc.astype(bf16)


def _merge(ys, zg, w_branch, layer):
    tn = TN_MERGE
    yspec = pl.BlockSpec((TM, BRANCH_WIDTH), lambda i, j: (i, 0))
    gspec = lambda b: pl.BlockSpec((TM, tn), lambda i, j: (i, (b * D_MODEL) // tn + j))
    return pl.pallas_call(
        _merge_kernel,
        grid=(zg.shape[0] // TM, D_MODEL // tn),
        in_specs=[yspec] * N_BRANCH + [gspec(b) for b in range(N_BRANCH)]
                 + [pl.BlockSpec((None, N_BRANCH, BRANCH_WIDTH, tn), lambda i, j: (layer, 0, 0, j))],
        out_specs=pl.BlockSpec((TM, tn), lambda i, j: (i, j)),
        out_shape=jax.ShapeDtypeStruct((zg.shape[0], D_MODEL), bf16),
        compiler_params=_params(2, 48),
        name="merge",
    )(*ys, zg, zg, zg, zg, w_branch)


def _oproj_kernel(m_ref, w_ref, x_ref, g_ref, b_ref, o_ref):
    d = jnp.dot(m_ref[...], w_ref[...], preferred_element_type=f32)
    o_ref[...] = _layer_norm(ALPHA * x_ref[...] + d, g_ref[0:1, :], b_ref[0:1, :])


def _oproj(merged, w_o, x, ln_g, ln_b, layer):
    tm = TM_OPROJ
    return pl.pallas_call(
        _oproj_kernel,
        grid=(x.shape[0] // tm,),
        in_specs=[pl.BlockSpec((tm, D_MODEL), lambda i: (i, 0)),
                  pl.BlockSpec((None, D_MODEL, D_MODEL), lambda i: (layer, 0, 0)),
                  pl.BlockSpec((tm, D_MODEL), lambda i: (i, 0)),
                  pl.BlockSpec((None, 2, D_MODEL), lambda i: (layer, 0, 0)),
                  pl.BlockSpec((None, 2, D_MODEL), lambda i: (layer, 0, 0))],
        out_specs=pl.BlockSpec((tm, D_MODEL), lambda i: (i, 0)),
        out_shape=jax.ShapeDtypeStruct(x.shape, f32),
        compiler_params=_params(1, 48),
        name="oproj_ln",
    )(merged, w_o, x, ln_g, ln_b)


def _swiglu_partial(xb, wg_ref, wu_ref, wd_ref):
    g = jnp.dot(xb, wg_ref[...], preferred_element_type=f32)
    u = jnp.dot(xb, wu_ref[...], preferred_element_type=f32)
    h = (jax.nn.silu(g) * u).astype(bf16)
    return jnp.dot(h, wd_ref[...], preferred_element_type=f32)


def _ffn_kernel(x_ref, wg_ref, wu_ref, wd_ref, g_ref, b_ref, o_ref, xb_ref):
    f = pl.program_id(1)

    @pl.when(f == 0)
    def _():
        xb_ref[...] = x_ref[...].astype(bf16)
        o_ref[...] = jnp.zeros(o_ref.shape, f32)

    o_ref[...] += _swiglu_partial(xb_ref[...], wg_ref, wu_ref, wd_ref)

    @pl.when(f == pl.num_programs(1) - 1)
    def _():
        o_ref[...] = _layer_norm(ALPHA * x_ref[...] + o_ref[...], g_ref[1:2, :], b_ref[1:2, :])


def _ffn(x, wg, wu, wd, ln_g, ln_b, layer, j):
    tm = TM_FFN
    w_in = pl.BlockSpec((None, D_MODEL, TF), lambda i, f: (j, 0, f))
    w_out = pl.BlockSpec((None, TF, D_MODEL), lambda i, f: (j, f, 0))
    xspec = pl.BlockSpec((tm, D_MODEL), lambda i, f: (i, 0))
    lnspec = pl.BlockSpec((None, 2, D_MODEL), lambda i, f: (layer, 0, 0))
    return pl.pallas_call(
        _ffn_kernel,
        grid=(x.shape[0] // tm, D_FF // TF),
        in_specs=[xspec, w_in, w_in, w_out, lnspec, lnspec],
        out_specs=xspec,
        out_shape=jax.ShapeDtypeStruct(x.shape, f32),
        scratch_shapes=[pltpu.VMEM((tm, D_MODEL), bf16)],
        compiler_params=_params(2, 52),
        name="ffn_dense",
    )(x, wg, wu, wd, ln_g, ln_b)


TG = 640


def _max_row_tiles(n_tokens):
    return (n_tokens * TOP_K + N_EXPERTS * (TG - 1)) // TG


def _router_kernel(x_ref, w_ref, p_ref, i_ref, cnt_ref, carry):
    @pl.when(pl.program_id(0) == 0)
    def _():
        carry[...] = jnp.zeros(carry.shape, f32)

    tm = x_ref.shape[0]
    logits = jnp.dot(x_ref[...], w_ref[...], precision=lax.Precision.HIGHEST,
                     preferred_element_type=f32)
    lane = lax.broadcasted_iota(jnp.int32, logits.shape, 1)
    logits = jnp.where(lane < N_EXPERTS, logits, -jnp.inf)
    m1 = jnp.max(logits, axis=-1, keepdims=True)
    i1 = jnp.min(jnp.where(logits == m1, lane, LANE), axis=-1, keepdims=True)
    rest = jnp.where(lane == i1, -jnp.inf, logits)
    m2 = jnp.max(rest, axis=-1, keepdims=True)
    i2 = jnp.min(jnp.where(rest == m2, lane, LANE), axis=-1, keepdims=True)
    e = jnp.exp(m2 - m1)
    p1 = 1.0 / (1.0 + e)
    p2 = e / (1.0 + e)
    sel = jnp.where(lane == i1, 1.0, jnp.where(lane == i2, 1.0, 0.0))
    r_i = lax.broadcasted_iota(jnp.int32, (tm, tm), 0)
    c_i = lax.broadcasted_iota(jnp.int32, (tm, tm), 1)
    tri = jnp.where(r_i > c_i, 1.0, 0.0).astype(bf16)
    before = jnp.dot(tri, sel.astype(bf16), preferred_element_type=f32) + carry[...]
    r1 = jnp.sum(jnp.where(lane == i1, before, 0.0), axis=-1, keepdims=True).astype(jnp.int32)
    r2 = jnp.sum(jnp.where(lane == i2, before, 0.0), axis=-1, keepdims=True).astype(jnp.int32)
    carry[...] = carry[...] + jnp.sum(sel, axis=0, keepdims=True)
    cnt_ref[...] = carry[...]
    p_ref[...] = jnp.where(lane == 0, p1, jnp.where(lane == 1, p2, 0.0))
    i_ref[...] = jnp.where(lane == 0, i1, jnp.where(lane == 1, i2,
                           jnp.where(lane == 2, r1, jnp.where(lane == 3, r2, 0))))


def _router(x, w_router_pad, j):
    n = x.shape[0]
    tm = TM_FFN
    return pl.pallas_call(
        _router_kernel,
        grid=(n // tm,),
        in_specs=[pl.BlockSpec((tm, D_MODEL), lambda i: (i, 0)),
                  pl.BlockSpec((None, D_MODEL, LANE), lambda i: (j, 0, 0))],
        out_specs=[pl.BlockSpec((tm, LANE), lambda i: (i, 0)),
                   pl.BlockSpec((tm, LANE), lambda i: (i, 0)),
                   pl.BlockSpec((1, LANE), lambda i: (0, 0))],
        out_shape=[jax.ShapeDtypeStruct((n, LANE), f32),
                   jax.ShapeDtypeStruct((n, LANE), jnp.int32),
                   jax.ShapeDtypeStruct((1, LANE), f32)],
        scratch_shapes=[pltpu.VMEM((1, LANE), f32)],
        compiler_params=_params(1, 40),
        name="router",
    )(x, w_router_pad)


def _routing_plan(iinfo, counts):
    cnt = counts[0, :N_EXPERTS].astype(jnp.int32)
    tiles = (cnt + (TG - 1)) // TG
    end_tile = jnp.cumsum(tiles)
    offset = (end_tile - tiles) * TG
    pos1 = offset[iinfo[:, 0]] + iinfo[:, 2]
    pos2 = offset[iinfo[:, 1]] + iinfo[:, 3]
    tile_ids = jnp.arange(_max_row_tiles(iinfo.shape[0]), dtype=jnp.int32)
    tile_expert = jnp.minimum(jnp.sum(tile_ids[:, None] >= end_tile[None, :], axis=1),
                              N_EXPERTS - 1).astype(jnp.int32)
    return pos1, pos2, tile_expert, end_tile[N_EXPERTS - 1:].astype(jnp.int32)


DMA_UNROLL = 8


def _row_copy(src, s, dst, d, sem):
    return pltpu.make_async_copy(src.at[pl.ds(s, 1)], dst.at[pl.ds(d, 1)], sem)


def _dispatch_kernel(pos1_ref, pos2_ref, x_ref, xs_in, xs_ref, sem):
    del xs_in
    tm = x_ref.shape[0]
    base = pl.program_id(0) * tm

    def issue(r, c):
        _row_copy(x_ref, r, xs_ref, pos1_ref[base + r], sem).start()
        _row_copy(x_ref, r, xs_ref, pos2_ref[base + r], sem).start()
        return c

    def drain(r, c):
        _row_copy(x_ref, r, xs_ref, pos1_ref[base + r], sem).wait()
        _row_copy(x_ref, r, xs_ref, pos2_ref[base + r], sem).wait()
        return c

    lax.fori_loop(0, tm, issue, 0, unroll=DMA_UNROLL)
    lax.fori_loop(0, tm, drain, 0, unroll=DMA_UNROLL)


def _dispatch(x, pos1, pos2):
    tm = TM_FFN
    n_rows = _max_row_tiles(x.shape[0]) * TG
    xs0 = jnp.zeros((n_rows, D_MODEL), f32)
    return pl.pallas_call(
        _dispatch_kernel,
        grid_spec=pltpu.PrefetchScalarGridSpec(
            num_scalar_prefetch=2,
            grid=(x.shape[0] // tm,),
            in_specs=[pl.BlockSpec((tm, D_MODEL), lambda i, p1, p2: (i, 0)),
                      pl.BlockSpec(memory_space=pl.ANY)],
            out_specs=pl.BlockSpec(memory_space=pl.ANY),
            scratch_shapes=[pltpu.SemaphoreType.DMA(())]),
        out_shape=jax.ShapeDtypeStruct((n_rows, D_MODEL), f32),
        input_output_aliases={3: 0},
        compiler_params=_params(1, 32),
        name="moe_dispatch",
    )(pos1, pos2, x, xs0)


def _gffn_kernel(te_ref, nt_ref, x_ref, wg_ref, wu_ref, wd_ref, o_ref, xb_ref):
    del te_ref
    f = pl.program_id(1)
    used = pl.program_id(0) < nt_ref[0]

    @pl.when(jnp.logical_and(jnp.logical_not(used), f == 0))
    def _():
        o_ref[...] = jnp.zeros(o_ref.shape, f32)

    @pl.when(jnp.logical_and(used, f == 0))
    def _():
        xb_ref[...] = x_ref[...].astype(bf16)
        o_ref[...] = jnp.zeros(o_ref.shape, f32)

    @pl.when(used)
    def _():
        o_ref[...] += _swiglu_partial(xb_ref[...], wg_ref, wu_ref, wd_ref)


def _grouped_ffn(xs, tile_expert, n_tiles, wg, wu, wd, j):
    n_f = D_FF // TF
    row = lambda i, nt: jnp.minimum(i, nt[0] - 1)
    fcol = lambda i, f, nt: jnp.where(i < nt[0], f, n_f - 1)
    xspec = pl.BlockSpec((TG, D_MODEL), lambda i, f, te, nt: (row(i, nt), 0))
    w_in = pl.BlockSpec((None, None, D_MODEL, TF),
                        lambda i, f, te, nt: (j, te[row(i, nt)], 0, fcol(i, f, nt)))
    w_out = pl.BlockSpec((None, None, TF, D_MODEL),
                         lambda i, f, te, nt: (j, te[row(i, nt)], fcol(i, f, nt), 0))
    return pl.pallas_call(
        _gffn_kernel,
        grid_spec=pltpu.PrefetchScalarGridSpec(
            num_scalar_prefetch=2,
            grid=(xs.shape[0] // TG, n_f),
            in_specs=[xspec, w_in, w_in, w_out],
            out_specs=pl.BlockSpec((TG, D_MODEL), lambda i, f, te, nt: (i, 0)),
            scratch_shapes=[pltpu.VMEM((TG, D_MODEL), bf16)]),
        out_shape=jax.ShapeDtypeStruct(xs.shape, f32),
        compiler_params=_params(2, 52),
        name="ffn_experts",
    )(tile_expert, n_tiles, xs, wg, wu, wd)


def _combine_kernel(pos1_ref, pos2_ref, x_ref, p_ref, y_ref, g_ref, b_ref, o_ref, buf1, buf2, sem):
    tm = x_ref.shape[0]
    base = pl.program_id(0) * tm

    def issue(r, c):
        _row_copy(y_ref, pos1_ref[base + r], buf1, r, sem).start()
        _row_copy(y_ref, pos2_ref[base + r], buf2, r, sem).start()
        return c

    def drain(r, c):
        _row_copy(y_ref, pos1_ref[base + r], buf1, r, sem).wait()
        _row_copy(y_ref, pos2_ref[base + r], buf2, r, sem).wait()
        return c

    lax.fori_loop(0, tm, issue, 0, unroll=DMA_UNROLL)
    lax.fori_loop(0, tm, drain, 0, unroll=DMA_UNROLL)
    p = p_ref[...]
    mixed = p[:, 0:1] * buf1[...] + p[:, 1:2] * buf2[...]
    o_ref[...] = _layer_norm(ALPHA * x_ref[...] + mixed, g_ref[1:2, :], b_ref[1:2, :])


def _combine(x, pinfo, y, pos1, pos2, ln_g, ln_b, layer):
    tm = TM_FFN
    xspec = pl.BlockSpec((tm, D_MODEL), lambda i, p1, p2: (i, 0))
    lnspec = pl.BlockSpec((None, 2, D_MODEL), lambda i, p1, p2: (layer, 0, 0))
    return pl.pallas_call(
        _combine_kernel,
        grid_spec=pltpu.PrefetchScalarGridSpec(
            num_scalar_prefetch=2,
            grid=(x.shape[0] // tm,),
            in_specs=[xspec, pl.BlockSpec((tm, LANE), lambda i, p1, p2: (i, 0)),
                      pl.BlockSpec(memory_space=pl.ANY), lnspec, lnspec],
            out_specs=xspec,
            scratch_shapes=[pltpu.VMEM((tm, D_MODEL), f32), pltpu.VMEM((tm, D_MODEL), f32),
                            pltpu.SemaphoreType.DMA(())]),
        out_shape=jax.ShapeDtypeStruct(x.shape, f32),
        compiler_params=_params(1, 48),
        name="moe_combine",
    )(pos1, pos2, x, pinfo, y, ln_g, ln_b)


def _moe(x, w_router_pad, wg, wu, wd, ln_g, ln_b, layer, j):
    pinfo, iinfo, counts = _router(x, w_router_pad, j)
    pos1, pos2, tile_expert, n_tiles = _routing_plan(iinfo, counts)
    xs = _dispatch(x, pos1, pos2)
    y = _grouped_ffn(xs, tile_expert, n_tiles, wg, wu, wd, j)
    return _combine(x, pinfo, y, pos1, pos2, ln_g, ln_b, layer)


def kernel(x_prompt, x_sample, mem_prompt, cache_win_k, cache_win_v, cache_mem_k, cache_mem_v,
           state_sconv, state_lru_conv, state_lru_h, w_mix_in, sinks, w_sconv, w_lru_conv, b_lru_conv,
           w_lru_a, b_lru_a, w_lru_x, b_lru_x, lru_lambda, w_mem_kv, w_branch, w_o, ln_g, ln_b,
           w_router, w_ffn_gate, w_ffn_up, w_ffn_down, w_exp_gate, w_exp_up, w_exp_down):
    x = jnp.concatenate([x_prompt.reshape(ROWS_P, D_MODEL),
                         x_sample.transpose(1, 0, 2).reshape(ROWS_S, D_MODEL)], axis=0)
    mem = mem_prompt.reshape(BATCH * MEM_LEN, D_MODEL)
    w_in_b = w_mix_in.astype(bf16)
    w_gate_b = w_in_b[:, :, OFF_G:]
    w_mkv_b = w_mem_kv.astype(bf16)
    w_br_b = w_branch.astype(bf16)
    w_o_b = w_o.astype(bf16)
    wfg, wfu, wfd = w_ffn_gate.astype(bf16), w_ffn_up.astype(bf16), w_ffn_down.astype(bf16)
    weg, weu, wed = w_exp_gate.astype(bf16), w_exp_up.astype(bf16), w_exp_down.astype(bf16)
    w_router_pad = jnp.pad(w_router, ((0, 0), (0, 0), (0, LANE - N_EXPERTS)))
    ck = cache_win_k.reshape(DEPTH, DEC_BATCH, WINDOW, KV_WIDTH)
    cv = cache_win_v.reshape(DEPTH, DEC_BATCH, WINDOW, KV_WIDTH)
    st_sc_t = state_sconv.transpose(0, 2, 1, 3)
    st_lc_t = state_lru_conv.transpose(0, 2, 1, 3)
    lru_w = (w_lru_conv, b_lru_conv.reshape(DEPTH, 1, LRU_WIDTH), w_lru_a,
             b_lru_a.reshape(DEPTH, 1, LRU_WIDTH), w_lru_x, b_lru_x.reshape(DEPTH, 1, LRU_WIDTH),
             lru_lambda.reshape(DEPTH, 1, LRU_WIDTH))
    tab_p = _rope_tables(jnp.arange(SEQ, dtype=jnp.int32))
    tab_s = _rope_tables(PAST_LEN + jnp.arange(DEC_SEQ, dtype=jnp.int32))

    def to_b_major(v):
        return v.reshape(DEC_SEQ, DEC_BATCH, v.shape[-1]).transpose(1, 0, 2)

    def to_t_major(v):
        return v.transpose(1, 0, 2).reshape(ROWS_S, v.shape[-1])

    outs = [[] for _ in range(12)]
    for l in range(DEPTH):
        z = _matmul(x, w_in_b, l, TM, TN_MIX, n=OFF_G)
        zg = _matmul(x, w_gate_b, l, TM, TN_GATE)
        mkv = _matmul(mem, w_mkv_b, l, BATCH * MEM_LEN, TN_MKV)
        zs = z[ROWS_P:]

        def all_rows(y_s):
            return lax.dynamic_update_slice(jnp.zeros((ROWS, y_s.shape[-1]), bf16), y_s, (ROWS_P, 0))

        ya_s, nk_s, nv_s = _attn_sample(to_b_major(zs[:, :OFF_CB]), ck, cv, sinks, tab_s, l)
        yc_s, yl_s, sc_s, h_s = _seq_sample(z, st_sc_t, st_lc_t, state_lru_h, w_sconv, lru_w, l)
        ym_s = _mem_sample(to_b_major(zs[:, OFF_QM:OFF_G]), cache_mem_k, cache_mem_v, l)
        ya, krot_p = _attn_prompt(all_rows(to_t_major(ya_s).astype(bf16)), z, sinks, tab_p, l)
        yc, sc_p = _sconv_prompt(all_rows(yc_s), z, w_sconv, l)
        yl, h_p = _lru_prompt(all_rows(yl_s), z, lru_w, l)
        ym = _mem_prompt(all_rows(to_t_major(ym_s).astype(bf16)), z, mkv)
        merged = _merge((ya, yc, yl, ym), zg, w_br_b, l)
        x = _oproj(merged, w_o_b, x, ln_g, ln_b, l)
        j = l // 2
        if l % 2 == 0:
            x = _ffn(x, wfg, wfu, wfd, ln_g, ln_b, l, j)
        else:
            x = _moe(x, w_router_pad, weg, weu, wed, ln_g, ln_b, l, j)

        def tail(nrows, lo, hi):
            return jnp.stack([z[(b + 1) * SEQ - nrows:(b + 1) * SEQ, lo:hi] for b in range(BATCH)])

        kv_shape = (BATCH, WINDOW, KV_HEADS, HEAD_DIM)
        outs[0].append(krot_p.reshape(BATCH, SEQ, KV_WIDTH)[:, SEQ - WINDOW:].reshape(kv_shape))
        outs[1].append(tail(WINDOW, OFF_V, OFF_CB).reshape(kv_shape))
        mem_shape = (BATCH, MEM_LEN, MEM_HEADS, MEM_HEAD_DIM)
        outs[2].append(mkv[:, :MEM_WIDTH].reshape(mem_shape))
        outs[3].append(mkv[:, MEM_WIDTH:].reshape(mem_shape))
        outs[4].append(sc_p[:, SUBLANE - (SCONV_K - 1):])
        outs[5].append(tail(LRU_CONV_K - 1, OFF_XL, OFF_QM))
        outs[6].append(h_p[:, SUBLANE - 1])
        kvs_shape = (DEC_BATCH, WINDOW, KV_HEADS, HEAD_DIM)
        outs[7].append(nk_s.reshape(kvs_shape))
        outs[8].append(nv_s.reshape(kvs_shape))
        outs[9].append(sc_s.transpose(1, 0, 2))
        outs[10].append(zs[DEC_BATCH:, OFF_XL:OFF_QM].reshape(LRU_CONV_K - 1, DEC_BATCH, LRU_WIDTH
                                                              ).transpose(1, 0, 2))
        outs[11].append(h_s)

    y_prompt = x[:ROWS_P].reshape(BATCH, SEQ, D_MODEL)
    y_sample = to_b_major(x[ROWS_P:])
    return (y_prompt, y_sample) + tuple(jnp.stack(o) for o in outs)
```

```python
import functools

import jax
import jax.numpy as jnp
from jax import lax
from jax.experimental import pallas as pl
from jax.experimental.pallas import tpu as pltpu

f32 = jnp.float32
bf16 = jnp.bfloat16

D_MODEL = 2048
BATCH = 2
SEQ = 4096
DEPTH = 4
DEC_BATCH = 32
DEC_SEQ = 4
PAST_LEN = 16384
N_HEADS = 16
KV_HEADS = 4
GQA_GROUP = N_HEADS // KV_HEADS
HEAD_DIM = 64
ROPE_DIM = HEAD_DIM // 4
ROPE_THETA = 500000.0
WINDOW = 128
ATTN_WIDTH = N_HEADS * HEAD_DIM
KV_WIDTH = KV_HEADS * HEAD_DIM
CONV_WIDTH = D_MODEL // 2
SCONV_K = 3
LRU_WIDTH = D_MODEL // 2
LRU_BLOCKS = 16
LRU_BLOCK_DIM = LRU_WIDTH // LRU_BLOCKS
LRU_CONV_K = 4
LRU_C = 8.0
MEM_LEN = 256
MEM_HEADS = 4
MEM_HEAD_DIM = 256
MEM_WIDTH = MEM_HEADS * MEM_HEAD_DIM
N_BRANCH = 4
BRANCH_WIDTH = D_MODEL // 2
D_FF = 5632
N_EXPERTS = 8
TOP_K = 2
ALPHA = (2 * DEPTH) ** 0.25
LN_EPS = 1e-5
NEG_INF = -1e30

ROWS_P = BATCH * SEQ
ROWS_S = DEC_BATCH * DEC_SEQ
ROWS = ROWS_P + ROWS_S

OFF_Q = 0
OFF_K = OFF_Q + ATTN_WIDTH
OFF_V = OFF_K + KV_WIDTH
OFF_CB = OFF_V + KV_WIDTH
OFF_CC = OFF_CB + CONV_WIDTH
OFF_CH = OFF_CC + CONV_WIDTH
OFF_XL = OFF_CH + CONV_WIDTH
OFF_QM = OFF_XL + LRU_WIDTH
OFF_G = OFF_QM + MEM_WIDTH
MIX_IN = OFF_G + N_BRANCH * D_MODEL

LANE = 128
SUBLANE = 8
MIB = 1024 * 1024
TM = 832
TN_MIX = 1664
TN_MKV = 512
TN_MERGE = 512
TM_OPROJ = 416
TM_FFN = 640
TF = 512
CW = 512
TT_CONV = 1024
TT_LRU = 512
TQ_MEM = 2048

NT_DIMS = (((1,), (1,)), ((), ()))


def _params(n_axes, vmem_mib):
    return pltpu.CompilerParams(dimension_semantics=("arbitrary",) * n_axes,
                                vmem_limit_bytes=vmem_mib * MIB)


def _layer_norm(v, g, b):
    mu = jnp.mean(v, axis=-1, keepdims=True)
    vc = v - mu
    var = jnp.mean(vc * vc, axis=-1, keepdims=True)
    return vc * lax.rsqrt(var + LN_EPS) * g + b


def _expm1(x):
    fact = [1.0]
    for k in range(1, 10):
        fact.append(fact[-1] * k)
    p = 1.0 / fact[9]
    for k in range(8, 0, -1):
        p = p * x + 1.0 / fact[k]
    return jnp.where(jnp.abs(x) < 0.5, x * p, jnp.exp(x) - 1.0)


def _mm_kernel(x_ref, w_ref, o_ref, xb_ref):
    @pl.when(pl.program_id(1) == 0)
    def _():
        xb_ref[...] = x_ref[...].astype(bf16)

    o_ref[...] = jnp.dot(xb_ref[...], w_ref[...], preferred_element_type=f32)


def _matmul(x, w_stack, layer, tm, tn, n=None):
    m, k = x.shape
    n = w_stack.shape[-1] if n is None else n
    vmem = 2 * (tm * k * 4 + k * tn * 2 + tm * tn * 4) + tm * k * 2
    return pl.pallas_call(
        _mm_kernel,
        grid=(m // tm, n // tn),
        in_specs=[pl.BlockSpec((tm, k), lambda i, j: (i, 0)),
                  pl.BlockSpec((None, k, tn), lambda i, j: (layer, 0, j))],
        out_specs=pl.BlockSpec((tm, tn), lambda i, j: (i, j)),
        out_shape=jax.ShapeDtypeStruct((m, n), f32),
        scratch_shapes=[pltpu.VMEM((tm, k), bf16)],
        compiler_params=_params(2, vmem // MIB + 8),
        name="matmul",
    )(x, w_stack)


def _rope_tables(pos):
    half = ROPE_DIM // 2
    inv_freq = ROPE_THETA ** (-jnp.arange(half, dtype=f32) * (2.0 / ROPE_DIM))
    ang = pos.astype(f32)[:, None] * inv_freq[None, :]
    cos = jnp.cos(ang)
    sin = jnp.sin(ang)
    p = pos.shape[0]
    rest = HEAD_DIM - ROPE_DIM
    c = jnp.concatenate([cos, cos, jnp.ones((p, rest), f32)], axis=1)
    sa = jnp.concatenate([-sin, jnp.zeros((p, HEAD_DIM - half), f32)], axis=1)
    sb = jnp.concatenate([jnp.zeros((p, half), f32), sin, jnp.zeros((p, rest), f32)], axis=1)
    rep = LANE // HEAD_DIM
    return jnp.tile(c, (1, rep)), jnp.tile(sa, (1, rep)), jnp.tile(sb, (1, rep))


def _rope(x, c, sa, sb):
    half = ROPE_DIM // 2
    chunks = []
    for j in range(x.shape[1] // LANE):
        xc = x[:, LANE * j:LANE * (j + 1)]
        chunks.append(xc * c + pltpu.roll(xc, LANE - half, 1) * sa + pltpu.roll(xc, half, 1) * sb)
    return chunks[0] if len(chunks) == 1 else jnp.concatenate(chunks, axis=1)


def _sink_softmax(s, sink):
    m = jnp.maximum(jnp.max(s, axis=-1, keepdims=True), sink)
    p = jnp.exp(s - m)
    denom = jnp.sum(p, axis=-1, keepdims=True) + jnp.exp(sink - m)
    return p * (1.0 / denom)


def _attn_block(n, sinks_ref, q_ref, k_ref, v_ref, c_ref, sa_ref, sb_ref, y_ref, kr_ref,
                kprev, vprev, layer, before_head=None):
    @pl.when(n == 0)
    def _():
        kprev[...] = jnp.zeros(kprev.shape, f32)
        vprev[...] = jnp.zeros(vprev.shape, f32)

    c, sa, sb = c_ref[...], sa_ref[...], sb_ref[...]
    q = _rope(q_ref[...], c, sa, sb)
    kc = _rope(k_ref[...], c, sa, sb)
    vc = v_ref[...]
    kr_ref[...] = kc
    kall = jnp.concatenate([kprev[...], kc], axis=0).astype(bf16)
    vall = jnp.concatenate([vprev[...], vc], axis=0).astype(bf16)
    kprev[...] = kc
    vprev[...] = vc

    rows = GQA_GROUP * WINDOW
    qi = lax.broadcasted_iota(jnp.int32, (rows, 2 * WINDOW), 0) % WINDOW
    kj = lax.broadcasted_iota(jnp.int32, (rows, 2 * WINDOW), 1)
    kmin = jnp.where(n > 0, 0, WINDOW)
    mask = (kj > qi) & (kj <= qi + WINDOW) & (kj >= kmin)
    rg = lax.broadcasted_iota(jnp.int32, (rows, 1), 0) // WINDOW
    for h in range(KV_HEADS):
        if before_head is not None:
            before_head(h)
        qh = jnp.concatenate(
            [q[:, (GQA_GROUP * h + g) * HEAD_DIM:(GQA_GROUP * h + g + 1) * HEAD_DIM]
             for g in range(GQA_GROUP)], axis=0).astype(bf16)
        kh = kall[:, HEAD_DIM * h:HEAD_DIM * (h + 1)]
        vh = vall[:, HEAD_DIM * h:HEAD_DIM * (h + 1)]
        s = lax.dot_general(qh, kh, NT_DIMS, preferred_element_type=f32) * (HEAD_DIM ** -0.5)
        s = jnp.where(mask, s, NEG_INF)
        sink = jnp.zeros((rows, 1), f32)
        for g in range(GQA_GROUP):
            sink = jnp.where(rg == g, sinks_ref[layer, GQA_GROUP * h + g], sink)
        pn = _sink_softmax(s, sink).astype(bf16)
        o = jnp.dot(pn, vh, preferred_element_type=f32)
        for g in range(GQA_GROUP):
            hd = GQA_GROUP * h + g
            y_ref[:, hd * HEAD_DIM:(hd + 1) * HEAD_DIM] = o[WINDOW * g:WINDOW * (g + 1), :].astype(bf16)


GA_ROWS = 8
GA_COLS = 8
TM_GA = ROWS // GA_ROWS
TN_GA = N_BRANCH * D_MODEL // GA_COLS


def _gates_attn_kernel(y_init, sinks_ref, x_ref, w_ref, q_ref, k_ref, v_ref, c_ref, sa_ref, sb_ref,
                       zg_ref, y_ref, kr_ref, xb_ref, kprev, vprev, *, layer):
    del y_init
    j = pl.program_id(1)
    step = pl.program_id(0) * GA_COLS + j

    @pl.when(j == 0)
    def _():
        xb_ref[...] = x_ref[...].astype(bf16)

    cw = TN_GA // KV_HEADS

    def gate_columns(h):
        zg_ref[:, cw * h:cw * (h + 1)] = jnp.dot(
            xb_ref[...], w_ref[:, cw * h:cw * (h + 1)], preferred_element_type=f32).astype(zg_ref.dtype)

    _attn_block(step % (SEQ // WINDOW), sinks_ref, q_ref, k_ref, v_ref, c_ref, sa_ref, sb_ref,
                y_ref, kr_ref, kprev, vprev, layer, before_head=gate_columns)


def _gates_attn_prompt(y_init, x, w_gate_b, z, sinks, tables, layer):
    nblk = SEQ // WINDOW
    assert GA_ROWS * GA_COLS == BATCH * nblk
    step = lambda i, j: i * GA_COLS + j
    tab = pl.BlockSpec((WINDOW, LANE), lambda i, j: (step(i, j) % nblk, 0))
    return pl.pallas_call(
        functools.partial(_gates_attn_kernel, layer=layer),
        grid=(GA_ROWS, GA_COLS),
        in_specs=[pl.BlockSpec(memory_space=pl.ANY),
                  pl.BlockSpec(memory_space=pltpu.SMEM),
                  pl.BlockSpec((TM_GA, D_MODEL), lambda i, j: (i, 0)),
                  pl.BlockSpec((None, D_MODEL, TN_GA), lambda i, j: (layer, 0, j)),
                  pl.BlockSpec((WINDOW, ATTN_WIDTH), lambda i, j: (step(i, j), OFF_Q // ATTN_WIDTH)),
                  pl.BlockSpec((WINDOW, KV_WIDTH), lambda i, j: (step(i, j), OFF_K // KV_WIDTH)),
                  pl.BlockSpec((WINDOW, KV_WIDTH), lambda i, j: (step(i, j), OFF_V // KV_WIDTH)),
                  tab, tab, tab],
        out_specs=[pl.BlockSpec((TM_GA, TN_GA), lambda i, j: (i, j)),
                   pl.BlockSpec((WINDOW, ATTN_WIDTH), lambda i, j: (step(i, j), 0)),
                   pl.BlockSpec((WINDOW, KV_WIDTH), lambda i, j: (step(i, j), 0))],
        out_shape=[jax.ShapeDtypeStruct((ROWS, N_BRANCH * D_MODEL), bf16),
                   jax.ShapeDtypeStruct((ROWS, ATTN_WIDTH), bf16),
                   jax.ShapeDtypeStruct((ROWS_P, KV_WIDTH), f32)],
        scratch_shapes=[pltpu.VMEM((TM_GA, D_MODEL), bf16),
                        pltpu.VMEM((WINDOW, KV_WIDTH), f32), pltpu.VMEM((WINDOW, KV_WIDTH), f32)],
        input_output_aliases={0: 1},
        compiler_params=_params(2, 52),
        name="gates_attn_prompt",
    )(y_init, sinks, x, w_gate_b, z, z, z, *tables)


SB_ATTN = 8
KALL = WINDOW + SUBLANE


def _attn_s_kernel(sinks_ref, z_ref, ck_ref, cv_ref, c_ref, sa_ref, sb_ref, y_ref, nk_ref, nv_ref,
                   kall, vall, qs, *, layer):
    c, sa, sb = c_ref[...], sa_ref[...], sb_ref[...]
    rows = GQA_GROUP * DEC_SEQ
    kall[WINDOW:KALL, :] = jnp.zeros((SUBLANE, KV_WIDTH), f32)
    vall[WINDOW:KALL, :] = jnp.zeros((SUBLANE, KV_WIDTH), f32)
    qt = lax.broadcasted_iota(jnp.int32, (rows, KALL), 0) % DEC_SEQ
    kj = lax.broadcasted_iota(jnp.int32, (rows, KALL), 1)
    mask = jnp.where(kj < WINDOW, kj - qt, qt - (kj - WINDOW) + 1) > 0
    rg = lax.broadcasted_iota(jnp.int32, (rows, 1), 0) // DEC_SEQ
    for bb in range(SB_ATTN):
        zb = z_ref[bb]
        q = _rope(zb[:, OFF_Q:OFF_Q + ATTN_WIDTH], c, sa, sb)
        kn = _rope(zb[:, OFF_K:OFF_K + KV_WIDTH], c, sa, sb)
        vn = zb[:, OFF_V:OFF_V + KV_WIDTH]
        kall[0:WINDOW, :] = ck_ref[bb]
        vall[0:WINDOW, :] = cv_ref[bb]
        kall[WINDOW:WINDOW + DEC_SEQ, :] = kn
        vall[WINDOW:WINDOW + DEC_SEQ, :] = vn
        nk_ref[bb] = kall[DEC_SEQ:DEC_SEQ + WINDOW, :]
        nv_ref[bb] = vall[DEC_SEQ:DEC_SEQ + WINDOW, :]
        for h in range(KV_HEADS):
            for g in range(GQA_GROUP):
                hd = GQA_GROUP * h + g
                qs[DEC_SEQ * g:DEC_SEQ * (g + 1), :] = q[:, hd * HEAD_DIM:(hd + 1) * HEAD_DIM]
            qh = qs[...].astype(bf16)
            kh = kall[:, HEAD_DIM * h:HEAD_DIM * (h + 1)].astype(bf16)
            vh = vall[:, HEAD_DIM * h:HEAD_DIM * (h + 1)].astype(bf16)
            s = lax.dot_general(qh, kh, NT_DIMS, preferred_element_type=f32) * (HEAD_DIM ** -0.5)
            s = jnp.where(mask, s, NEG_INF)
            sink = jnp.zeros((rows, 1), f32)
            for g in range(GQA_GROUP):
                sink = jnp.where(rg == g, sinks_ref[layer, GQA_GROUP * h + g], sink)
            pn = _sink_softmax(s, sink).astype(bf16)
            o = jnp.dot(pn, vh, preferred_element_type=f32)
            for g in range(GQA_GROUP):
                hd = GQA_GROUP * h + g
                y_ref[bb, :, hd * HEAD_DIM:(hd + 1) * HEAD_DIM] = o[DEC_SEQ * g:DEC_SEQ * (g + 1), :]


def _attn_sample(zs3, cache_k, cache_v, sinks, tables, layer):
    qkv = ATTN_WIDTH + 2 * KV_WIDTH
    return pl.pallas_call(
        functools.partial(_attn_s_kernel, layer=layer),
        grid=(DEC_BATCH // SB_ATTN,),
        in_specs=[pl.BlockSpec(memory_space=pltpu.SMEM),
                  pl.BlockSpec((SB_ATTN, DEC_SEQ, qkv), lambda i: (i, 0, 0)),
                  pl.BlockSpec((None, SB_ATTN, WINDOW, KV_WIDTH), lambda i: (layer, i, 0, 0)),
                  pl.BlockSpec((None, SB_ATTN, WINDOW, KV_WIDTH), lambda i: (layer, i, 0, 0)),
                  pl.BlockSpec((DEC_SEQ, LANE), lambda i: (0, 0)),
                  pl.BlockSpec((DEC_SEQ, LANE), lambda i: (0, 0)),
                  pl.BlockSpec((DEC_SEQ, LANE), lambda i: (0, 0))],
        out_specs=[pl.BlockSpec((SB_ATTN, DEC_SEQ, ATTN_WIDTH), lambda i: (i, 0, 0)),
                   pl.BlockSpec((SB_ATTN, WINDOW, KV_WIDTH), lambda i: (i, 0, 0)),
                   pl.BlockSpec((SB_ATTN, WINDOW, KV_WIDTH), lambda i: (i, 0, 0))],
        out_shape=[jax.ShapeDtypeStruct((DEC_BATCH, DEC_SEQ, ATTN_WIDTH), f32),
                   jax.ShapeDtypeStruct((DEC_BATCH, WINDOW, KV_WIDTH), f32),
                   jax.ShapeDtypeStruct((DEC_BATCH, WINDOW, KV_WIDTH), f32)],
        scratch_shapes=[pltpu.VMEM((KALL, KV_WIDTH), f32), pltpu.VMEM((KALL, KV_WIDTH), f32),
                        pltpu.VMEM((GQA_GROUP * DEC_SEQ, HEAD_DIM), f32)],
        compiler_params=_params(1, 32),
        name="attn_sample",
    )(sinks, zs3, cache_k, cache_v, *tables)


def _sconv_p_kernel(y_init, cb_ref, cc_ref, ch_ref, w_ref, y_ref, st_ref, ubuf):
    del y_init
    t = pl.program_id(2)
    tt = cc_ref.shape[0]

    @pl.when(t == 0)
    def _():
        ubuf[0:SUBLANE, :] = jnp.zeros((SUBLANE, CW), f32)

    ubuf[SUBLANE:SUBLANE + tt, :] = cc_ref[...] * ch_ref[...]
    w = w_ref[...]
    yc = ubuf[SUBLANE - 2:SUBLANE - 2 + tt, :] * w[0:1]
    yc = yc + ubuf[SUBLANE - 1:SUBLANE - 1 + tt, :] * w[1:2]
    yc = yc + ubuf[SUBLANE:SUBLANE + tt, :] * w[2:3]
    y_ref[...] = (cb_ref[...] * yc).astype(bf16)
    last = ubuf[tt:tt + SUBLANE, :]
    st_ref[...] = last
    ubuf[0:SUBLANE, :] = last


def _sconv_prompt(y_init, z, w_sconv, layer):
    nt = SEQ // TT_CONV
    zspec = lambda off: pl.BlockSpec((TT_CONV, CW), lambda b, c, t: (b * nt + t, off // CW + c))
    return pl.pallas_call(
        _sconv_p_kernel,
        grid=(BATCH, CONV_WIDTH // CW, nt),
        in_specs=[pl.BlockSpec(memory_space=pl.ANY), zspec(OFF_CB), zspec(OFF_CC), zspec(OFF_CH),
                  pl.BlockSpec((None, SCONV_K, CW), lambda b, c, t: (layer, 0, c))],
        out_specs=[pl.BlockSpec((TT_CONV, CW), lambda b, c, t: (b * nt + t, c)),
                   pl.BlockSpec((None, SUBLANE, CW), lambda b, c, t: (b, 0, c))],
        out_shape=[jax.ShapeDtypeStruct((ROWS, CONV_WIDTH), bf16),
                   jax.ShapeDtypeStruct((BATCH, SUBLANE, CONV_WIDTH), f32)],
        scratch_shapes=[pltpu.VMEM((TT_CONV + SUBLANE, CW), f32)],
        input_output_aliases={0: 0},
        compiler_params=_params(3, 40),
        name="sconv_prompt",
    )(y_init, z, z, z, w_sconv)


LRU_BLOCKS_PER_TILE = CW // LRU_BLOCK_DIM


def _build_block_diag(w_ref, bd_ref):
    bd_ref[...] = jnp.zeros(bd_ref.shape, bf16)
    for n in range(LRU_BLOCKS_PER_TILE):
        lo, hi = LRU_BLOCK_DIM * n, LRU_BLOCK_DIM * (n + 1)
        bd_ref[lo:hi, lo:hi] = w_ref[n].astype(bf16)


def _lru_gates(xc, wa_bd, wx_bd, ba, bx, lam):
    xcb = xc.astype(bf16)
    r = jax.nn.sigmoid(jnp.dot(xcb, wa_bd, preferred_element_type=f32) + ba)
    i = jax.nn.sigmoid(jnp.dot(xcb, wx_bd, preferred_element_type=f32) + bx)
    log_a = -LRU_C * r * jax.nn.softplus(-lam)
    a = jnp.exp(log_a)
    mult = jnp.sqrt(-_expm1(2.0 * log_a))
    return a, mult * (i * xc)


def _lru_p_kernel(y_init, xl_ref, wc_ref, bc_ref, wa_ref, ba_ref, wx_ref, bx_ref, lam_ref,
                  y_ref, h_ref, xbuf, wa_bd, wx_bd, a_s, b_s, hcar):
    del y_init
    t = pl.program_id(2)
    tt = xl_ref.shape[0]

    @pl.when(t == 0)
    def _():
        xbuf[0:SUBLANE, :] = jnp.zeros((SUBLANE, CW), f32)
        hcar[...] = jnp.zeros(hcar.shape, f32)
        _build_block_diag(wa_ref, wa_bd)
        _build_block_diag(wx_ref, wx_bd)

    xbuf[SUBLANE:SUBLANE + tt, :] = xl_ref[...]
    w = wc_ref[...]
    xc = xbuf[SUBLANE - 3:SUBLANE - 3 + tt, :] * w[0:1]
    for j in range(1, LRU_CONV_K):
        xc = xc + xbuf[SUBLANE - 3 + j:SUBLANE - 3 + j + tt, :] * w[j:j + 1]
    xc = xc + bc_ref[...]
    xbuf[0:SUBLANE, :] = xbuf[tt:tt + SUBLANE, :]

    a, bx = _lru_gates(xc, wa_bd[...], wx_bd[...], ba_ref[...], bx_ref[...], lam_ref[...])

    r8 = lax.broadcasted_iota(jnp.int32, (tt, CW), 0) % SUBLANE
    for s in (1, 2, 4):
        keep = r8 >= s
        a_sh = jnp.where(keep, pltpu.roll(a, s, 0), 1.0)
        b_sh = jnp.where(keep, pltpu.roll(bx, s, 0), 0.0)
        bx = bx + a * b_sh
        a = a * a_sh
    a_s[...] = a
    b_s[...] = bx

    def tile_step(j, h):
        r0 = pl.multiple_of(j * SUBLANE, SUBLANE)
        ht = b_s[pl.ds(r0, SUBLANE), :] + a_s[pl.ds(r0, SUBLANE), :] * h
        b_s[pl.ds(r0, SUBLANE), :] = ht
        return ht[SUBLANE - 1:SUBLANE, :]

    h_last = lax.fori_loop(0, tt // SUBLANE, tile_step, hcar[...])
    hcar[...] = h_last
    y_ref[...] = b_s[...].astype(bf16)
    h_ref[...] = b_s[tt - SUBLANE:tt, :]


def _lru_specs(layer, idx):
    return [pl.BlockSpec((None, LRU_CONV_K, CW), lambda *g: (layer, 0, idx(*g))),
            pl.BlockSpec((None, 1, CW), lambda *g: (layer, 0, idx(*g))),
            pl.BlockSpec((None, LRU_BLOCKS_PER_TILE, LRU_BLOCK_DIM, LRU_BLOCK_DIM),
                         lambda *g: (layer, idx(*g), 0, 0)),
            pl.BlockSpec((None, 1, CW), lambda *g: (layer, 0, idx(*g))),
            pl.BlockSpec((None, LRU_BLOCKS_PER_TILE, LRU_BLOCK_DIM, LRU_BLOCK_DIM),
                         lambda *g: (layer, idx(*g), 0, 0)),
            pl.BlockSpec((None, 1, CW), lambda *g: (layer, 0, idx(*g))),
            pl.BlockSpec((None, 1, CW), lambda *g: (layer, 0, idx(*g)))]


def _lru_prompt(y_init, z, lru_w, layer):
    nt = SEQ // TT_LRU
    return pl.pallas_call(
        _lru_p_kernel,
        grid=(BATCH, LRU_WIDTH // CW, nt),
        in_specs=[pl.BlockSpec(memory_space=pl.ANY),
                  pl.BlockSpec((TT_LRU, CW), lambda b, c, t: (b * nt + t, OFF_XL // CW + c))]
                 + _lru_specs(layer, lambda b, c, t: c),
        out_specs=[pl.BlockSpec((TT_LRU, CW), lambda b, c, t: (b * nt + t, c)),
                   pl.BlockSpec((None, SUBLANE, CW), lambda b, c, t: (b, 0, c))],
        out_shape=[jax.ShapeDtypeStruct((ROWS, LRU_WIDTH), bf16),
                   jax.ShapeDtypeStruct((BATCH, SUBLANE, LRU_WIDTH), f32)],
        input_output_aliases={0: 0},
        scratch_shapes=[pltpu.VMEM((TT_LRU + SUBLANE, CW), f32),
                        pltpu.VMEM((CW, CW), bf16), pltpu.VMEM((CW, CW), bf16),
                        pltpu.VMEM((TT_LRU, CW), f32), pltpu.VMEM((TT_LRU, CW), f32),
                        pltpu.VMEM((1, CW), f32)],
        compiler_params=_params(3, 40),
        name="lru_prompt",
    )(y_init, z, *lru_w)


def _seq_s_kernel(cb_ref, cc_ref, ch_ref, xl_ref, sc_ref, lc_ref, h0_ref, wsc_ref,
                  wc_ref, bc_ref, wa_ref, ba_ref, wx_ref, bx_ref, lam_ref,
                  yc_ref, yl_ref, scn_ref, hn_ref, wa_bd, wx_bd):
    nb = DEC_BATCH
    rows = lambda v, t: v[nb * t:nb * (t + 1), :]
    u = cc_ref[...] * ch_ref[...]
    cb = cb_ref[...]
    up = [sc_ref[j] for j in range(SCONV_K - 1)] + [rows(u, t) for t in range(DEC_SEQ)]
    w = wsc_ref[...]
    for t in range(DEC_SEQ):
        yc = up[t] * w[0:1]
        for j in range(1, SCONV_K):
            yc = yc + up[t + j] * w[j:j + 1]
        yc_ref[nb * t:nb * (t + 1), :] = (rows(cb, t) * yc).astype(bf16)
    for j in range(SCONV_K - 1):
        scn_ref[j] = up[DEC_SEQ + j]

    _build_block_diag(wa_ref, wa_bd)
    _build_block_diag(wx_ref, wx_bd)
    xl = xl_ref[...]
    xp = [lc_ref[j] for j in range(LRU_CONV_K - 1)] + [rows(xl, t) for t in range(DEC_SEQ)]
    wl = wc_ref[...]
    xcs = []
    for t in range(DEC_SEQ):
        xc = xp[t] * wl[0:1]
        for j in range(1, LRU_CONV_K):
            xc = xc + xp[t + j] * wl[j:j + 1]
        xcs.append(xc + bc_ref[...])
    xc = jnp.concatenate(xcs, axis=0)
    a, bx = _lru_gates(xc, wa_bd[...], wx_bd[...], ba_ref[...], bx_ref[...], lam_ref[...])
    h = h0_ref[...]
    for t in range(DEC_SEQ):
        h = rows(a, t) * h + rows(bx, t)
        yl_ref[nb * t:nb * (t + 1), :] = h.astype(bf16)
    hn_ref[...] = h


def _seq_sample(z, st_sconv_t, st_lconv_t, st_h, w_sconv, lru_w, layer):
    rblk = ROWS_P // ROWS_S
    zspec = lambda off: pl.BlockSpec((ROWS_S, CW), lambda c: (rblk, off // CW + c))
    return pl.pallas_call(
        _seq_s_kernel,
        grid=(CONV_WIDTH // CW,),
        in_specs=[zspec(OFF_CB), zspec(OFF_CC), zspec(OFF_CH), zspec(OFF_XL),
                  pl.BlockSpec((None, SCONV_K - 1, DEC_BATCH, CW), lambda c: (layer, 0, 0, c)),
                  pl.BlockSpec((None, LRU_CONV_K - 1, DEC_BATCH, CW), lambda c: (layer, 0, 0, c)),
                  pl.BlockSpec((None, DEC_BATCH, CW), lambda c: (layer, 0, c)),
                  pl.BlockSpec((None, SCONV_K, CW), lambda c: (layer, 0, c))]
                 + _lru_specs(layer, lambda c: c),
        out_specs=[pl.BlockSpec((ROWS_S, CW), lambda c: (0, c)),
                   pl.BlockSpec((ROWS_S, CW), lambda c: (0, c)),
                   pl.BlockSpec((SCONV_K - 1, DEC_BATCH, CW), lambda c: (0, 0, c)),
                   pl.BlockSpec((DEC_BATCH, CW), lambda c: (0, c))],
        out_shape=[jax.ShapeDtypeStruct((ROWS_S, CONV_WIDTH), bf16),
                   jax.ShapeDtypeStruct((ROWS_S, LRU_WIDTH), bf16),
                   jax.ShapeDtypeStruct((SCONV_K - 1, DEC_BATCH, CONV_WIDTH), f32),
                   jax.ShapeDtypeStruct((DEC_BATCH, LRU_WIDTH), f32)],
        scratch_shapes=[pltpu.VMEM((CW, CW), bf16), pltpu.VMEM((CW, CW), bf16)],
        compiler_params=_params(1, 32),
        name="seq_sample",
    )(z, z, z, z, st_sconv_t, st_lconv_t, st_h, w_sconv, *lru_w)


def _softmax_rows(s):
    m = jnp.max(s, axis=-1, keepdims=True)
    p = jnp.exp(s - m)
    return p * (1.0 / jnp.sum(p, axis=-1, keepdims=True))


def _mem_p_kernel(y_init, q_ref, k_ref, v_ref, y_ref):
    del y_init
    s = lax.dot_general(q_ref[...].astype(bf16), k_ref[...].astype(bf16), NT_DIMS,
                        preferred_element_type=f32) * (MEM_HEAD_DIM ** -0.5)
    pn = _softmax_rows(s).astype(bf16)
    y_ref[...] = jnp.dot(pn, v_ref[...].astype(bf16), preferred_element_type=f32).astype(bf16)


def _mem_prompt(y_init, z, mkv):
    nt = SEQ // TQ_MEM
    hw = MEM_HEAD_DIM
    return pl.pallas_call(
        _mem_p_kernel,
        grid=(BATCH, MEM_HEADS, nt),
        in_specs=[pl.BlockSpec(memory_space=pl.ANY),
                  pl.BlockSpec((TQ_MEM, hw), lambda b, h, t: (b * nt + t, OFF_QM // hw + h)),
                  pl.BlockSpec((MEM_LEN, hw), lambda b, h, t: (b, h)),
                  pl.BlockSpec((MEM_LEN, hw), lambda b, h, t: (b, MEM_HEADS + h))],
        out_specs=pl.BlockSpec((TQ_MEM, hw), lambda b, h, t: (b * nt + t, h)),
        out_shape=jax.ShapeDtypeStruct((ROWS, MEM_WIDTH), bf16),
        input_output_aliases={0: 0},
        compiler_params=_params(3, 32),
        name="mem_prompt",
    )(y_init, z, mkv, mkv)


SB_MEM = 4


QEXP = MEM_HEADS * DEC_SEQ


def _mem_s_kernel(q_ref, k_ref, v_ref, y_ref, qexp):
    hw = MEM_HEAD_DIM
    qexp[...] = jnp.zeros(qexp.shape, f32)
    for bb in range(SB_MEM):
        qb = q_ref[bb]
        for h in range(MEM_HEADS):
            qexp[DEC_SEQ * h:DEC_SEQ * (h + 1), hw * h:hw * (h + 1)] = qb[:, hw * h:hw * (h + 1)]
        s = lax.dot_general(qexp[...].astype(bf16), k_ref[bb].astype(bf16), NT_DIMS,
                            preferred_element_type=f32) * (MEM_HEAD_DIM ** -0.5)
        pn = _softmax_rows(s).astype(bf16)
        o = jnp.dot(pn, v_ref[bb].astype(bf16), preferred_element_type=f32)
        for h in range(MEM_HEADS):
            y_ref[bb, :, hw * h:hw * (h + 1)] = o[DEC_SEQ * h:DEC_SEQ * (h + 1), hw * h:hw * (h + 1)]


def _mem_sample(qm3, cache_k, cache_v, layer):
    cache_spec = pl.BlockSpec((None, SB_MEM, MEM_LEN, MEM_WIDTH), lambda i: (layer, i, 0, 0))
    return pl.pallas_call(
        _mem_s_kernel,
        grid=(DEC_BATCH // SB_MEM,),
        in_specs=[pl.BlockSpec((SB_MEM, DEC_SEQ, MEM_WIDTH), lambda i: (i, 0, 0)),
                  cache_spec, cache_spec],
        out_specs=pl.BlockSpec((SB_MEM, DEC_SEQ, MEM_WIDTH), lambda i: (i, 0, 0)),
        out_shape=jax.ShapeDtypeStruct((DEC_BATCH, DEC_SEQ, MEM_WIDTH), f32),
        scratch_shapes=[pltpu.VMEM((QEXP, MEM_WIDTH), f32)],
        compiler_params=_params(1, 40),
        name="mem_sample",
    )(qm3, cache_k, cache_v)


def _merge_kernel(y0, y1, y2, y3, g0, g1, g2, g3, wb_ref, o_ref):
    acc = None
    for b, (y, g) in enumerate(((y0, g0), (y1, g1), (y2, g2), (y3, g3))):
        proj = jnp.dot(y[...], wb_ref[b], preferred_element_type=f32)
        term = jax.nn.sigmoid(g[...].astype(f32)) * proj
        acc = term if acc is None else acc + term
    o_ref[...] = acc.astype(bf16)


def _merge(ys, zg, w_branch, layer):
    tn = TN_MERGE
    yspec = pl.BlockSpec((TM, BRANCH_WIDTH), lambda i, j: (i, 0))
    gspec = lambda b: pl.BlockSpec((TM, tn), lambda i, j: (i, (b * D_MODEL) // tn + j))
    return pl.pallas_call(
        _merge_kernel,
        grid=(zg.shape[0] // TM, D_MODEL // tn),
        in_specs=[yspec] * N_BRANCH + [gspec(b) for b in range(N_BRANCH)]
                 + [pl.BlockSpec((None, N_BRANCH, BRANCH_WIDTH, tn), lambda i, j: (layer, 0, 0, j))],
        out_specs=pl.BlockSpec((TM, tn), lambda i, j: (i, j)),
        out_shape=jax.ShapeDtypeStruct((zg.shape[0], D_MODEL), bf16),
        compiler_params=_params(2, 48),
        name="merge",
    )(*ys, zg, zg, zg, zg, w_branch)


def _oproj_kernel(m_ref, w_ref, x_ref, g_ref, b_ref, o_ref):
    d = jnp.dot(m_ref[...], w_ref[...], preferred_element_type=f32)
    o_ref[...] = _layer_norm(ALPHA * x_ref[...] + d, g_ref[0:1, :], b_ref[0:1, :])


def _oproj(merged, w_o, x, ln_g, ln_b, layer):
    tm = TM_OPROJ
    return pl.pallas_call(
        _oproj_kernel,
        grid=(x.shape[0] // tm,),
        in_specs=[pl.BlockSpec((tm, D_MODEL), lambda i: (i, 0)),
                  pl.BlockSpec((None, D_MODEL, D_MODEL), lambda i: (layer, 0, 0)),
                  pl.BlockSpec((tm, D_MODEL), lambda i: (i, 0)),
                  pl.BlockSpec((None, 2, D_MODEL), lambda i: (layer, 0, 0)),
                  pl.BlockSpec((None, 2, D_MODEL), lambda i: (layer, 0, 0))],
        out_specs=pl.BlockSpec((tm, D_MODEL), lambda i: (i, 0)),
        out_shape=jax.ShapeDtypeStruct(x.shape, f32),
        compiler_params=_params(1, 48),
        name="oproj_ln",
    )(merged, w_o, x, ln_g, ln_b)


def _swiglu_partial(xb, wg_ref, wu_ref, wd_ref):
    g = jnp.dot(xb, wg_ref[...], preferred_element_type=f32)
    u = jnp.dot(xb, wu_ref[...], preferred_element_type=f32)
    h = (jax.nn.silu(g) * u).astype(bf16)
    return jnp.dot(h, wd_ref[...], preferred_element_type=f32)


def _ffn_kernel(x_ref, wg_ref, wu_ref, wd_ref, g_ref, b_ref, o_ref, xb_ref):
    f = pl.program_id(1)

    @pl.when(f == 0)
    def _():
        xb_ref[...] = x_ref[...].astype(bf16)
        o_ref[...] = jnp.zeros(o_ref.shape, f32)

    o_ref[...] += _swiglu_partial(xb_ref[...], wg_ref, wu_ref, wd_ref)

    @pl.when(f == pl.num_programs(1) - 1)
    def _():
        o_ref[...] = _layer_norm(ALPHA * x_ref[...] + o_ref[...], g_ref[1:2, :], b_ref[1:2, :])


def _ffn(x, wg, wu, wd, ln_g, ln_b, layer, j):
    tm = TM_FFN
    w_in = pl.BlockSpec((None, D_MODEL, TF), lambda i, f: (j, 0, f))
    w_out = pl.BlockSpec((None, TF, D_MODEL), lambda i, f: (j, f, 0))
    xspec = pl.BlockSpec((tm, D_MODEL), lambda i, f: (i, 0))
    lnspec = pl.BlockSpec((None, 2, D_MODEL), lambda i, f: (layer, 0, 0))
    return pl.pallas_call(
        _ffn_kernel,
        grid=(x.shape[0] // tm, D_FF // TF),
        in_specs=[xspec, w_in, w_in, w_out, lnspec, lnspec],
        out_specs=xspec,
        out_shape=jax.ShapeDtypeStruct(x.shape, f32),
        scratch_shapes=[pltpu.VMEM((tm, D_MODEL), bf16)],
        compiler_params=_params(2, 52),
        name="ffn_dense",
    )(x, wg, wu, wd, ln_g, ln_b)


TG = 640


def _max_row_tiles(n_tokens):
    return (n_tokens * TOP_K + N_EXPERTS * (TG - 1)) // TG


def _router_kernel(x_ref, w_ref, p_ref, i_ref, cnt_ref, carry):
    @pl.when(pl.program_id(0) == 0)
    def _():
        carry[...] = jnp.zeros(carry.shape, f32)

    tm = x_ref.shape[0]
    logits = jnp.dot(x_ref[...], w_ref[...], precision=lax.Precision.HIGHEST,
                     preferred_element_type=f32)
    lane = lax.broadcasted_iota(jnp.int32, logits.shape, 1)
    logits = jnp.where(lane < N_EXPERTS, logits, -jnp.inf)
    m1 = jnp.max(logits, axis=-1, keepdims=True)
    i1 = jnp.min(jnp.where(logits == m1, lane, LANE), axis=-1, keepdims=True)
    rest = jnp.where(lane == i1, -jnp.inf, logits)
    m2 = jnp.max(rest, axis=-1, keepdims=True)
    i2 = jnp.min(jnp.where(rest == m2, lane, LANE), axis=-1, keepdims=True)
    e = jnp.exp(m2 - m1)
    p1 = 1.0 / (1.0 + e)
    p2 = e / (1.0 + e)
    sel = jnp.where(lane == i1, 1.0, jnp.where(lane == i2, 1.0, 0.0))
    r_i = lax.broadcasted_iota(jnp.int32, (tm, tm), 0)
    c_i = lax.broadcasted_iota(jnp.int32, (tm, tm), 1)
    tri = jnp.where(r_i > c_i, 1.0, 0.0).astype(bf16)
    before = jnp.dot(tri, sel.astype(bf16), preferred_element_type=f32) + carry[...]
    r1 = jnp.sum(jnp.where(lane == i1, before, 0.0), axis=-1, keepdims=True).astype(jnp.int32)
    r2 = jnp.sum(jnp.where(lane == i2, before, 0.0), axis=-1, keepdims=True).astype(jnp.int32)
    carry[...] = carry[...] + jnp.sum(sel, axis=0, keepdims=True)
    cnt_ref[...] = carry[...]
    p_ref[...] = jnp.where(lane == 0, p1, jnp.where(lane == 1, p2, 0.0))
    i_ref[...] = jnp.where(lane == 0, i1, jnp.where(lane == 1, i2,
                           jnp.where(lane == 2, r1, jnp.where(lane == 3, r2, 0))))


def _router(x, w_router_pad, j):
    n = x.shape[0]
    tm = TM_FFN
    return pl.pallas_call(
        _router_kernel,
        grid=(n // tm,),
        in_specs=[pl.BlockSpec((tm, D_MODEL), lambda i: (i, 0)),
                  pl.BlockSpec((None, D_MODEL, LANE), lambda i: (j, 0, 0))],
        out_specs=[pl.BlockSpec((tm, LANE), lambda i: (i, 0)),
                   pl.BlockSpec((tm, LANE), lambda i: (i, 0)),
                   pl.BlockSpec((1, LANE), lambda i: (0, 0))],
        out_shape=[jax.ShapeDtypeStruct((n, LANE), f32),
                   jax.ShapeDtypeStruct((n, LANE), jnp.int32),
                   jax.ShapeDtypeStruct((1, LANE), f32)],
        scratch_shapes=[pltpu.VMEM((1, LANE), f32)],
        compiler_params=_params(1, 40),
        name="router",
    )(x, w_router_pad)


def _routing_plan(iinfo, counts):
    cnt = counts[0, :N_EXPERTS].astype(jnp.int32)
    tiles = (cnt + (TG - 1)) // TG
    end_tile = jnp.cumsum(tiles)
    offset = (end_tile - tiles) * TG
    pos1 = offset[iinfo[:, 0]] + iinfo[:, 2]
    pos2 = offset[iinfo[:, 1]] + iinfo[:, 3]
    tile_ids = jnp.arange(_max_row_tiles(iinfo.shape[0]), dtype=jnp.int32)
    tile_expert = jnp.minimum(jnp.sum(tile_ids[:, None] >= end_tile[None, :], axis=1),
                              N_EXPERTS - 1).astype(jnp.int32)
    return pos1, pos2, tile_expert, end_tile[N_EXPERTS - 1:].astype(jnp.int32)


DMA_UNROLL = 8


def _row_copy(src, s, dst, d, sem):
    return pltpu.make_async_copy(src.at[pl.ds(s, 1)], dst.at[pl.ds(d, 1)], sem)


def _dispatch_kernel(pos1_ref, pos2_ref, x_ref, xs_in, xs_ref, sem):
    del xs_in
    tm = x_ref.shape[0]
    base = pl.program_id(0) * tm

    def issue(r, c):
        _row_copy(x_ref, r, xs_ref, pos1_ref[base + r], sem).start()
        _row_copy(x_ref, r, xs_ref, pos2_ref[base + r], sem).start()
        return c

    def drain(r, c):
        _row_copy(x_ref, r, xs_ref, pos1_ref[base + r], sem).wait()
        _row_copy(x_ref, r, xs_ref, pos2_ref[base + r], sem).wait()
        return c

    lax.fori_loop(0, tm, issue, 0, unroll=DMA_UNROLL)
    lax.fori_loop(0, tm, drain, 0, unroll=DMA_UNROLL)


def _dispatch(x, pos1, pos2):
    tm = TM_FFN
    n_rows = _max_row_tiles(x.shape[0]) * TG
    xs0 = jnp.zeros((n_rows, D_MODEL), f32)
    return pl.pallas_call(
        _dispatch_kernel,
        grid_spec=pltpu.PrefetchScalarGridSpec(
            num_scalar_prefetch=2,
            grid=(x.shape[0] // tm,),
            in_specs=[pl.BlockSpec((tm, D_MODEL), lambda i, p1, p2: (i, 0)),
                      pl.BlockSpec(memory_space=pl.ANY)],
            out_specs=pl.BlockSpec(memory_space=pl.ANY),
            scratch_shapes=[pltpu.SemaphoreType.DMA(())]),
        out_shape=jax.ShapeDtypeStruct((n_rows, D_MODEL), f32),
        input_output_aliases={3: 0},
        compiler_params=_params(1, 32),
        name="moe_dispatch",
    )(pos1, pos2, x, xs0)


def _gffn_kernel(te_ref, nt_ref, x_ref, wg_ref, wu_ref, wd_ref, o_ref, xb_ref):
    del te_ref
    f = pl.program_id(1)
    used = pl.program_id(0) < nt_ref[0]

    @pl.when(jnp.logical_and(jnp.logical_not(used), f == 0))
    def _():
        o_ref[...] = jnp.zeros(o_ref.shape, f32)

    @pl.when(jnp.logical_and(used, f == 0))
    def _():
        xb_ref[...] = x_ref[...].astype(bf16)
        o_ref[...] = jnp.zeros(o_ref.shape, f32)

    @pl.when(used)
    def _():
        o_ref[...] += _swiglu_partial(xb_ref[...], wg_ref, wu_ref, wd_ref)


def _grouped_ffn(xs, tile_expert, n_tiles, wg, wu, wd, j):
    n_f = D_FF // TF
    row = lambda i, nt: jnp.minimum(i, nt[0] - 1)
    fcol = lambda i, f, nt: jnp.where(i < nt[0], f, n_f - 1)
    xspec = pl.BlockSpec((TG, D_MODEL), lambda i, f, te, nt: (row(i, nt), 0))
    w_in = pl.BlockSpec((None, None, D_MODEL, TF),
                        lambda i, f, te, nt: (j, te[row(i, nt)], 0, fcol(i, f, nt)))
    w_out = pl.BlockSpec((None, None, TF, D_MODEL),
                         lambda i, f, te, nt: (j, te[row(i, nt)], fcol(i, f, nt), 0))
    return pl.pallas_call(
        _gffn_kernel,
        grid_spec=pltpu.PrefetchScalarGridSpec(
            num_scalar_prefetch=2,
            grid=(xs.shape[0] // TG, n_f),
            in_specs=[xspec, w_in, w_in, w_out],
            out_specs=pl.BlockSpec((TG, D_MODEL), lambda i, f, te, nt: (i, 0)),
            scratch_shapes=[pltpu.VMEM((TG, D_MODEL), bf16)]),
        out_shape=jax.ShapeDtypeStruct(xs.shape, f32),
        compiler_params=_params(2, 52),
        name="ffn_experts",
    )(tile_expert, n_tiles, xs, wg, wu, wd)


def _combine_kernel(pos1_ref, pos2_ref, x_ref, p_ref, y_ref, g_ref, b_ref, o_ref, buf1, buf2, sem):
    tm = x_ref.shape[0]
    base = pl.program_id(0) * tm

    def issue(r, c):
        _row_copy(y_ref, pos1_ref[base + r], buf1, r, sem).start()
        _row_copy(y_ref, pos2_ref[base + r], buf2, r, sem).start()
        return c

    def drain(r, c):
        _row_copy(y_ref, pos1_ref[base + r], buf1, r, sem).wait()
        _row_copy(y_ref, pos2_ref[base + r], buf2, r, sem).wait()
        return c

    lax.fori_loop(0, tm, issue, 0, unroll=DMA_UNROLL)
    lax.fori_loop(0, tm, drain, 0, unroll=DMA_UNROLL)
    p = p_ref[...]
    mixed = p[:, 0:1] * buf1[...] + p[:, 1:2] * buf2[...]
    o_ref[...] = _layer_norm(ALPHA * x_ref[...] + mixed, g_ref[1:2, :], b_ref[1:2, :])


def _combine(x, pinfo, y, pos1, pos2, ln_g, ln_b, layer):
    tm = TM_FFN
    xspec = pl.BlockSpec((tm, D_MODEL), lambda i, p1, p2: (i, 0))
    lnspec = pl.BlockSpec((None, 2, D_MODEL), lambda i, p1, p2: (layer, 0, 0))
    return pl.pallas_call(
        _combine_kernel,
        grid_spec=pltpu.PrefetchScalarGridSpec(
            num_scalar_prefetch=2,
            grid=(x.shape[0] // tm,),
            in_specs=[xspec, pl.BlockSpec((tm, LANE), lambda i, p1, p2: (i, 0)),
                      pl.BlockSpec(memory_space=pl.ANY), lnspec, lnspec],
            out_specs=xspec,
            scratch_shapes=[pltpu.VMEM((tm, D_MODEL), f32), pltpu.VMEM((tm, D_MODEL), f32),
                            pltpu.SemaphoreType.DMA(())]),
        out_shape=jax.ShapeDtypeStruct(x.shape, f32),
        compiler_params=_params(1, 48),
        name="moe_combine",
    )(pos1, pos2, x, pinfo, y, ln_g, ln_b)


def _moe(x, w_router_pad, wg, wu, wd, ln_g, ln_b, layer, j):
    pinfo, iinfo, counts = _router(x, w_router_pad, j)
    pos1, pos2, tile_expert, n_tiles = _routing_plan(iinfo, counts)
    xs = _dispatch(x, pos1, pos2)
    y = _grouped_ffn(xs, tile_expert, n_tiles, wg, wu, wd, j)
    return _combine(x, pinfo, y, pos1, pos2, ln_g, ln_b, layer)


def kernel(x_prompt, x_sample, mem_prompt, cache_win_k, cache_win_v, cache_mem_k, cache_mem_v,
           state_sconv, state_lru_conv, state_lru_h, w_mix_in, sinks, w_sconv, w_lru_conv, b_lru_conv,
           w_lru_a, b_lru_a, w_lru_x, b_lru_x, lru_lambda, w_mem_kv, w_branch, w_o, ln_g, ln_b,
           w_router, w_ffn_gate, w_ffn_up, w_ffn_down, w_exp_gate, w_exp_up, w_exp_down):
    x = jnp.concatenate([x_prompt.reshape(ROWS_P, D_MODEL),
                         x_sample.transpose(1, 0, 2).reshape(ROWS_S, D_MODEL)], axis=0)
    mem = mem_prompt.reshape(BATCH * MEM_LEN, D_MODEL)
    w_in_b = w_mix_in.astype(bf16)
    w_gate_b = w_in_b[:, :, OFF_G:]
    w_mkv_b = w_mem_kv.astype(bf16)
    w_br_b = w_branch.astype(bf16)
    w_o_b = w_o.astype(bf16)
    wfg, wfu, wfd = w_ffn_gate.astype(bf16), w_ffn_up.astype(bf16), w_ffn_down.astype(bf16)
    weg, weu, wed = w_exp_gate.astype(bf16), w_exp_up.astype(bf16), w_exp_down.astype(bf16)
    w_router_pad = jnp.pad(w_router, ((0, 0), (0, 0), (0, LANE - N_EXPERTS)))
    ck = cache_win_k.reshape(DEPTH, DEC_BATCH, WINDOW, KV_WIDTH)
    cv = cache_win_v.reshape(DEPTH, DEC_BATCH, WINDOW, KV_WIDTH)
    cmk = cache_mem_k.reshape(DEPTH, DEC_BATCH, MEM_LEN, MEM_WIDTH)
    cmv = cache_mem_v.reshape(DEPTH, DEC_BATCH, MEM_LEN, MEM_WIDTH)
    st_sc_t = state_sconv.transpose(0, 2, 1, 3)
    st_lc_t = state_lru_conv.transpose(0, 2, 1, 3)
    lru_w = (w_lru_conv, b_lru_conv.reshape(DEPTH, 1, LRU_WIDTH), w_lru_a,
             b_lru_a.reshape(DEPTH, 1, LRU_WIDTH), w_lru_x, b_lru_x.reshape(DEPTH, 1, LRU_WIDTH),
             lru_lambda.reshape(DEPTH, 1, LRU_WIDTH))
    tab_p = _rope_tables(jnp.arange(SEQ, dtype=jnp.int32))
    tab_s = _rope_tables(PAST_LEN + jnp.arange(DEC_SEQ, dtype=jnp.int32))

    def to_b_major(v):
        return v.reshape(DEC_SEQ, DEC_BATCH, v.shape[-1]).transpose(1, 0, 2)

    def to_t_major(v):
        return v.transpose(1, 0, 2).reshape(ROWS_S, v.shape[-1])

    outs = [[] for _ in range(12)]
    for l in range(DEPTH):
        z = _matmul(x, w_in_b, l, TM, TN_MIX, n=OFF_G)
        mkv = _matmul(mem, w_mkv_b, l, BATCH * MEM_LEN, TN_MKV)
        zs = z[ROWS_P:]

        def all_rows(y_s):
            return lax.dynamic_update_slice(jnp.zeros((ROWS, y_s.shape[-1]), bf16), y_s, (ROWS_P, 0))

        ya_s, nk_s, nv_s = _attn_sample(to_b_major(zs[:, :OFF_CB]), ck, cv, sinks, tab_s, l)
        yc_s, yl_s, sc_s, h_s = _seq_sample(z, st_sc_t, st_lc_t, state_lru_h, w_sconv, lru_w, l)
        ym_s = _mem_sample(to_b_major(zs[:, OFF_QM:OFF_G]), cmk, cmv, l)
        zg, ya, krot_p = _gates_attn_prompt(all_rows(to_t_major(ya_s).astype(bf16)), x, w_gate_b, z,
                                            sinks, tab_p, l)
        yc, sc_p = _sconv_prompt(all_rows(yc_s), z, w_sconv, l)
        yl, h_p = _lru_prompt(all_rows(yl_s), z, lru_w, l)
        ym = _mem_prompt(all_rows(to_t_major(ym_s).astype(bf16)), z, mkv)
        merged = _merge((ya, yc, yl, ym), zg, w_br_b, l)
        x = _oproj(merged, w_o_b, x, ln_g, ln_b, l)
        j = l // 2
        if l % 2 == 0:
            x = _ffn(x, wfg, wfu, wfd, ln_g, ln_b, l, j)
        else:
            x = _moe(x, w_router_pad, weg, weu, wed, ln_g, ln_b, l, j)

        def tail(nrows, lo, hi):
            return jnp.stack([z[(b + 1) * SEQ - nrows:(b + 1) * SEQ, lo:hi] for b in range(BATCH)])

        kv_shape = (BATCH, WINDOW, KV_HEADS, HEAD_DIM)
        outs[0].append(krot_p.reshape(BATCH, SEQ, KV_WIDTH)[:, SEQ - WINDOW:].reshape(kv_shape))
        outs[1].append(tail(WINDOW, OFF_V, OFF_CB).reshape(kv_shape))
        mem_shape = (BATCH, MEM_LEN, MEM_HEADS, MEM_HEAD_DIM)
        outs[2].append(mkv[:, :MEM_WIDTH].reshape(mem_shape))
        outs[3].append(mkv[:, MEM_WIDTH:].reshape(mem_shape))
        outs[4].append(sc_p[:, SUBLANE - (SCONV_K - 1):])
        outs[5].append(tail(LRU_CONV_K - 1, OFF_XL, OFF_QM))
        outs[6].append(h_p[:, SUBLANE - 1])
        kvs_shape = (DEC_BATCH, WINDOW, KV_HEADS, HEAD_DIM)
        outs[7].append(nk_s.reshape(kvs_shape))
        outs[8].append(nv_s.reshape(kvs_shape))
        outs[9].append(sc_s.transpose(1, 0, 2))
        outs[10].append(zs[DEC_BATCH:, OFF_XL:OFF_QM].reshape(LRU_CONV_K - 1, DEC_BATCH, LRU_WIDTH
                                                              ).transpose(1, 0, 2))
        outs[11].append(h_s)

    y_prompt = x[:ROWS_P].reshape(BATCH, SEQ, D_MODEL)
    y_sample = to_b_major(x[ROWS_P:])
    return (y_prompt, y_sample) + tuple(jnp.stack(o) for o in outs)
```

```python
import functools

import jax
import jax.numpy as jnp
from jax import lax
from jax.experimental import pallas as pl
from jax.experimental.pallas import tpu as pltpu

f32 = jnp.float32
bf16 = jnp.bfloat16

D_MODEL = 2048
BATCH = 2
SEQ = 4096
DEPTH = 4
DEC_BATCH = 32
DEC_SEQ = 4
PAST_LEN = 16384
N_HEADS = 16
KV_HEADS = 4
GQA_GROUP = N_HEADS // KV_HEADS
HEAD_DIM = 64
ROPE_DIM = HEAD_DIM // 4
ROPE_THETA = 500000.0
WINDOW = 128
ATTN_WIDTH = N_HEADS * HEAD_DIM
KV_WIDTH = KV_HEADS * HEAD_DIM
CONV_WIDTH = D_MODEL // 2
SCONV_K = 3
LRU_WIDTH = D_MODEL // 2
LRU_BLOCKS = 16
LRU_BLOCK_DIM = LRU_WIDTH // LRU_BLOCKS
LRU_CONV_K = 4
LRU_C = 8.0
MEM_LEN = 256
MEM_HEADS = 4
MEM_HEAD_DIM = 256
MEM_WIDTH = MEM_HEADS * MEM_HEAD_DIM
N_BRANCH = 4
BRANCH_WIDTH = D_MODEL // 2
D_FF = 5632
N_EXPERTS = 8
TOP_K = 2
ALPHA = (2 * DEPTH) ** 0.25
LN_EPS = 1e-5
NEG_INF = -1e30

ROWS_P = BATCH * SEQ
ROWS_S = DEC_BATCH * DEC_SEQ
ROWS = ROWS_P + ROWS_S

OFF_Q = 0
OFF_K = OFF_Q + ATTN_WIDTH
OFF_V = OFF_K + KV_WIDTH
OFF_CB = OFF_V + KV_WIDTH
OFF_CC = OFF_CB + CONV_WIDTH
OFF_CH = OFF_CC + CONV_WIDTH
OFF_XL = OFF_CH + CONV_WIDTH
OFF_QM = OFF_XL + LRU_WIDTH
OFF_G = OFF_QM + MEM_WIDTH
MIX_IN = OFF_G + N_BRANCH * D_MODEL

LANE = 128
SUBLANE = 8
BF16_SUBLANE = 16
MIB = 1024 * 1024
TM = 832
TN_MIX = 1664
TN_MKV = 512
TN_MERGE = 512
TM_OPROJ = 416
TM_FFN = 640
TF = 512
CW = 512
TT_CONV = 1024
TT_LRU = 512
TQ_MEM = 2048

NT_DIMS = (((1,), (1,)), ((), ()))


def _params(n_axes, vmem_mib):
    return pltpu.CompilerParams(dimension_semantics=("arbitrary",) * n_axes,
                                vmem_limit_bytes=vmem_mib * MIB)


def _layer_norm(v, g, b):
    mu = jnp.mean(v, axis=-1, keepdims=True)
    vc = v - mu
    var = jnp.mean(vc * vc, axis=-1, keepdims=True)
    return vc * lax.rsqrt(var + LN_EPS) * g + b


def _expm1(x):
    fact = [1.0]
    for k in range(1, 10):
        fact.append(fact[-1] * k)
    p = 1.0 / fact[9]
    for k in range(8, 0, -1):
        p = p * x + 1.0 / fact[k]
    return jnp.where(jnp.abs(x) < 0.5, x * p, jnp.exp(x) - 1.0)


def _cast_rider(w, j, n_steps, step_of):
    rows, cols = w.shape[1:]
    n_blocks = max(n for n in range(1, n_steps + 1)
                   if rows % n == 0 and (rows // n) % BF16_SUBLANE == 0)
    br = rows // n_blocks
    blk = lambda *g: jnp.minimum(step_of(*g), n_blocks - 1)
    return (pl.BlockSpec((None, br, cols), lambda *g: (j, blk(*g), 0)),
            pl.BlockSpec((br, cols), lambda *g: (blk(*g), 0)),
            jax.ShapeDtypeStruct((rows, cols), bf16))


def _mm_kernel(x_ref, w_ref, o_ref, xb_ref):
    @pl.when(pl.program_id(1) == 0)
    def _():
        xb_ref[...] = x_ref[...].astype(bf16)

    o_ref[...] = jnp.dot(xb_ref[...], w_ref[...], preferred_element_type=f32)


def _matmul(x, w_stack, layer, tm, tn, n=None):
    m, k = x.shape
    n = w_stack.shape[-1] if n is None else n
    vmem = 2 * (tm * k * 4 + k * tn * 2 + tm * tn * 4) + tm * k * 2
    return pl.pallas_call(
        _mm_kernel,
        grid=(m // tm, n // tn),
        in_specs=[pl.BlockSpec((tm, k), lambda i, j: (i, 0)),
                  pl.BlockSpec((None, k, tn), lambda i, j: (layer, 0, j))],
        out_specs=pl.BlockSpec((tm, tn), lambda i, j: (i, j)),
        out_shape=jax.ShapeDtypeStruct((m, n), f32),
        scratch_shapes=[pltpu.VMEM((tm, k), bf16)],
        compiler_params=_params(2, vmem // MIB + 8),
        name="matmul",
    )(x, w_stack)


def _rope_tables(pos):
    half = ROPE_DIM // 2
    inv_freq = ROPE_THETA ** (-jnp.arange(half, dtype=f32) * (2.0 / ROPE_DIM))
    ang = pos.astype(f32)[:, None] * inv_freq[None, :]
    cos = jnp.cos(ang)
    sin = jnp.sin(ang)
    p = pos.shape[0]
    rest = HEAD_DIM - ROPE_DIM
    c = jnp.concatenate([cos, cos, jnp.ones((p, rest), f32)], axis=1)
    sa = jnp.concatenate([-sin, jnp.zeros((p, HEAD_DIM - half), f32)], axis=1)
    sb = jnp.concatenate([jnp.zeros((p, half), f32), sin, jnp.zeros((p, rest), f32)], axis=1)
    rep = LANE // HEAD_DIM
    return jnp.tile(c, (1, rep)), jnp.tile(sa, (1, rep)), jnp.tile(sb, (1, rep))


def _rope(x, c, sa, sb):
    half = ROPE_DIM // 2
    chunks = []
    for j in range(x.shape[1] // LANE):
        xc = x[:, LANE * j:LANE * (j + 1)]
        chunks.append(xc * c + pltpu.roll(xc, LANE - half, 1) * sa + pltpu.roll(xc, half, 1) * sb)
    return chunks[0] if len(chunks) == 1 else jnp.concatenate(chunks, axis=1)


def _sink_softmax(s, sink):
    m = jnp.maximum(jnp.max(s, axis=-1, keepdims=True), sink)
    p = jnp.exp(s - m)
    denom = jnp.sum(p, axis=-1, keepdims=True) + jnp.exp(sink - m)
    return p * (1.0 / denom)


def _attn_block(n, sinks_ref, q_ref, k_ref, v_ref, c_ref, sa_ref, sb_ref, y_ref, kr_ref,
                kprev, vprev, layer, before_head=None):
    @pl.when(n == 0)
    def _():
        kprev[...] = jnp.zeros(kprev.shape, f32)
        vprev[...] = jnp.zeros(vprev.shape, f32)

    c, sa, sb = c_ref[...], sa_ref[...], sb_ref[...]
    q = _rope(q_ref[...], c, sa, sb)
    kc = _rope(k_ref[...], c, sa, sb)
    vc = v_ref[...]
    kr_ref[...] = kc
    kall = jnp.concatenate([kprev[...], kc], axis=0).astype(bf16)
    vall = jnp.concatenate([vprev[...], vc], axis=0).astype(bf16)
    kprev[...] = kc
    vprev[...] = vc

    rows = GQA_GROUP * WINDOW
    qi = lax.broadcasted_iota(jnp.int32, (rows, 2 * WINDOW), 0) % WINDOW
    kj = lax.broadcasted_iota(jnp.int32, (rows, 2 * WINDOW), 1)
    kmin = jnp.where(n > 0, 0, WINDOW)
    mask = (kj > qi) & (kj <= qi + WINDOW) & (kj >= kmin)
    rg = lax.broadcasted_iota(jnp.int32, (rows, 1), 0) // WINDOW
    for h in range(KV_HEADS):
        if before_head is not None:
            before_head(h)
        qh = jnp.concatenate(
            [q[:, (GQA_GROUP * h + g) * HEAD_DIM:(GQA_GROUP * h + g + 1) * HEAD_DIM]
             for g in range(GQA_GROUP)], axis=0).astype(bf16)
        kh = kall[:, HEAD_DIM * h:HEAD_DIM * (h + 1)]
        vh = vall[:, HEAD_DIM * h:HEAD_DIM * (h + 1)]
        s = lax.dot_general(qh, kh, NT_DIMS, preferred_element_type=f32) * (HEAD_DIM ** -0.5)
        s = jnp.where(mask, s, NEG_INF)
        sink = jnp.zeros((rows, 1), f32)
        for g in range(GQA_GROUP):
            sink = jnp.where(rg == g, sinks_ref[layer, GQA_GROUP * h + g], sink)
        pn = _sink_softmax(s, sink).astype(bf16)
        o = jnp.dot(pn, vh, preferred_element_type=f32)
        for g in range(GQA_GROUP):
            hd = GQA_GROUP * h + g
            y_ref[:, hd * HEAD_DIM:(hd + 1) * HEAD_DIM] = o[WINDOW * g:WINDOW * (g + 1), :].astype(bf16)


GA_ROWS = 8
GA_COLS = 8
TM_GA = ROWS // GA_ROWS
TN_GA = N_BRANCH * D_MODEL // GA_COLS


def _gates_attn_kernel(*refs, layer, has_cast):
    if has_cast:
        (y_init, sinks_ref, x_ref, w_ref, q_ref, k_ref, v_ref, c_ref, sa_ref, sb_ref, cast_in,
         zg_ref, y_ref, kr_ref, cast_out, xb_ref, kprev, vprev) = refs
        cast_out[...] = cast_in[...].astype(bf16)
    else:
        (y_init, sinks_ref, x_ref, w_ref, q_ref, k_ref, v_ref, c_ref, sa_ref, sb_ref,
         zg_ref, y_ref, kr_ref, xb_ref, kprev, vprev) = refs
    del y_init
    j = pl.program_id(1)
    step = pl.program_id(0) * GA_COLS + j

    @pl.when(j == 0)
    def _():
        xb_ref[...] = x_ref[...].astype(bf16)

    cw = TN_GA // KV_HEADS

    def gate_columns(h):
        zg_ref[:, cw * h:cw * (h + 1)] = jnp.dot(
            xb_ref[...], w_ref[:, cw * h:cw * (h + 1)], preferred_element_type=f32).astype(zg_ref.dtype)

    _attn_block(step % (SEQ // WINDOW), sinks_ref, q_ref, k_ref, v_ref, c_ref, sa_ref, sb_ref,
                y_ref, kr_ref, kprev, vprev, layer, before_head=gate_columns)


def _gates_attn_prompt(y_init, x, w_gate_b, z, sinks, tables, layer, cast_w=None, cast_j=0):
    nblk = SEQ // WINDOW
    assert GA_ROWS * GA_COLS == BATCH * nblk
    step = lambda i, j: i * GA_COLS + j
    tab = pl.BlockSpec((WINDOW, LANE), lambda i, j: (step(i, j) % nblk, 0))
    in_specs = [pl.BlockSpec(memory_space=pl.ANY),
                pl.BlockSpec(memory_space=pltpu.SMEM),
                pl.BlockSpec((TM_GA, D_MODEL), lambda i, j: (i, 0), pipeline_mode=pl.Buffered(1)),
                pl.BlockSpec((None, D_MODEL, TN_GA), lambda i, j: (layer, 0, j)),
                pl.BlockSpec((WINDOW, ATTN_WIDTH), lambda i, j: (step(i, j), OFF_Q // ATTN_WIDTH)),
                pl.BlockSpec((WINDOW, KV_WIDTH), lambda i, j: (step(i, j), OFF_K // KV_WIDTH)),
                pl.BlockSpec((WINDOW, KV_WIDTH), lambda i, j: (step(i, j), OFF_V // KV_WIDTH)),
                tab, tab, tab]
    out_specs = [pl.BlockSpec((TM_GA, TN_GA), lambda i, j: (i, j)),
                 pl.BlockSpec((WINDOW, ATTN_WIDTH), lambda i, j: (step(i, j), 0)),
                 pl.BlockSpec((WINDOW, KV_WIDTH), lambda i, j: (step(i, j), 0))]
    out_shape = [jax.ShapeDtypeStruct((ROWS, N_BRANCH * D_MODEL), bf16),
                 jax.ShapeDtypeStruct((ROWS, ATTN_WIDTH), bf16),
                 jax.ShapeDtypeStruct((ROWS_P, KV_WIDTH), f32)]
    args = [y_init, sinks, x, w_gate_b, z, z, z, *tables]
    if cast_w is not None:
        c_in, c_out, c_shape = _cast_rider(cast_w, cast_j, GA_ROWS * GA_COLS, step)
        in_specs.append(c_in)
        out_specs.append(c_out)
        out_shape.append(c_shape)
        args.append(cast_w)
    return pl.pallas_call(
        functools.partial(_gates_attn_kernel, layer=layer, has_cast=cast_w is not None),
        grid=(GA_ROWS, GA_COLS),
        in_specs=in_specs,
        out_specs=out_specs,
        out_shape=out_shape,
        scratch_shapes=[pltpu.VMEM((TM_GA, D_MODEL), bf16),
                        pltpu.VMEM((WINDOW, KV_WIDTH), f32), pltpu.VMEM((WINDOW, KV_WIDTH), f32)],
        input_output_aliases={0: 1},
        compiler_params=_params(2, 52),
        name="gates_attn_prompt",
    )(*args)


SB_ATTN = 8
KALL = WINDOW + SUBLANE


def _attn_s_kernel(sinks_ref, z_ref, ck_ref, cv_ref, c_ref, sa_ref, sb_ref, y_ref, nk_ref, nv_ref,
                   kall, vall, qs, *, layer):
    c, sa, sb = c_ref[...], sa_ref[...], sb_ref[...]
    rows = GQA_GROUP * DEC_SEQ
    kall[WINDOW:KALL, :] = jnp.zeros((SUBLANE, KV_WIDTH), f32)
    vall[WINDOW:KALL, :] = jnp.zeros((SUBLANE, KV_WIDTH), f32)
    qt = lax.broadcasted_iota(jnp.int32, (rows, KALL), 0) % DEC_SEQ
    kj = lax.broadcasted_iota(jnp.int32, (rows, KALL), 1)
    mask = jnp.where(kj < WINDOW, kj - qt, qt - (kj - WINDOW) + 1) > 0
    rg = lax.broadcasted_iota(jnp.int32, (rows, 1), 0) // DEC_SEQ
    for bb in range(SB_ATTN):
        zb = z_ref[bb]
        q = _rope(zb[:, OFF_Q:OFF_Q + ATTN_WIDTH], c, sa, sb)
        kn = _rope(zb[:, OFF_K:OFF_K + KV_WIDTH], c, sa, sb)
        vn = zb[:, OFF_V:OFF_V + KV_WIDTH]
        kall[0:WINDOW, :] = ck_ref[bb]
        vall[0:WINDOW, :] = cv_ref[bb]
        kall[WINDOW:WINDOW + DEC_SEQ, :] = kn
        vall[WINDOW:WINDOW + DEC_SEQ, :] = vn
        nk_ref[bb] = kall[DEC_SEQ:DEC_SEQ + WINDOW, :]
        nv_ref[bb] = vall[DEC_SEQ:DEC_SEQ + WINDOW, :]
        for h in range(KV_HEADS):
            for g in range(GQA_GROUP):
                hd = GQA_GROUP * h + g
                qs[DEC_SEQ * g:DEC_SEQ * (g + 1), :] = q[:, hd * HEAD_DIM:(hd + 1) * HEAD_DIM]
            qh = qs[...].astype(bf16)
            kh = kall[:, HEAD_DIM * h:HEAD_DIM * (h + 1)].astype(bf16)
            vh = vall[:, HEAD_DIM * h:HEAD_DIM * (h + 1)].astype(bf16)
            s = lax.dot_general(qh, kh, NT_DIMS, preferred_element_type=f32) * (HEAD_DIM ** -0.5)
            s = jnp.where(mask, s, NEG_INF)
            sink = jnp.zeros((rows, 1), f32)
            for g in range(GQA_GROUP):
                sink = jnp.where(rg == g, sinks_ref[layer, GQA_GROUP * h + g], sink)
            pn = _sink_softmax(s, sink).astype(bf16)
            o = jnp.dot(pn, vh, preferred_element_type=f32)
            for g in range(GQA_GROUP):
                hd = GQA_GROUP * h + g
                y_ref[bb, :, hd * HEAD_DIM:(hd + 1) * HEAD_DIM] = o[DEC_SEQ * g:DEC_SEQ * (g + 1), :]


def _attn_sample(zs3, cache_k, cache_v, sinks, tables, layer):
    qkv = ATTN_WIDTH + 2 * KV_WIDTH
    return pl.pallas_call(
        functools.partial(_attn_s_kernel, layer=layer),
        grid=(DEC_BATCH // SB_ATTN,),
        in_specs=[pl.BlockSpec(memory_space=pltpu.SMEM),
                  pl.BlockSpec((SB_ATTN, DEC_SEQ, qkv), lambda i: (i, 0, 0)),
                  pl.BlockSpec((None, SB_ATTN, WINDOW, KV_WIDTH), lambda i: (layer, i, 0, 0)),
                  pl.BlockSpec((None, SB_ATTN, WINDOW, KV_WIDTH), lambda i: (layer, i, 0, 0)),
                  pl.BlockSpec((DEC_SEQ, LANE), lambda i: (0, 0)),
                  pl.BlockSpec((DEC_SEQ, LANE), lambda i: (0, 0)),
                  pl.BlockSpec((DEC_SEQ, LANE), lambda i: (0, 0))],
        out_specs=[pl.BlockSpec((SB_ATTN, DEC_SEQ, ATTN_WIDTH), lambda i: (i, 0, 0)),
                   pl.BlockSpec((SB_ATTN, WINDOW, KV_WIDTH), lambda i: (i, 0, 0)),
                   pl.BlockSpec((SB_ATTN, WINDOW, KV_WIDTH), lambda i: (i, 0, 0))],
        out_shape=[jax.ShapeDtypeStruct((DEC_BATCH, DEC_SEQ, ATTN_WIDTH), f32),
                   jax.ShapeDtypeStruct((DEC_BATCH, WINDOW, KV_WIDTH), f32),
                   jax.ShapeDtypeStruct((DEC_BATCH, WINDOW, KV_WIDTH), f32)],
        scratch_shapes=[pltpu.VMEM((KALL, KV_WIDTH), f32), pltpu.VMEM((KALL, KV_WIDTH), f32),
                        pltpu.VMEM((GQA_GROUP * DEC_SEQ, HEAD_DIM), f32)],
        compiler_params=_params(1, 32),
        name="attn_sample",
    )(sinks, zs3, cache_k, cache_v, *tables)


def _sconv_p_kernel(y_init, cb_ref, cc_ref, ch_ref, w_ref, y_ref, st_ref, ubuf):
    del y_init
    t = pl.program_id(2)
    tt = cc_ref.shape[0]

    @pl.when(t == 0)
    def _():
        ubuf[0:SUBLANE, :] = jnp.zeros((SUBLANE, CW), f32)

    ubuf[SUBLANE:SUBLANE + tt, :] = cc_ref[...] * ch_ref[...]
    w = w_ref[...]
    yc = ubuf[SUBLANE - 2:SUBLANE - 2 + tt, :] * w[0:1]
    yc = yc + ubuf[SUBLANE - 1:SUBLANE - 1 + tt, :] * w[1:2]
    yc = yc + ubuf[SUBLANE:SUBLANE + tt, :] * w[2:3]
    y_ref[...] = (cb_ref[...] * yc).astype(bf16)
    last = ubuf[tt:tt + SUBLANE, :]
    st_ref[...] = last
    ubuf[0:SUBLANE, :] = last


def _sconv_prompt(y_init, z, w_sconv, layer):
    nt = SEQ // TT_CONV
    zspec = lambda off: pl.BlockSpec((TT_CONV, CW), lambda b, c, t: (b * nt + t, off // CW + c))
    return pl.pallas_call(
        _sconv_p_kernel,
        grid=(BATCH, CONV_WIDTH // CW, nt),
        in_specs=[pl.BlockSpec(memory_space=pl.ANY), zspec(OFF_CB), zspec(OFF_CC), zspec(OFF_CH),
                  pl.BlockSpec((None, SCONV_K, CW), lambda b, c, t: (layer, 0, c))],
        out_specs=[pl.BlockSpec((TT_CONV, CW), lambda b, c, t: (b * nt + t, c)),
                   pl.BlockSpec((None, SUBLANE, CW), lambda b, c, t: (b, 0, c))],
        out_shape=[jax.ShapeDtypeStruct((ROWS, CONV_WIDTH), bf16),
                   jax.ShapeDtypeStruct((BATCH, SUBLANE, CONV_WIDTH), f32)],
        scratch_shapes=[pltpu.VMEM((TT_CONV + SUBLANE, CW), f32)],
        input_output_aliases={0: 0},
        compiler_params=_params(3, 40),
        name="sconv_prompt",
    )(y_init, z, z, z, w_sconv)


LRU_BLOCKS_PER_TILE = CW // LRU_BLOCK_DIM


def _build_block_diag(w_ref, bd_ref):
    bd_ref[...] = jnp.zeros(bd_ref.shape, bf16)
    for n in range(LRU_BLOCKS_PER_TILE):
        lo, hi = LRU_BLOCK_DIM * n, LRU_BLOCK_DIM * (n + 1)
        bd_ref[lo:hi, lo:hi] = w_ref[n].astype(bf16)


def _lru_gates(xc, wa_bd, wx_bd, ba, bx, lam):
    xcb = xc.astype(bf16)
    r = jax.nn.sigmoid(jnp.dot(xcb, wa_bd, preferred_element_type=f32) + ba)
    i = jax.nn.sigmoid(jnp.dot(xcb, wx_bd, preferred_element_type=f32) + bx)
    log_a = -LRU_C * r * jax.nn.softplus(-lam)
    a = jnp.exp(log_a)
    mult = jnp.sqrt(-_expm1(2.0 * log_a))
    return a, mult * (i * xc)


def _lru_p_kernel(y_init, xl_ref, wc_ref, bc_ref, wa_ref, ba_ref, wx_ref, bx_ref, lam_ref,
                  y_ref, h_ref, xbuf, wa_bd, wx_bd, a_s, b_s, hcar):
    del y_init
    t = pl.program_id(2)
    tt = xl_ref.shape[0]

    @pl.when(t == 0)
    def _():
        xbuf[0:SUBLANE, :] = jnp.zeros((SUBLANE, CW), f32)
        hcar[...] = jnp.zeros(hcar.shape, f32)
        _build_block_diag(wa_ref, wa_bd)
        _build_block_diag(wx_ref, wx_bd)

    xbuf[SUBLANE:SUBLANE + tt, :] = xl_ref[...]
    w = wc_ref[...]
    xc = xbuf[SUBLANE - 3:SUBLANE - 3 + tt, :] * w[0:1]
    for j in range(1, LRU_CONV_K):
        xc = xc + xbuf[SUBLANE - 3 + j:SUBLANE - 3 + j + tt, :] * w[j:j + 1]
    xc = xc + bc_ref[...]
    xbuf[0:SUBLANE, :] = xbuf[tt:tt + SUBLANE, :]

    a, bx = _lru_gates(xc, wa_bd[...], wx_bd[...], ba_ref[...], bx_ref[...], lam_ref[...])

    r8 = lax.broadcasted_iota(jnp.int32, (tt, CW), 0) % SUBLANE
    for s in (1, 2, 4):
        keep = r8 >= s
        a_sh = jnp.where(keep, pltpu.roll(a, s, 0), 1.0)
        b_sh = jnp.where(keep, pltpu.roll(bx, s, 0), 0.0)
        bx = bx + a * b_sh
        a = a * a_sh
    a_s[...] = a
    b_s[...] = bx

    def tile_step(j, h):
        r0 = pl.multiple_of(j * SUBLANE, SUBLANE)
        ht = b_s[pl.ds(r0, SUBLANE), :] + a_s[pl.ds(r0, SUBLANE), :] * h
        b_s[pl.ds(r0, SUBLANE), :] = ht
        return ht[SUBLANE - 1:SUBLANE, :]

    h_last = lax.fori_loop(0, tt // SUBLANE, tile_step, hcar[...])
    hcar[...] = h_last
    y_ref[...] = b_s[...].astype(bf16)
    h_ref[...] = b_s[tt - SUBLANE:tt, :]


def _lru_specs(layer, idx):
    return [pl.BlockSpec((None, LRU_CONV_K, CW), lambda *g: (layer, 0, idx(*g))),
            pl.BlockSpec((None, 1, CW), lambda *g: (layer, 0, idx(*g))),
            pl.BlockSpec((None, LRU_BLOCKS_PER_TILE, LRU_BLOCK_DIM, LRU_BLOCK_DIM),
                         lambda *g: (layer, idx(*g), 0, 0)),
            pl.BlockSpec((None, 1, CW), lambda *g: (layer, 0, idx(*g))),
            pl.BlockSpec((None, LRU_BLOCKS_PER_TILE, LRU_BLOCK_DIM, LRU_BLOCK_DIM),
                         lambda *g: (layer, idx(*g), 0, 0)),
            pl.BlockSpec((None, 1, CW), lambda *g: (layer, 0, idx(*g))),
            pl.BlockSpec((None, 1, CW), lambda *g: (layer, 0, idx(*g)))]


def _lru_prompt(y_init, z, lru_w, layer):
    nt = SEQ // TT_LRU
    return pl.pallas_call(
        _lru_p_kernel,
        grid=(BATCH, LRU_WIDTH // CW, nt),
        in_specs=[pl.BlockSpec(memory_space=pl.ANY),
                  pl.BlockSpec((TT_LRU, CW), lambda b, c, t: (b * nt + t, OFF_XL // CW + c))]
                 + _lru_specs(layer, lambda b, c, t: c),
        out_specs=[pl.BlockSpec((TT_LRU, CW), lambda b, c, t: (b * nt + t, c)),
                   pl.BlockSpec((None, SUBLANE, CW), lambda b, c, t: (b, 0, c))],
        out_shape=[jax.ShapeDtypeStruct((ROWS, LRU_WIDTH), bf16),
                   jax.ShapeDtypeStruct((BATCH, SUBLANE, LRU_WIDTH), f32)],
        input_output_aliases={0: 0},
        scratch_shapes=[pltpu.VMEM((TT_LRU + SUBLANE, CW), f32),
                        pltpu.VMEM((CW, CW), bf16), pltpu.VMEM((CW, CW), bf16),
                        pltpu.VMEM((TT_LRU, CW), f32), pltpu.VMEM((TT_LRU, CW), f32),
                        pltpu.VMEM((1, CW), f32)],
        compiler_params=_params(3, 40),
        name="lru_prompt",
    )(y_init, z, *lru_w)


def _seq_s_kernel(cb_ref, cc_ref, ch_ref, xl_ref, sc_ref, lc_ref, h0_ref, wsc_ref,
                  wc_ref, bc_ref, wa_ref, ba_ref, wx_ref, bx_ref, lam_ref,
                  yc_ref, yl_ref, scn_ref, hn_ref, wa_bd, wx_bd):
    nb = DEC_BATCH
    rows = lambda v, t: v[nb * t:nb * (t + 1), :]
    u = cc_ref[...] * ch_ref[...]
    cb = cb_ref[...]
    up = [sc_ref[j] for j in range(SCONV_K - 1)] + [rows(u, t) for t in range(DEC_SEQ)]
    w = wsc_ref[...]
    for t in range(DEC_SEQ):
        yc = up[t] * w[0:1]
        for j in range(1, SCONV_K):
            yc = yc + up[t + j] * w[j:j + 1]
        yc_ref[nb * t:nb * (t + 1), :] = (rows(cb, t) * yc).astype(bf16)
    for j in range(SCONV_K - 1):
        scn_ref[j] = up[DEC_SEQ + j]

    _build_block_diag(wa_ref, wa_bd)
    _build_block_diag(wx_ref, wx_bd)
    xl = xl_ref[...]
    xp = [lc_ref[j] for j in range(LRU_CONV_K - 1)] + [rows(xl, t) for t in range(DEC_SEQ)]
    wl = wc_ref[...]
    xcs = []
    for t in range(DEC_SEQ):
        xc = xp[t] * wl[0:1]
        for j in range(1, LRU_CONV_K):
            xc = xc + xp[t + j] * wl[j:j + 1]
        xcs.append(xc + bc_ref[...])
    xc = jnp.concatenate(xcs, axis=0)
    a, bx = _lru_gates(xc, wa_bd[...], wx_bd[...], ba_ref[...], bx_ref[...], lam_ref[...])
    h = h0_ref[...]
    for t in range(DEC_SEQ):
        h = rows(a, t) * h + rows(bx, t)
        yl_ref[nb * t:nb * (t + 1), :] = h.astype(bf16)
    hn_ref[...] = h


def _seq_sample(z, st_sconv_t, st_lconv_t, st_h, w_sconv, lru_w, layer):
    rblk = ROWS_P // ROWS_S
    zspec = lambda off: pl.BlockSpec((ROWS_S, CW), lambda c: (rblk, off // CW + c))
    return pl.pallas_call(
        _seq_s_kernel,
        grid=(CONV_WIDTH // CW,),
        in_specs=[zspec(OFF_CB), zspec(OFF_CC), zspec(OFF_CH), zspec(OFF_XL),
                  pl.BlockSpec((None, SCONV_K - 1, DEC_BATCH, CW), lambda c: (layer, 0, 0, c)),
                  pl.BlockSpec((None, LRU_CONV_K - 1, DEC_BATCH, CW), lambda c: (layer, 0, 0, c)),
                  pl.BlockSpec((None, DEC_BATCH, CW), lambda c: (layer, 0, c)),
                  pl.BlockSpec((None, SCONV_K, CW), lambda c: (layer, 0, c))]
                 + _lru_specs(layer, lambda c: c),
        out_specs=[pl.BlockSpec((ROWS_S, CW), lambda c: (0, c)),
                   pl.BlockSpec((ROWS_S, CW), lambda c: (0, c)),
                   pl.BlockSpec((SCONV_K - 1, DEC_BATCH, CW), lambda c: (0, 0, c)),
                   pl.BlockSpec((DEC_BATCH, CW), lambda c: (0, c))],
        out_shape=[jax.ShapeDtypeStruct((ROWS_S, CONV_WIDTH), bf16),
                   jax.ShapeDtypeStruct((ROWS_S, LRU_WIDTH), bf16),
                   jax.ShapeDtypeStruct((SCONV_K - 1, DEC_BATCH, CONV_WIDTH), f32),
                   jax.ShapeDtypeStruct((DEC_BATCH, LRU_WIDTH), f32)],
        scratch_shapes=[pltpu.VMEM((CW, CW), bf16), pltpu.VMEM((CW, CW), bf16)],
        compiler_params=_params(1, 32),
        name="seq_sample",
    )(z, z, z, z, st_sconv_t, st_lconv_t, st_h, w_sconv, *lru_w)


def _softmax_rows(s):
    m = jnp.max(s, axis=-1, keepdims=True)
    p = jnp.exp(s - m)
    return p * (1.0 / jnp.sum(p, axis=-1, keepdims=True))


def _mem_p_kernel(y_init, q_ref, k_ref, v_ref, y_ref):
    del y_init
    s = lax.dot_general(q_ref[...].astype(bf16), k_ref[...].astype(bf16), NT_DIMS,
                        preferred_element_type=f32) * (MEM_HEAD_DIM ** -0.5)
    pn = _softmax_rows(s).astype(bf16)
    y_ref[...] = jnp.dot(pn, v_ref[...].astype(bf16), preferred_element_type=f32).astype(bf16)


def _mem_prompt(y_init, z, mkv):
    nt = SEQ // TQ_MEM
    hw = MEM_HEAD_DIM
    return pl.pallas_call(
        _mem_p_kernel,
        grid=(BATCH, MEM_HEADS, nt),
        in_specs=[pl.BlockSpec(memory_space=pl.ANY),
                  pl.BlockSpec((TQ_MEM, hw), lambda b, h, t: (b * nt + t, OFF_QM // hw + h)),
                  pl.BlockSpec((MEM_LEN, hw), lambda b, h, t: (b, h)),
                  pl.BlockSpec((MEM_LEN, hw), lambda b, h, t: (b, MEM_HEADS + h))],
        out_specs=pl.BlockSpec((TQ_MEM, hw), lambda b, h, t: (b * nt + t, h)),
        out_shape=jax.ShapeDtypeStruct((ROWS, MEM_WIDTH), bf16),
        input_output_aliases={0: 0},
        compiler_params=_params(3, 32),
        name="mem_prompt",
    )(y_init, z, mkv, mkv)


SB_MEM = 4


QEXP = MEM_HEADS * DEC_SEQ


def _mem_s_kernel(q_ref, k_ref, v_ref, y_ref, qexp):
    hw = MEM_HEAD_DIM
    qexp[...] = jnp.zeros(qexp.shape, f32)
    for bb in range(SB_MEM):
        qb = q_ref[bb]
        for h in range(MEM_HEADS):
            qexp[DEC_SEQ * h:DEC_SEQ * (h + 1), hw * h:hw * (h + 1)] = qb[:, hw * h:hw * (h + 1)]
        s = lax.dot_general(qexp[...].astype(bf16), k_ref[bb].astype(bf16), NT_DIMS,
                            preferred_element_type=f32) * (MEM_HEAD_DIM ** -0.5)
        pn = _softmax_rows(s).astype(bf16)
        o = jnp.dot(pn, v_ref[bb].astype(bf16), preferred_element_type=f32)
        for h in range(MEM_HEADS):
            y_ref[bb, :, hw * h:hw * (h + 1)] = o[DEC_SEQ * h:DEC_SEQ * (h + 1), hw * h:hw * (h + 1)]


def _mem_sample(qm3, cache_k, cache_v, layer):
    cache_spec = pl.BlockSpec((None, SB_MEM, MEM_LEN, MEM_WIDTH), lambda i: (layer, i, 0, 0))
    return pl.pallas_call(
        _mem_s_kernel,
        grid=(DEC_BATCH // SB_MEM,),
        in_specs=[pl.BlockSpec((SB_MEM, DEC_SEQ, MEM_WIDTH), lambda i: (i, 0, 0)),
                  cache_spec, cache_spec],
        out_specs=pl.BlockSpec((SB_MEM, DEC_SEQ, MEM_WIDTH), lambda i: (i, 0, 0)),
        out_shape=jax.ShapeDtypeStruct((DEC_BATCH, DEC_SEQ, MEM_WIDTH), f32),
        scratch_shapes=[pltpu.VMEM((QEXP, MEM_WIDTH), f32)],
        compiler_params=_params(1, 40),
        name="mem_sample",
    )(qm3, cache_k, cache_v)


def _merge_kernel(y0, y1, y2, y3, g0, g1, g2, g3, wb_ref, o_ref):
    acc = None
    for b, (y, g) in enumerate(((y0, g0), (y1, g1), (y2, g2), (y3, g3))):
        proj = jnp.dot(y[...], wb_ref[b], preferred_element_type=f32)
        term = jax.nn.sigmoid(g[...].astype(f32)) * proj
        acc = term if acc is None else acc + term
    o_ref[...] = acc.astype(bf16)


def _merge(ys, zg, w_branch, layer):
    tn = TN_MERGE
    yspec = pl.BlockSpec((TM, BRANCH_WIDTH), lambda i, j: (i, 0))
    gspec = lambda b: pl.BlockSpec((TM, tn), lambda i, j: (i, (b * D_MODEL) // tn + j))
    return pl.pallas_call(
        _merge_kernel,
        grid=(zg.shape[0] // TM, D_MODEL // tn),
        in_specs=[yspec] * N_BRANCH + [gspec(b) for b in range(N_BRANCH)]
                 + [pl.BlockSpec((None, N_BRANCH, BRANCH_WIDTH, tn), lambda i, j: (layer, 0, 0, j))],
        out_specs=pl.BlockSpec((TM, tn), lambda i, j: (i, j)),
        out_shape=jax.ShapeDtypeStruct((zg.shape[0], D_MODEL), bf16),
        compiler_params=_params(2, 48),
        name="merge",
    )(*ys, zg, zg, zg, zg, w_branch)


def _oproj_kernel(m_ref, w_ref, x_ref, g_ref, b_ref, o_ref):
    d = jnp.dot(m_ref[...], w_ref[...], preferred_element_type=f32)
    o_ref[...] = _layer_norm(ALPHA * x_ref[...] + d, g_ref[0:1, :], b_ref[0:1, :])


def _oproj(merged, w_o, x, ln_g, ln_b, layer):
    tm = TM_OPROJ
    return pl.pallas_call(
        _oproj_kernel,
        grid=(x.shape[0] // tm,),
        in_specs=[pl.BlockSpec((tm, D_MODEL), lambda i: (i, 0)),
                  pl.BlockSpec((None, D_MODEL, D_MODEL), lambda i: (layer, 0, 0)),
                  pl.BlockSpec((tm, D_MODEL), lambda i: (i, 0)),
                  pl.BlockSpec((None, 2, D_MODEL), lambda i: (layer, 0, 0)),
                  pl.BlockSpec((None, 2, D_MODEL), lambda i: (layer, 0, 0))],
        out_specs=pl.BlockSpec((tm, D_MODEL), lambda i: (i, 0)),
        out_shape=jax.ShapeDtypeStruct(x.shape, f32),
        compiler_params=_params(1, 48),
        name="oproj_ln",
    )(merged, w_o, x, ln_g, ln_b)


def _swiglu_partial(xb, wg_ref, wu_ref, wd_ref):
    g = jnp.dot(xb, wg_ref[...], preferred_element_type=f32)
    u = jnp.dot(xb, wu_ref[...], preferred_element_type=f32)
    h = (jax.nn.silu(g) * u).astype(bf16)
    return jnp.dot(h, wd_ref[...], preferred_element_type=f32)


def _ffn_kernel(x_ref, wg_ref, wu_ref, wd_ref, g_ref, b_ref, cast_in, o_ref, cast_out, xb_ref):
    f = pl.program_id(1)
    cast_out[...] = cast_in[...].astype(bf16)

    @pl.when(f == 0)
    def _():
        xb_ref[...] = x_ref[...].astype(bf16)
        o_ref[...] = jnp.zeros(o_ref.shape, f32)

    o_ref[...] += _swiglu_partial(xb_ref[...], wg_ref, wu_ref, wd_ref)

    @pl.when(f == pl.num_programs(1) - 1)
    def _():
        o_ref[...] = _layer_norm(ALPHA * x_ref[...] + o_ref[...], g_ref[1:2, :], b_ref[1:2, :])


def _ffn(x, wg, wu, wd, ln_g, ln_b, layer, j, cast_w, cast_j):
    tm = TM_FFN
    n_f = D_FF // TF
    w_in = pl.BlockSpec((None, D_MODEL, TF), lambda i, f: (j, 0, f))
    w_out = pl.BlockSpec((None, TF, D_MODEL), lambda i, f: (j, f, 0))
    xspec = pl.BlockSpec((tm, D_MODEL), lambda i, f: (i, 0))
    x_once = pl.BlockSpec((tm, D_MODEL), lambda i, f: (i, 0), pipeline_mode=pl.Buffered(1))
    lnspec = pl.BlockSpec((None, 2, D_MODEL), lambda i, f: (layer, 0, 0))
    n_steps = (x.shape[0] // tm) * n_f
    c_in, c_out, c_shape = _cast_rider(cast_w, cast_j, n_steps, lambda i, f: i * n_f + f)
    return pl.pallas_call(
        _ffn_kernel,
        grid=(x.shape[0] // tm, n_f),
        in_specs=[x_once, w_in, w_in, w_out, lnspec, lnspec, c_in],
        out_specs=[xspec, c_out],
        out_shape=[jax.ShapeDtypeStruct(x.shape, f32), c_shape],
        scratch_shapes=[pltpu.VMEM((tm, D_MODEL), bf16)],
        compiler_params=_params(2, 54),
        name="ffn_dense",
    )(x, wg, wu, wd, ln_g, ln_b, cast_w)


TG = 640


def _max_row_tiles(n_tokens):
    return (n_tokens * TOP_K + N_EXPERTS * (TG - 1)) // TG


def _router_kernel(x_ref, w_ref, p_ref, i_ref, cnt_ref, carry):
    @pl.when(pl.program_id(0) == 0)
    def _():
        carry[...] = jnp.zeros(carry.shape, f32)

    tm = x_ref.shape[0]
    logits = jnp.dot(x_ref[...], w_ref[...], precision=lax.Precision.HIGHEST,
                     preferred_element_type=f32)
    lane = lax.broadcasted_iota(jnp.int32, logits.shape, 1)
    logits = jnp.where(lane < N_EXPERTS, logits, -jnp.inf)
    m1 = jnp.max(logits, axis=-1, keepdims=True)
    i1 = jnp.min(jnp.where(logits == m1, lane, LANE), axis=-1, keepdims=True)
    rest = jnp.where(lane == i1, -jnp.inf, logits)
    m2 = jnp.max(rest, axis=-1, keepdims=True)
    i2 = jnp.min(jnp.where(rest == m2, lane, LANE), axis=-1, keepdims=True)
    e = jnp.exp(m2 - m1)
    p1 = 1.0 / (1.0 + e)
    p2 = e / (1.0 + e)
    sel = jnp.where(lane == i1, 1.0, jnp.where(lane == i2, 1.0, 0.0))
    r_i = lax.broadcasted_iota(jnp.int32, (tm, tm), 0)
    c_i = lax.broadcasted_iota(jnp.int32, (tm, tm), 1)
    tri = jnp.where(r_i > c_i, 1.0, 0.0).astype(bf16)
    before = jnp.dot(tri, sel.astype(bf16), preferred_element_type=f32) + carry[...]
    r1 = jnp.sum(jnp.where(lane == i1, before, 0.0), axis=-1, keepdims=True).astype(jnp.int32)
    r2 = jnp.sum(jnp.where(lane == i2, before, 0.0), axis=-1, keepdims=True).astype(jnp.int32)
    carry[...] = carry[...] + jnp.sum(sel, axis=0, keepdims=True)
    cnt_ref[...] = carry[...]
    p_ref[...] = jnp.where(lane == 0, p1, jnp.where(lane == 1, p2, 0.0))
    i_ref[...] = jnp.where(lane == 0, i1, jnp.where(lane == 1, i2,
                           jnp.where(lane == 2, r1, jnp.where(lane == 3, r2, 0))))


def _router(x, w_router_pad, j):
    n = x.shape[0]
    tm = TM_FFN
    return pl.pallas_call(
        _router_kernel,
        grid=(n // tm,),
        in_specs=[pl.BlockSpec((tm, D_MODEL), lambda i: (i, 0)),
                  pl.BlockSpec((None, D_MODEL, LANE), lambda i: (j, 0, 0))],
        out_specs=[pl.BlockSpec((tm, LANE), lambda i: (i, 0)),
                   pl.BlockSpec((tm, LANE), lambda i: (i, 0)),
                   pl.BlockSpec((1, LANE), lambda i: (0, 0))],
        out_shape=[jax.ShapeDtypeStruct((n, LANE), f32),
                   jax.ShapeDtypeStruct((n, LANE), jnp.int32),
                   jax.ShapeDtypeStruct((1, LANE), f32)],
        scratch_shapes=[pltpu.VMEM((1, LANE), f32)],
        compiler_params=_params(1, 40),
        name="router",
    )(x, w_router_pad)


def _routing_plan(iinfo, counts):
    cnt = counts[0, :N_EXPERTS].astype(jnp.int32)
    tiles = (cnt + (TG - 1)) // TG
    end_tile = jnp.cumsum(tiles)
    offset = (end_tile - tiles) * TG
    pos1 = offset[iinfo[:, 0]] + iinfo[:, 2]
    pos2 = offset[iinfo[:, 1]] + iinfo[:, 3]
    tile_ids = jnp.arange(_max_row_tiles(iinfo.shape[0]), dtype=jnp.int32)
    tile_expert = jnp.minimum(jnp.sum(tile_ids[:, None] >= end_tile[None, :], axis=1),
                              N_EXPERTS - 1).astype(jnp.int32)
    return pos1, pos2, tile_expert, end_tile[N_EXPERTS - 1:].astype(jnp.int32)


DMA_UNROLL = 8


def _row_copy(src, s, dst, d, sem):
    return pltpu.make_async_copy(src.at[pl.ds(s, 1)], dst.at[pl.ds(d, 1)], sem)


def _dispatch_kernel(pos1_ref, pos2_ref, x_ref, xs_in, xs_ref, sem):
    del xs_in
    tm = x_ref.shape[0]
    base = pl.program_id(0) * tm

    def issue(r, c):
        _row_copy(x_ref, r, xs_ref, pos1_ref[base + r], sem).start()
        _row_copy(x_ref, r, xs_ref, pos2_ref[base + r], sem).start()
        return c

    def drain(r, c):
        _row_copy(x_ref, r, xs_ref, pos1_ref[base + r], sem).wait()
        _row_copy(x_ref, r, xs_ref, pos2_ref[base + r], sem).wait()
        return c

    lax.fori_loop(0, tm, issue, 0, unroll=DMA_UNROLL)
    lax.fori_loop(0, tm, drain, 0, unroll=DMA_UNROLL)


def _dispatch(x, pos1, pos2):
    tm = TM_FFN
    n_rows = _max_row_tiles(x.shape[0]) * TG
    xs0 = jnp.zeros((n_rows, D_MODEL), f32)
    return pl.pallas_call(
        _dispatch_kernel,
        grid_spec=pltpu.PrefetchScalarGridSpec(
            num_scalar_prefetch=2,
            grid=(x.shape[0] // tm,),
            in_specs=[pl.BlockSpec((tm, D_MODEL), lambda i, p1, p2: (i, 0)),
                      pl.BlockSpec(memory_space=pl.ANY)],
            out_specs=pl.BlockSpec(memory_space=pl.ANY),
            scratch_shapes=[pltpu.SemaphoreType.DMA(())]),
        out_shape=jax.ShapeDtypeStruct((n_rows, D_MODEL), f32),
        input_output_aliases={3: 0},
        compiler_params=_params(1, 32),
        name="moe_dispatch",
    )(pos1, pos2, x, xs0)


def _gffn_kernel(te_ref, nt_ref, x_ref, wg_ref, wu_ref, wd_ref, o_ref, xb_ref):
    del te_ref
    f = pl.program_id(1)
    used = pl.program_id(0) < nt_ref[0]

    @pl.when(jnp.logical_and(jnp.logical_not(used), f == 0))
    def _():
        o_ref[...] = jnp.zeros(o_ref.shape, f32)

    @pl.when(jnp.logical_and(used, f == 0))
    def _():
        xb_ref[...] = x_ref[...].astype(bf16)
        o_ref[...] = jnp.zeros(o_ref.shape, f32)

    @pl.when(used)
    def _():
        o_ref[...] += _swiglu_partial(xb_ref[...], wg_ref, wu_ref, wd_ref)


def _grouped_ffn(xs, tile_expert, n_tiles, wg, wu, wd):
    n_f = D_FF // TF
    row = lambda i, nt: jnp.minimum(i, nt[0] - 1)
    fcol = lambda i, f, nt: jnp.where(i < nt[0], f, n_f - 1)
    xspec = pl.BlockSpec((TG, D_MODEL), lambda i, f, te, nt: (row(i, nt), 0))
    w_in = pl.BlockSpec((None, D_MODEL, TF),
                        lambda i, f, te, nt: (te[row(i, nt)], 0, fcol(i, f, nt)))
    w_out = pl.BlockSpec((None, TF, D_MODEL),
                         lambda i, f, te, nt: (te[row(i, nt)], fcol(i, f, nt), 0))
    return pl.pallas_call(
        _gffn_kernel,
        grid_spec=pltpu.PrefetchScalarGridSpec(
            num_scalar_prefetch=2,
            grid=(xs.shape[0] // TG, n_f),
            in_specs=[xspec, w_in, w_in, w_out],
            out_specs=pl.BlockSpec((TG, D_MODEL), lambda i, f, te, nt: (i, 0)),
            scratch_shapes=[pltpu.VMEM((TG, D_MODEL), bf16)]),
        out_shape=jax.ShapeDtypeStruct(xs.shape, f32),
        compiler_params=_params(2, 52),
        name="ffn_experts",
    )(tile_expert, n_tiles, xs, wg, wu, wd)


def _combine_kernel(pos1_ref, pos2_ref, x_ref, p_ref, y_ref, g_ref, b_ref, o_ref, buf1, buf2, sem):
    tm = x_ref.shape[0]
    base = pl.program_id(0) * tm

    def issue(r, c):
        _row_copy(y_ref, pos1_ref[base + r], buf1, r, sem).start()
        _row_copy(y_ref, pos2_ref[base + r], buf2, r, sem).start()
        return c

    def drain(r, c):
        _row_copy(y_ref, pos1_ref[base + r], buf1, r, sem).wait()
        _row_copy(y_ref, pos2_ref[base + r], buf2, r, sem).wait()
        return c

    lax.fori_loop(0, tm, issue, 0, unroll=DMA_UNROLL)
    lax.fori_loop(0, tm, drain, 0, unroll=DMA_UNROLL)
    p = p_ref[...]
    mixed = p[:, 0:1] * buf1[...] + p[:, 1:2] * buf2[...]
    o_ref[...] = _layer_norm(ALPHA * x_ref[...] + mixed, g_ref[1:2, :], b_ref[1:2, :])


def _combine(x, pinfo, y, pos1, pos2, ln_g, ln_b, layer):
    tm = TM_FFN
    xspec = pl.BlockSpec((tm, D_MODEL), lambda i, p1, p2: (i, 0))
    lnspec = pl.BlockSpec((None, 2, D_MODEL), lambda i, p1, p2: (layer, 0, 0))
    return pl.pallas_call(
        _combine_kernel,
        grid_spec=pltpu.PrefetchScalarGridSpec(
            num_scalar_prefetch=2,
            grid=(x.shape[0] // tm,),
            in_specs=[xspec, pl.BlockSpec((tm, LANE), lambda i, p1, p2: (i, 0)),
                      pl.BlockSpec(memory_space=pl.ANY), lnspec, lnspec],
            out_specs=xspec,
            scratch_shapes=[pltpu.VMEM((tm, D_MODEL), f32), pltpu.VMEM((tm, D_MODEL), f32),
                            pltpu.SemaphoreType.DMA(())]),
        out_shape=jax.ShapeDtypeStruct(x.shape, f32),
        compiler_params=_params(1, 48),
        name="moe_combine",
    )(pos1, pos2, x, pinfo, y, ln_g, ln_b)


def _moe(x, w_router_pad, wg, wu, wd, ln_g, ln_b, layer, j):
    pinfo, iinfo, counts = _router(x, w_router_pad, j)
    pos1, pos2, tile_expert, n_tiles = _routing_plan(iinfo, counts)
    xs = _dispatch(x, pos1, pos2)
    y = _grouped_ffn(xs, tile_expert, n_tiles, wg, wu, wd)
    return _combine(x, pinfo, y, pos1, pos2, ln_g, ln_b, layer)


def kernel(x_prompt, x_sample, mem_prompt, cache_win_k, cache_win_v, cache_mem_k, cache_mem_v,
           state_sconv, state_lru_conv, state_lru_h, w_mix_in, sinks, w_sconv, w_lru_conv, b_lru_conv,
           w_lru_a, b_lru_a, w_lru_x, b_lru_x, lru_lambda, w_mem_kv, w_branch, w_o, ln_g, ln_b,
           w_router, w_ffn_gate, w_ffn_up, w_ffn_down, w_exp_gate, w_exp_up, w_exp_down):
    x = jnp.concatenate([x_prompt.reshape(ROWS_P, D_MODEL),
                         x_sample.transpose(1, 0, 2).reshape(ROWS_S, D_MODEL)], axis=0)
    mem = mem_prompt.reshape(BATCH * MEM_LEN, D_MODEL)
    w_in_b = w_mix_in.astype(bf16)
    w_gate_b = w_in_b[:, :, OFF_G:]
    w_mkv_b = w_mem_kv.astype(bf16)
    w_br_b = w_branch.astype(bf16)
    w_o_b = w_o.astype(bf16)
    wfg, wfu, wfd = w_ffn_gate.astype(bf16), w_ffn_up.astype(bf16), w_ffn_down.astype(bf16)
    n_moe = w_exp_gate.shape[0]
    weg_rows = w_exp_gate.reshape(n_moe, N_EXPERTS * D_MODEL, D_FF)
    wed_rows = w_exp_down.reshape(n_moe, N_EXPERTS * D_FF, D_MODEL)
    w_router_pad = jnp.pad(w_router, ((0, 0), (0, 0), (0, LANE - N_EXPERTS)))
    ck = cache_win_k.reshape(DEPTH, DEC_BATCH, WINDOW, KV_WIDTH)
    cv = cache_win_v.reshape(DEPTH, DEC_BATCH, WINDOW, KV_WIDTH)
    cmk = cache_mem_k.reshape(DEPTH, DEC_BATCH, MEM_LEN, MEM_WIDTH)
    cmv = cache_mem_v.reshape(DEPTH, DEC_BATCH, MEM_LEN, MEM_WIDTH)
    st_sc_t = state_sconv.transpose(0, 2, 1, 3)
    st_lc_t = state_lru_conv.transpose(0, 2, 1, 3)
    lru_w = (w_lru_conv, b_lru_conv.reshape(DEPTH, 1, LRU_WIDTH), w_lru_a,
             b_lru_a.reshape(DEPTH, 1, LRU_WIDTH), w_lru_x, b_lru_x.reshape(DEPTH, 1, LRU_WIDTH),
             lru_lambda.reshape(DEPTH, 1, LRU_WIDTH))
    tab_p = _rope_tables(jnp.arange(SEQ, dtype=jnp.int32))
    tab_s = _rope_tables(PAST_LEN + jnp.arange(DEC_SEQ, dtype=jnp.int32))

    def to_b_major(v):
        return v.reshape(DEC_SEQ, DEC_BATCH, v.shape[-1]).transpose(1, 0, 2)

    def to_t_major(v):
        return v.transpose(1, 0, 2).reshape(ROWS_S, v.shape[-1])

    outs = [[] for _ in range(12)]
    for l in range(DEPTH):
        z = _matmul(x, w_in_b, l, TM, TN_MIX, n=OFF_G)
        mkv = _matmul(mem, w_mkv_b, l, BATCH * MEM_LEN, TN_MKV)
        zs = z[ROWS_P:]

        def all_rows(y_s):
            return lax.dynamic_update_slice(jnp.zeros((ROWS, y_s.shape[-1]), bf16), y_s, (ROWS_P, 0))

        ya_s, nk_s, nv_s = _attn_sample(to_b_major(zs[:, :OFF_CB]), ck, cv, sinks, tab_s, l)
        yc_s, yl_s, sc_s, h_s = _seq_sample(z, st_sc_t, st_lc_t, state_lru_h, w_sconv, lru_w, l)
        ym_s = _mem_sample(to_b_major(zs[:, OFF_QM:OFF_G]), cmk, cmv, l)
        j = l // 2
        hosts_casts = l % 2 == 0 and j < n_moe
        ga = _gates_attn_prompt(all_rows(to_t_major(ya_s).astype(bf16)), x, w_gate_b, z, sinks, tab_p, l,
                                cast_w=wed_rows if hosts_casts else None, cast_j=j)
        zg, ya, krot_p = ga[:3]
        yc, sc_p = _sconv_prompt(all_rows(yc_s), z, w_sconv, l)
        yl, h_p = _lru_prompt(all_rows(yl_s), z, lru_w, l)
        ym = _mem_prompt(all_rows(to_t_major(ym_s).astype(bf16)), z, mkv)
        merged = _merge((ya, yc, yl, ym), zg, w_br_b, l)
        x = _oproj(merged, w_o_b, x, ln_g, ln_b, l)
        if l % 2 == 0:
            x, weg_j = _ffn(x, wfg, wfu, wfd, ln_g, ln_b, l, j, weg_rows, j)
            wed_j = ga[3]
        else:
            x = _moe(x, w_router_pad, weg_j.reshape(N_EXPERTS, D_MODEL, D_FF), w_exp_up[j].astype(bf16),
                     wed_j.reshape(N_EXPERTS, D_FF, D_MODEL), ln_g, ln_b, l, j)

        def tail(nrows, lo, hi):
            return jnp.stack([z[(b + 1) * SEQ - nrows:(b + 1) * SEQ, lo:hi] for b in range(BATCH)])

        kv_shape = (BATCH, WINDOW, KV_HEADS, HEAD_DIM)
        outs[0].append(krot_p.reshape(BATCH, SEQ, KV_WIDTH)[:, SEQ - WINDOW:].reshape(kv_shape))
        outs[1].append(tail(WINDOW, OFF_V, OFF_CB).reshape(kv_shape))
        mem_shape = (BATCH, MEM_LEN, MEM_HEADS, MEM_HEAD_DIM)
        outs[2].append(mkv[:, :MEM_WIDTH].reshape(mem_shape))
        outs[3].append(mkv[:, MEM_WIDTH:].reshape(mem_shape))
        outs[4].append(sc_p[:, SUBLANE - (SCONV_K - 1):])
        outs[5].append(tail(LRU_CONV_K - 1, OFF_XL, OFF_QM))
        outs[6].append(h_p[:, SUBLANE - 1])
        kvs_shape = (DEC_BATCH, WINDOW, KV_HEADS, HEAD_DIM)
        outs[7].append(nk_s.reshape(kvs_shape))
        outs[8].append(nv_s.reshape(kvs_shape))
        outs[9].append(sc_s.transpose(1, 0, 2))
        outs[10].append(zs[DEC_BATCH:, OFF_XL:OFF_QM].reshape(LRU_CONV_K - 1, DEC_BATCH, LRU_WIDTH
                                                              ).transpose(1, 0, 2))
        outs[11].append(h_s)

    y_prompt = x[:ROWS_P].reshape(BATCH, SEQ, D_MODEL)
    y_sample = to_b_major(x[ROWS_P:])
    return (y_prompt, y_sample) + tuple(jnp.stack(o) for o in outs)
```

```python
import functools

import jax
import jax.numpy as jnp
from jax import lax
from jax.experimental import pallas as pl
from jax.experimental.pallas import tpu as pltpu

f32 = jnp.float32
bf16 = jnp.bfloat16

D_MODEL = 2048
BATCH = 2
SEQ = 4096
DEPTH = 4
DEC_BATCH = 32
DEC_SEQ = 4
PAST_LEN = 16384
N_HEADS = 16
KV_HEADS = 4
GQA_GROUP = N_HEADS // KV_HEADS
HEAD_DIM = 64
ROPE_DIM = HEAD_DIM // 4
ROPE_THETA = 500000.0
WINDOW = 128
ATTN_WIDTH = N_HEADS * HEAD_DIM
KV_WIDTH = KV_HEADS * HEAD_DIM
CONV_WIDTH = D_MODEL // 2
SCONV_K = 3
LRU_WIDTH = D_MODEL // 2
LRU_BLOCKS = 16
LRU_BLOCK_DIM = LRU_WIDTH // LRU_BLOCKS
LRU_CONV_K = 4
LRU_C = 8.0
MEM_LEN = 256
MEM_HEADS = 4
MEM_HEAD_DIM = 256
MEM_WIDTH = MEM_HEADS * MEM_HEAD_DIM
N_BRANCH = 4
BRANCH_WIDTH = D_MODEL // 2
D_FF = 5632
N_EXPERTS = 8
TOP_K = 2
ALPHA = (2 * DEPTH) ** 0.25
LN_EPS = 1e-5
NEG_INF = -1e30

ROWS_P = BATCH * SEQ
ROWS_S = DEC_BATCH * DEC_SEQ
ROWS = ROWS_P + ROWS_S

OFF_Q = 0
OFF_K = OFF_Q + ATTN_WIDTH
OFF_V = OFF_K + KV_WIDTH
OFF_CB = OFF_V + KV_WIDTH
OFF_CC = OFF_CB + CONV_WIDTH
OFF_CH = OFF_CC + CONV_WIDTH
OFF_XL = OFF_CH + CONV_WIDTH
OFF_QM = OFF_XL + LRU_WIDTH
OFF_G = OFF_QM + MEM_WIDTH
MIX_IN = OFF_G + N_BRANCH * D_MODEL

LANE = 128
SUBLANE = 8
BF16_SUBLANE = 16
MIB = 1024 * 1024
TM = 832
TN_MIX = 1664
TN_MKV = 512
TN_MERGE = 512
TM_OPROJ = 416
TM_FFN = 640
TF = 512
CW = 512
TT_CONV = 1024
TT_LRU = 512
TQ_MEM = 2048

NT_DIMS = (((1,), (1,)), ((), ()))


def _params(n_axes, vmem_mib):
    return pltpu.CompilerParams(dimension_semantics=("arbitrary",) * n_axes,
                                vmem_limit_bytes=vmem_mib * MIB)


def _layer_norm(v, g, b):
    mu = jnp.mean(v, axis=-1, keepdims=True)
    vc = v - mu
    var = jnp.mean(vc * vc, axis=-1, keepdims=True)
    return vc * lax.rsqrt(var + LN_EPS) * g + b


def _expm1(x):
    fact = [1.0]
    for k in range(1, 10):
        fact.append(fact[-1] * k)
    p = 1.0 / fact[9]
    for k in range(8, 0, -1):
        p = p * x + 1.0 / fact[k]
    return jnp.where(jnp.abs(x) < 0.5, x * p, jnp.exp(x) - 1.0)


def _cast_rider(w, j, n_steps, step_of):
    rows, cols = w.shape[1:]
    n_blocks = max(n for n in range(1, n_steps + 1)
                   if rows % n == 0 and (rows // n) % BF16_SUBLANE == 0)
    br = rows // n_blocks
    blk = lambda *g: jnp.minimum(step_of(*g), n_blocks - 1)
    return (pl.BlockSpec((None, br, cols), lambda *g: (j, blk(*g), 0)),
            pl.BlockSpec((br, cols), lambda *g: (blk(*g), 0)),
            jax.ShapeDtypeStruct((rows, cols), bf16))


def _mm_kernel(x_ref, w_ref, o_ref, xb_ref):
    @pl.when(pl.program_id(1) == 0)
    def _():
        xb_ref[...] = x_ref[...].astype(bf16)

    o_ref[...] = jnp.dot(xb_ref[...], w_ref[...], preferred_element_type=f32)


def _matmul(x, w_stack, layer, tm, tn, n=None):
    m, k = x.shape
    n = w_stack.shape[-1] if n is None else n
    vmem = 2 * (tm * k * 4 + k * tn * 2 + tm * tn * 4) + tm * k * 2
    return pl.pallas_call(
        _mm_kernel,
        grid=(m // tm, n // tn),
        in_specs=[pl.BlockSpec((tm, k), lambda i, j: (i, 0)),
                  pl.BlockSpec((None, k, tn), lambda i, j: (layer, 0, j))],
        out_specs=pl.BlockSpec((tm, tn), lambda i, j: (i, j)),
        out_shape=jax.ShapeDtypeStruct((m, n), f32),
        scratch_shapes=[pltpu.VMEM((tm, k), bf16)],
        compiler_params=_params(2, vmem // MIB + 8),
        name="matmul",
    )(x, w_stack)


def _rope_tables(pos):
    half = ROPE_DIM // 2
    inv_freq = ROPE_THETA ** (-jnp.arange(half, dtype=f32) * (2.0 / ROPE_DIM))
    ang = pos.astype(f32)[:, None] * inv_freq[None, :]
    cos = jnp.cos(ang)
    sin = jnp.sin(ang)
    p = pos.shape[0]
    rest = HEAD_DIM - ROPE_DIM
    c = jnp.concatenate([cos, cos, jnp.ones((p, rest), f32)], axis=1)
    sa = jnp.concatenate([-sin, jnp.zeros((p, HEAD_DIM - half), f32)], axis=1)
    sb = jnp.concatenate([jnp.zeros((p, half), f32), sin, jnp.zeros((p, rest), f32)], axis=1)
    rep = LANE // HEAD_DIM
    return jnp.tile(c, (1, rep)), jnp.tile(sa, (1, rep)), jnp.tile(sb, (1, rep))


def _rope(x, c, sa, sb):
    half = ROPE_DIM // 2
    chunks = []
    for j in range(x.shape[1] // LANE):
        xc = x[:, LANE * j:LANE * (j + 1)]
        chunks.append(xc * c + pltpu.roll(xc, LANE - half, 1) * sa + pltpu.roll(xc, half, 1) * sb)
    return chunks[0] if len(chunks) == 1 else jnp.concatenate(chunks, axis=1)


def _sink_softmax(s, sink):
    m = jnp.maximum(jnp.max(s, axis=-1, keepdims=True), sink)
    p = jnp.exp(s - m)
    denom = jnp.sum(p, axis=-1, keepdims=True) + jnp.exp(sink - m)
    return p * (1.0 / denom)


def _attn_block(n, sinks_ref, q_ref, k_ref, v_ref, c_ref, sa_ref, sb_ref, y_ref, kr_ref,
                kprev, vprev, layer, before_head=None):
    @pl.when(n == 0)
    def _():
        kprev[...] = jnp.zeros(kprev.shape, f32)
        vprev[...] = jnp.zeros(vprev.shape, f32)

    c, sa, sb = c_ref[...], sa_ref[...], sb_ref[...]
    q = _rope(q_ref[...], c, sa, sb)
    kc = _rope(k_ref[...], c, sa, sb)
    vc = v_ref[...]
    kr_ref[...] = kc
    kall = jnp.concatenate([kprev[...], kc], axis=0).astype(bf16)
    vall = jnp.concatenate([vprev[...], vc], axis=0).astype(bf16)
    kprev[...] = kc
    vprev[...] = vc

    rows = GQA_GROUP * WINDOW
    qi = lax.broadcasted_iota(jnp.int32, (rows, 2 * WINDOW), 0) % WINDOW
    kj = lax.broadcasted_iota(jnp.int32, (rows, 2 * WINDOW), 1)
    kmin = jnp.where(n > 0, 0, WINDOW)
    mask = (kj > qi) & (kj <= qi + WINDOW) & (kj >= kmin)
    rg = lax.broadcasted_iota(jnp.int32, (rows, 1), 0) // WINDOW
    for h in range(KV_HEADS):
        if before_head is not None:
            before_head(h)
        qh = jnp.concatenate(
            [q[:, (GQA_GROUP * h + g) * HEAD_DIM:(GQA_GROUP * h + g + 1) * HEAD_DIM]
             for g in range(GQA_GROUP)], axis=0).astype(bf16)
        kh = kall[:, HEAD_DIM * h:HEAD_DIM * (h + 1)]
        vh = vall[:, HEAD_DIM * h:HEAD_DIM * (h + 1)]
        s = lax.dot_general(qh, kh, NT_DIMS, preferred_element_type=f32) * (HEAD_DIM ** -0.5)
        s = jnp.where(mask, s, NEG_INF)
        sink = jnp.zeros((rows, 1), f32)
        for g in range(GQA_GROUP):
            sink = jnp.where(rg == g, sinks_ref[layer, GQA_GROUP * h + g], sink)
        pn = _sink_softmax(s, sink).astype(bf16)
        o = jnp.dot(pn, vh, preferred_element_type=f32)
        for g in range(GQA_GROUP):
            hd = GQA_GROUP * h + g
            y_ref[:, hd * HEAD_DIM:(hd + 1) * HEAD_DIM] = o[WINDOW * g:WINDOW * (g + 1), :].astype(bf16)


GA_ROWS = 8
GA_COLS = 8
TM_GA = ROWS // GA_ROWS
TN_GA = N_BRANCH * D_MODEL // GA_COLS


def _gates_attn_kernel(y_init, sinks_ref, x_ref, w_ref, q_ref, k_ref, v_ref, c_ref, sa_ref, sb_ref,
                       cast_in, zg_ref, y_ref, kr_ref, cast_out, xb_ref, kprev, vprev, *, layer):
    del y_init
    cast_out[...] = cast_in[...].astype(bf16)
    j = pl.program_id(1)
    step = pl.program_id(0) * GA_COLS + j

    @pl.when(j == 0)
    def _():
        xb_ref[...] = x_ref[...].astype(bf16)

    cw = TN_GA // KV_HEADS

    def gate_columns(h):
        zg_ref[:, cw * h:cw * (h + 1)] = jnp.dot(
            xb_ref[...], w_ref[:, cw * h:cw * (h + 1)], preferred_element_type=f32).astype(zg_ref.dtype)

    _attn_block(step % (SEQ // WINDOW), sinks_ref, q_ref, k_ref, v_ref, c_ref, sa_ref, sb_ref,
                y_ref, kr_ref, kprev, vprev, layer, before_head=gate_columns)


def _gates_attn_prompt(y_init, x, w_gate_b, z, sinks, tables, layer, cast_w, cast_j):
    nblk = SEQ // WINDOW
    assert GA_ROWS * GA_COLS == BATCH * nblk
    step = lambda i, j: i * GA_COLS + j
    tab = pl.BlockSpec((WINDOW, LANE), lambda i, j: (step(i, j) % nblk, 0))
    c_in, c_out, c_shape = _cast_rider(cast_w, cast_j, GA_ROWS * GA_COLS, step)
    in_specs = [pl.BlockSpec(memory_space=pl.ANY),
                pl.BlockSpec(memory_space=pltpu.SMEM),
                pl.BlockSpec((TM_GA, D_MODEL), lambda i, j: (i, 0), pipeline_mode=pl.Buffered(1)),
                pl.BlockSpec((None, D_MODEL, TN_GA), lambda i, j: (layer, 0, j)),
                pl.BlockSpec((WINDOW, ATTN_WIDTH), lambda i, j: (step(i, j), OFF_Q // ATTN_WIDTH)),
                pl.BlockSpec((WINDOW, KV_WIDTH), lambda i, j: (step(i, j), OFF_K // KV_WIDTH)),
                pl.BlockSpec((WINDOW, KV_WIDTH), lambda i, j: (step(i, j), OFF_V // KV_WIDTH)),
                tab, tab, tab, c_in]
    out_specs = [pl.BlockSpec((TM_GA, TN_GA), lambda i, j: (i, j)),
                 pl.BlockSpec((WINDOW, ATTN_WIDTH), lambda i, j: (step(i, j), 0)),
                 pl.BlockSpec((WINDOW, KV_WIDTH), lambda i, j: (step(i, j), 0)), c_out]
    out_shape = [jax.ShapeDtypeStruct((ROWS, N_BRANCH * D_MODEL), bf16),
                 jax.ShapeDtypeStruct((ROWS, ATTN_WIDTH), bf16),
                 jax.ShapeDtypeStruct((ROWS_P, KV_WIDTH), f32), c_shape]
    args = [y_init, sinks, x, w_gate_b, z, z, z, *tables, cast_w]
    return pl.pallas_call(
        functools.partial(_gates_attn_kernel, layer=layer),
        grid=(GA_ROWS, GA_COLS),
        in_specs=in_specs,
        out_specs=out_specs,
        out_shape=out_shape,
        scratch_shapes=[pltpu.VMEM((TM_GA, D_MODEL), bf16),
                        pltpu.VMEM((WINDOW, KV_WIDTH), f32), pltpu.VMEM((WINDOW, KV_WIDTH), f32)],
        input_output_aliases={0: 1},
        compiler_params=_params(2, 52),
        name="gates_attn_prompt",
    )(*args)


SB_ATTN = 8
KALL = WINDOW + SUBLANE


def _attn_s_kernel(sinks_ref, z_ref, ck_ref, cv_ref, c_ref, sa_ref, sb_ref, y_ref, nk_ref, nv_ref,
                   kall, vall, qs, *, layer):
    c, sa, sb = c_ref[...], sa_ref[...], sb_ref[...]
    rows = GQA_GROUP * DEC_SEQ
    kall[WINDOW:KALL, :] = jnp.zeros((SUBLANE, KV_WIDTH), f32)
    vall[WINDOW:KALL, :] = jnp.zeros((SUBLANE, KV_WIDTH), f32)
    qt = lax.broadcasted_iota(jnp.int32, (rows, KALL), 0) % DEC_SEQ
    kj = lax.broadcasted_iota(jnp.int32, (rows, KALL), 1)
    mask = jnp.where(kj < WINDOW, kj - qt, qt - (kj - WINDOW) + 1) > 0
    rg = lax.broadcasted_iota(jnp.int32, (rows, 1), 0) // DEC_SEQ
    for bb in range(SB_ATTN):
        zb = z_ref[bb]
        q = _rope(zb[:, OFF_Q:OFF_Q + ATTN_WIDTH], c, sa, sb)
        kn = _rope(zb[:, OFF_K:OFF_K + KV_WIDTH], c, sa, sb)
        vn = zb[:, OFF_V:OFF_V + KV_WIDTH]
        kall[0:WINDOW, :] = ck_ref[bb]
        vall[0:WINDOW, :] = cv_ref[bb]
        kall[WINDOW:WINDOW + DEC_SEQ, :] = kn
        vall[WINDOW:WINDOW + DEC_SEQ, :] = vn
        nk_ref[bb] = kall[DEC_SEQ:DEC_SEQ + WINDOW, :]
        nv_ref[bb] = vall[DEC_SEQ:DEC_SEQ + WINDOW, :]
        for h in range(KV_HEADS):
            for g in range(GQA_GROUP):
                hd = GQA_GROUP * h + g
                qs[DEC_SEQ * g:DEC_SEQ * (g + 1), :] = q[:, hd * HEAD_DIM:(hd + 1) * HEAD_DIM]
            qh = qs[...].astype(bf16)
            kh = kall[:, HEAD_DIM * h:HEAD_DIM * (h + 1)].astype(bf16)
            vh = vall[:, HEAD_DIM * h:HEAD_DIM * (h + 1)].astype(bf16)
            s = lax.dot_general(qh, kh, NT_DIMS, preferred_element_type=f32) * (HEAD_DIM ** -0.5)
            s = jnp.where(mask, s, NEG_INF)
            sink = jnp.zeros((rows, 1), f32)
            for g in range(GQA_GROUP):
                sink = jnp.where(rg == g, sinks_ref[layer, GQA_GROUP * h + g], sink)
            pn = _sink_softmax(s, sink).astype(bf16)
            o = jnp.dot(pn, vh, preferred_element_type=f32)
            for g in range(GQA_GROUP):
                hd = GQA_GROUP * h + g
                y_ref[bb, :, hd * HEAD_DIM:(hd + 1) * HEAD_DIM] = o[DEC_SEQ * g:DEC_SEQ * (g + 1), :]


def _attn_sample(zs3, cache_k, cache_v, sinks, tables, layer):
    qkv = ATTN_WIDTH + 2 * KV_WIDTH
    return pl.pallas_call(
        functools.partial(_attn_s_kernel, layer=layer),
        grid=(DEC_BATCH // SB_ATTN,),
        in_specs=[pl.BlockSpec(memory_space=pltpu.SMEM),
                  pl.BlockSpec((SB_ATTN, DEC_SEQ, qkv), lambda i: (i, 0, 0)),
                  pl.BlockSpec((None, SB_ATTN, WINDOW, KV_WIDTH), lambda i: (layer, i, 0, 0)),
                  pl.BlockSpec((None, SB_ATTN, WINDOW, KV_WIDTH), lambda i: (layer, i, 0, 0)),
                  pl.BlockSpec((DEC_SEQ, LANE), lambda i: (0, 0)),
                  pl.BlockSpec((DEC_SEQ, LANE), lambda i: (0, 0)),
                  pl.BlockSpec((DEC_SEQ, LANE), lambda i: (0, 0))],
        out_specs=[pl.BlockSpec((SB_ATTN, DEC_SEQ, ATTN_WIDTH), lambda i: (i, 0, 0)),
                   pl.BlockSpec((SB_ATTN, WINDOW, KV_WIDTH), lambda i: (i, 0, 0)),
                   pl.BlockSpec((SB_ATTN, WINDOW, KV_WIDTH), lambda i: (i, 0, 0))],
        out_shape=[jax.ShapeDtypeStruct((DEC_BATCH, DEC_SEQ, ATTN_WIDTH), f32),
                   jax.ShapeDtypeStruct((DEC_BATCH, WINDOW, KV_WIDTH), f32),
                   jax.ShapeDtypeStruct((DEC_BATCH, WINDOW, KV_WIDTH), f32)],
        scratch_shapes=[pltpu.VMEM((KALL, KV_WIDTH), f32), pltpu.VMEM((KALL, KV_WIDTH), f32),
                        pltpu.VMEM((GQA_GROUP * DEC_SEQ, HEAD_DIM), f32)],
        compiler_params=_params(1, 32),
        name="attn_sample",
    )(sinks, zs3, cache_k, cache_v, *tables)


def _sconv_p_kernel(y_init, cb_ref, cc_ref, ch_ref, w_ref, y_ref, st_ref, ubuf):
    del y_init
    t = pl.program_id(2)
    tt = cc_ref.shape[0]

    @pl.when(t == 0)
    def _():
        ubuf[0:SUBLANE, :] = jnp.zeros((SUBLANE, CW), f32)

    ubuf[SUBLANE:SUBLANE + tt, :] = cc_ref[...] * ch_ref[...]
    w = w_ref[...]
    yc = ubuf[SUBLANE - 2:SUBLANE - 2 + tt, :] * w[0:1]
    yc = yc + ubuf[SUBLANE - 1:SUBLANE - 1 + tt, :] * w[1:2]
    yc = yc + ubuf[SUBLANE:SUBLANE + tt, :] * w[2:3]
    y_ref[...] = (cb_ref[...] * yc).astype(bf16)
    last = ubuf[tt:tt + SUBLANE, :]
    st_ref[...] = last
    ubuf[0:SUBLANE, :] = last


def _sconv_prompt(y_init, z, w_sconv, layer):
    nt = SEQ // TT_CONV
    zspec = lambda off: pl.BlockSpec((TT_CONV, CW), lambda b, c, t: (b * nt + t, off // CW + c))
    return pl.pallas_call(
        _sconv_p_kernel,
        grid=(BATCH, CONV_WIDTH // CW, nt),
        in_specs=[pl.BlockSpec(memory_space=pl.ANY), zspec(OFF_CB), zspec(OFF_CC), zspec(OFF_CH),
                  pl.BlockSpec((None, SCONV_K, CW), lambda b, c, t: (layer, 0, c))],
        out_specs=[pl.BlockSpec((TT_CONV, CW), lambda b, c, t: (b * nt + t, c)),
                   pl.BlockSpec((None, SUBLANE, CW), lambda b, c, t: (b, 0, c))],
        out_shape=[jax.ShapeDtypeStruct((ROWS, CONV_WIDTH), bf16),
                   jax.ShapeDtypeStruct((BATCH, SUBLANE, CONV_WIDTH), f32)],
        scratch_shapes=[pltpu.VMEM((TT_CONV + SUBLANE, CW), f32)],
        input_output_aliases={0: 0},
        compiler_params=_params(3, 40),
        name="sconv_prompt",
    )(y_init, z, z, z, w_sconv)


LRU_BLOCKS_PER_TILE = CW // LRU_BLOCK_DIM


def _build_block_diag(w_ref, bd_ref):
    bd_ref[...] = jnp.zeros(bd_ref.shape, bf16)
    for n in range(LRU_BLOCKS_PER_TILE):
        lo, hi = LRU_BLOCK_DIM * n, LRU_BLOCK_DIM * (n + 1)
        bd_ref[lo:hi, lo:hi] = w_ref[n].astype(bf16)


def _lru_gates(xc, wa_bd, wx_bd, ba, bx, lam):
    xcb = xc.astype(bf16)
    r = jax.nn.sigmoid(jnp.dot(xcb, wa_bd, preferred_element_type=f32) + ba)
    i = jax.nn.sigmoid(jnp.dot(xcb, wx_bd, preferred_element_type=f32) + bx)
    log_a = -LRU_C * r * jax.nn.softplus(-lam)
    a = jnp.exp(log_a)
    mult = jnp.sqrt(-_expm1(2.0 * log_a))
    return a, mult * (i * xc)


def _lru_p_kernel(y_init, xl_ref, wc_ref, bc_ref, wa_ref, ba_ref, wx_ref, bx_ref, lam_ref,
                  y_ref, h_ref, xbuf, wa_bd, wx_bd, a_s, b_s, hcar):
    del y_init
    t = pl.program_id(2)
    tt = xl_ref.shape[0]

    @pl.when(t == 0)
    def _():
        xbuf[0:SUBLANE, :] = jnp.zeros((SUBLANE, CW), f32)
        hcar[...] = jnp.zeros(hcar.shape, f32)
        _build_block_diag(wa_ref, wa_bd)
        _build_block_diag(wx_ref, wx_bd)

    xbuf[SUBLANE:SUBLANE + tt, :] = xl_ref[...]
    w = wc_ref[...]
    xc = xbuf[SUBLANE - 3:SUBLANE - 3 + tt, :] * w[0:1]
    for j in range(1, LRU_CONV_K):
        xc = xc + xbuf[SUBLANE - 3 + j:SUBLANE - 3 + j + tt, :] * w[j:j + 1]
    xc = xc + bc_ref[...]
    xbuf[0:SUBLANE, :] = xbuf[tt:tt + SUBLANE, :]

    a, bx = _lru_gates(xc, wa_bd[...], wx_bd[...], ba_ref[...], bx_ref[...], lam_ref[...])

    r8 = lax.broadcasted_iota(jnp.int32, (tt, CW), 0) % SUBLANE
    for s in (1, 2, 4):
        keep = r8 >= s
        a_sh = jnp.where(keep, pltpu.roll(a, s, 0), 1.0)
        b_sh = jnp.where(keep, pltpu.roll(bx, s, 0), 0.0)
        bx = bx + a * b_sh
        a = a * a_sh
    a_s[...] = a
    b_s[...] = bx

    def tile_step(j, h):
        r0 = pl.multiple_of(j * SUBLANE, SUBLANE)
        ht = b_s[pl.ds(r0, SUBLANE), :] + a_s[pl.ds(r0, SUBLANE), :] * h
        b_s[pl.ds(r0, SUBLANE), :] = ht
        return ht[SUBLANE - 1:SUBLANE, :]

    h_last = lax.fori_loop(0, tt // SUBLANE, tile_step, hcar[...])
    hcar[...] = h_last
    y_ref[...] = b_s[...].astype(bf16)
    h_ref[...] = b_s[tt - SUBLANE:tt, :]


def _lru_specs(layer, idx):
    return [pl.BlockSpec((None, LRU_CONV_K, CW), lambda *g: (layer, 0, idx(*g))),
            pl.BlockSpec((None, 1, CW), lambda *g: (layer, 0, idx(*g))),
            pl.BlockSpec((None, LRU_BLOCKS_PER_TILE, LRU_BLOCK_DIM, LRU_BLOCK_DIM),
                         lambda *g: (layer, idx(*g), 0, 0)),
            pl.BlockSpec((None, 1, CW), lambda *g: (layer, 0, idx(*g))),
            pl.BlockSpec((None, LRU_BLOCKS_PER_TILE, LRU_BLOCK_DIM, LRU_BLOCK_DIM),
                         lambda *g: (layer, idx(*g), 0, 0)),
            pl.BlockSpec((None, 1, CW), lambda *g: (layer, 0, idx(*g))),
            pl.BlockSpec((None, 1, CW), lambda *g: (layer, 0, idx(*g)))]


def _lru_prompt(y_init, z, lru_w, layer):
    nt = SEQ // TT_LRU
    return pl.pallas_call(
        _lru_p_kernel,
        grid=(BATCH, LRU_WIDTH // CW, nt),
        in_specs=[pl.BlockSpec(memory_space=pl.ANY),
                  pl.BlockSpec((TT_LRU, CW), lambda b, c, t: (b * nt + t, OFF_XL // CW + c))]
                 + _lru_specs(layer, lambda b, c, t: c),
        out_specs=[pl.BlockSpec((TT_LRU, CW), lambda b, c, t: (b * nt + t, c)),
                   pl.BlockSpec((None, SUBLANE, CW), lambda b, c, t: (b, 0, c))],
        out_shape=[jax.ShapeDtypeStruct((ROWS, LRU_WIDTH), bf16),
                   jax.ShapeDtypeStruct((BATCH, SUBLANE, LRU_WIDTH), f32)],
        input_output_aliases={0: 0},
        scratch_shapes=[pltpu.VMEM((TT_LRU + SUBLANE, CW), f32),
                        pltpu.VMEM((CW, CW), bf16), pltpu.VMEM((CW, CW), bf16),
                        pltpu.VMEM((TT_LRU, CW), f32), pltpu.VMEM((TT_LRU, CW), f32),
                        pltpu.VMEM((1, CW), f32)],
        compiler_params=_params(3, 40),
        name="lru_prompt",
    )(y_init, z, *lru_w)


def _seq_s_kernel(cb_ref, cc_ref, ch_ref, xl_ref, sc_ref, lc_ref, h0_ref, wsc_ref,
                  wc_ref, bc_ref, wa_ref, ba_ref, wx_ref, bx_ref, lam_ref,
                  yc_ref, yl_ref, scn_ref, hn_ref, wa_bd, wx_bd):
    nb = DEC_BATCH
    rows = lambda v, t: v[nb * t:nb * (t + 1), :]
    u = cc_ref[...] * ch_ref[...]
    cb = cb_ref[...]
    up = [sc_ref[j] for j in range(SCONV_K - 1)] + [rows(u, t) for t in range(DEC_SEQ)]
    w = wsc_ref[...]
    for t in range(DEC_SEQ):
        yc = up[t] * w[0:1]
        for j in range(1, SCONV_K):
            yc = yc + up[t + j] * w[j:j + 1]
        yc_ref[nb * t:nb * (t + 1), :] = (rows(cb, t) * yc).astype(bf16)
    for j in range(SCONV_K - 1):
        scn_ref[j] = up[DEC_SEQ + j]

    _build_block_diag(wa_ref, wa_bd)
    _build_block_diag(wx_ref, wx_bd)
    xl = xl_ref[...]
    xp = [lc_ref[j] for j in range(LRU_CONV_K - 1)] + [rows(xl, t) for t in range(DEC_SEQ)]
    wl = wc_ref[...]
    xcs = []
    for t in range(DEC_SEQ):
        xc = xp[t] * wl[0:1]
        for j in range(1, LRU_CONV_K):
            xc = xc + xp[t + j] * wl[j:j + 1]
        xcs.append(xc + bc_ref[...])
    xc = jnp.concatenate(xcs, axis=0)
    a, bx = _lru_gates(xc, wa_bd[...], wx_bd[...], ba_ref[...], bx_ref[...], lam_ref[...])
    h = h0_ref[...]
    for t in range(DEC_SEQ):
        h = rows(a, t) * h + rows(bx, t)
        yl_ref[nb * t:nb * (t + 1), :] = h.astype(bf16)
    hn_ref[...] = h


def _seq_sample(z, st_sconv_t, st_lconv_t, st_h, w_sconv, lru_w, layer):
    rblk = ROWS_P // ROWS_S
    zspec = lambda off: pl.BlockSpec((ROWS_S, CW), lambda c: (rblk, off // CW + c))
    return pl.pallas_call(
        _seq_s_kernel,
        grid=(CONV_WIDTH // CW,),
        in_specs=[zspec(OFF_CB), zspec(OFF_CC), zspec(OFF_CH), zspec(OFF_XL),
                  pl.BlockSpec((None, SCONV_K - 1, DEC_BATCH, CW), lambda c: (layer, 0, 0, c)),
                  pl.BlockSpec((None, LRU_CONV_K - 1, DEC_BATCH, CW), lambda c: (layer, 0, 0, c)),
                  pl.BlockSpec((None, DEC_BATCH, CW), lambda c: (layer, 0, c)),
                  pl.BlockSpec((None, SCONV_K, CW), lambda c: (layer, 0, c))]
                 + _lru_specs(layer, lambda c: c),
        out_specs=[pl.BlockSpec((ROWS_S, CW), lambda c: (0, c)),
                   pl.BlockSpec((ROWS_S, CW), lambda c: (0, c)),
                   pl.BlockSpec((SCONV_K - 1, DEC_BATCH, CW), lambda c: (0, 0, c)),
                   pl.BlockSpec((DEC_BATCH, CW), lambda c: (0, c))],
        out_shape=[jax.ShapeDtypeStruct((ROWS_S, CONV_WIDTH), bf16),
                   jax.ShapeDtypeStruct((ROWS_S, LRU_WIDTH), bf16),
                   jax.ShapeDtypeStruct((SCONV_K - 1, DEC_BATCH, CONV_WIDTH), f32),
                   jax.ShapeDtypeStruct((DEC_BATCH, LRU_WIDTH), f32)],
        scratch_shapes=[pltpu.VMEM((CW, CW), bf16), pltpu.VMEM((CW, CW), bf16)],
        compiler_params=_params(1, 32),
        name="seq_sample",
    )(z, z, z, z, st_sconv_t, st_lconv_t, st_h, w_sconv, *lru_w)


def _softmax_rows(s):
    m = jnp.max(s, axis=-1, keepdims=True)
    p = jnp.exp(s - m)
    return p * (1.0 / jnp.sum(p, axis=-1, keepdims=True))


def _mem_p_kernel(y_init, q_ref, k_ref, v_ref, y_ref):
    del y_init
    s = lax.dot_general(q_ref[...].astype(bf16), k_ref[...].astype(bf16), NT_DIMS,
                        preferred_element_type=f32) * (MEM_HEAD_DIM ** -0.5)
    pn = _softmax_rows(s).astype(bf16)
    y_ref[...] = jnp.dot(pn, v_ref[...].astype(bf16), preferred_element_type=f32).astype(bf16)


def _mem_prompt(y_init, z, mkv):
    nt = SEQ // TQ_MEM
    hw = MEM_HEAD_DIM
    return pl.pallas_call(
        _mem_p_kernel,
        grid=(BATCH, MEM_HEADS, nt),
        in_specs=[pl.BlockSpec(memory_space=pl.ANY),
                  pl.BlockSpec((TQ_MEM, hw), lambda b, h, t: (b * nt + t, OFF_QM // hw + h)),
                  pl.BlockSpec((MEM_LEN, hw), lambda b, h, t: (b, h)),
                  pl.BlockSpec((MEM_LEN, hw), lambda b, h, t: (b, MEM_HEADS + h))],
        out_specs=pl.BlockSpec((TQ_MEM, hw), lambda b, h, t: (b * nt + t, h)),
        out_shape=jax.ShapeDtypeStruct((ROWS, MEM_WIDTH), bf16),
        input_output_aliases={0: 0},
        compiler_params=_params(3, 32),
        name="mem_prompt",
    )(y_init, z, mkv, mkv)


SB_MEM = 4


QEXP = MEM_HEADS * DEC_SEQ


def _mem_s_kernel(q_ref, k_ref, v_ref, y_ref, qexp):
    hw = MEM_HEAD_DIM
    qexp[...] = jnp.zeros(qexp.shape, f32)
    for bb in range(SB_MEM):
        qb = q_ref[bb]
        for h in range(MEM_HEADS):
            qexp[DEC_SEQ * h:DEC_SEQ * (h + 1), hw * h:hw * (h + 1)] = qb[:, hw * h:hw * (h + 1)]
        s = lax.dot_general(qexp[...].astype(bf16), k_ref[bb].astype(bf16), NT_DIMS,
                            preferred_element_type=f32) * (MEM_HEAD_DIM ** -0.5)
        pn = _softmax_rows(s).astype(bf16)
        o = jnp.dot(pn, v_ref[bb].astype(bf16), preferred_element_type=f32)
        for h in range(MEM_HEADS):
            y_ref[bb, :, hw * h:hw * (h + 1)] = o[DEC_SEQ * h:DEC_SEQ * (h + 1), hw * h:hw * (h + 1)]


def _mem_sample(qm3, cache_k, cache_v, layer):
    cache_spec = pl.BlockSpec((None, SB_MEM, MEM_LEN, MEM_WIDTH), lambda i: (layer, i, 0, 0))
    return pl.pallas_call(
        _mem_s_kernel,
        grid=(DEC_BATCH // SB_MEM,),
        in_specs=[pl.BlockSpec((SB_MEM, DEC_SEQ, MEM_WIDTH), lambda i: (i, 0, 0)),
                  cache_spec, cache_spec],
        out_specs=pl.BlockSpec((SB_MEM, DEC_SEQ, MEM_WIDTH), lambda i: (i, 0, 0)),
        out_shape=jax.ShapeDtypeStruct((DEC_BATCH, DEC_SEQ, MEM_WIDTH), f32),
        scratch_shapes=[pltpu.VMEM((QEXP, MEM_WIDTH), f32)],
        compiler_params=_params(1, 40),
        name="mem_sample",
    )(qm3, cache_k, cache_v)


def _merge_kernel(y0, y1, y2, y3, g0, g1, g2, g3, wb_ref, o_ref):
    acc = None
    for b, (y, g) in enumerate(((y0, g0), (y1, g1), (y2, g2), (y3, g3))):
        proj = jnp.dot(y[...], wb_ref[b], preferred_element_type=f32)
        term = jax.nn.sigmoid(g[...].astype(f32)) * proj
        acc = term if acc is None else acc + term
    o_ref[...] = acc.astype(bf16)


def _merge(ys, zg, w_branch, layer):
    tn = TN_MERGE
    yspec = pl.BlockSpec((TM, BRANCH_WIDTH), lambda i, j: (i, 0))
    gspec = lambda b: pl.BlockSpec((TM, tn), lambda i, j: (i, (b * D_MODEL) // tn + j))
    return pl.pallas_call(
        _merge_kernel,
        grid=(zg.shape[0] // TM, D_MODEL // tn),
        in_specs=[yspec] * N_BRANCH + [gspec(b) for b in range(N_BRANCH)]
                 + [pl.BlockSpec((None, N_BRANCH, BRANCH_WIDTH, tn), lambda i, j: (layer, 0, 0, j))],
        out_specs=pl.BlockSpec((TM, tn), lambda i, j: (i, j)),
        out_shape=jax.ShapeDtypeStruct((zg.shape[0], D_MODEL), bf16),
        compiler_params=_params(2, 48),
        name="merge",
    )(*ys, zg, zg, zg, zg, w_branch)


def _oproj_kernel(m_ref, w_ref, x_ref, g_ref, b_ref, o_ref):
    d = jnp.dot(m_ref[...], w_ref[...], preferred_element_type=f32)
    o_ref[...] = _layer_norm(ALPHA * x_ref[...] + d, g_ref[0:1, :], b_ref[0:1, :])


def _oproj(merged, w_o, x, ln_g, ln_b, layer):
    tm = TM_OPROJ
    return pl.pallas_call(
        _oproj_kernel,
        grid=(x.shape[0] // tm,),
        in_specs=[pl.BlockSpec((tm, D_MODEL), lambda i: (i, 0)),
                  pl.BlockSpec((None, D_MODEL, D_MODEL), lambda i: (layer, 0, 0)),
                  pl.BlockSpec((tm, D_MODEL), lambda i: (i, 0)),
                  pl.BlockSpec((None, 2, D_MODEL), lambda i: (layer, 0, 0)),
                  pl.BlockSpec((None, 2, D_MODEL), lambda i: (layer, 0, 0))],
        out_specs=pl.BlockSpec((tm, D_MODEL), lambda i: (i, 0)),
        out_shape=jax.ShapeDtypeStruct(x.shape, f32),
        compiler_params=_params(1, 48),
        name="oproj_ln",
    )(merged, w_o, x, ln_g, ln_b)


def _swiglu_partial(xb, wg_ref, wu_ref, wd_ref):
    g = jnp.dot(xb, wg_ref[...], preferred_element_type=f32)
    u = jnp.dot(xb, wu_ref[...], preferred_element_type=f32)
    h = (jax.nn.silu(g) * u).astype(bf16)
    return jnp.dot(h, wd_ref[...], preferred_element_type=f32)


def _ffn_kernel(x_ref, wg_ref, wu_ref, wd_ref, g_ref, b_ref, cast_in, o_ref, cast_out, xb_ref):
    f = pl.program_id(1)
    cast_out[...] = cast_in[...].astype(bf16)

    @pl.when(f == 0)
    def _():
        xb_ref[...] = x_ref[...].astype(bf16)
        o_ref[...] = jnp.zeros(o_ref.shape, f32)

    o_ref[...] += _swiglu_partial(xb_ref[...], wg_ref, wu_ref, wd_ref)

    @pl.when(f == pl.num_programs(1) - 1)
    def _():
        o_ref[...] = _layer_norm(ALPHA * x_ref[...] + o_ref[...], g_ref[1:2, :], b_ref[1:2, :])


def _ffn(x, wg, wu, wd, ln_g, ln_b, layer, j, cast_w, cast_j):
    tm = TM_FFN
    n_f = D_FF // TF
    w_in = pl.BlockSpec((None, D_MODEL, TF), lambda i, f: (j, 0, f))
    w_out = pl.BlockSpec((None, TF, D_MODEL), lambda i, f: (j, f, 0))
    xspec = pl.BlockSpec((tm, D_MODEL), lambda i, f: (i, 0))
    x_once = pl.BlockSpec((tm, D_MODEL), lambda i, f: (i, 0), pipeline_mode=pl.Buffered(1))
    lnspec = pl.BlockSpec((None, 2, D_MODEL), lambda i, f: (layer, 0, 0))
    n_steps = (x.shape[0] // tm) * n_f
    c_in, c_out, c_shape = _cast_rider(cast_w, cast_j, n_steps, lambda i, f: i * n_f + f)
    return pl.pallas_call(
        _ffn_kernel,
        grid=(x.shape[0] // tm, n_f),
        in_specs=[x_once, w_in, w_in, w_out, lnspec, lnspec, c_in],
        out_specs=[xspec, c_out],
        out_shape=[jax.ShapeDtypeStruct(x.shape, f32), c_shape],
        scratch_shapes=[pltpu.VMEM((tm, D_MODEL), bf16)],
        compiler_params=_params(2, 54),
        name="ffn_dense",
    )(x, wg, wu, wd, ln_g, ln_b, cast_w)


TG = 640


def _max_row_tiles(n_tokens):
    return (n_tokens * TOP_K + N_EXPERTS * (TG - 1)) // TG


def _router_kernel(x_ref, w_ref, p_ref, i_ref, cnt_ref, carry):
    @pl.when(pl.program_id(0) == 0)
    def _():
        carry[...] = jnp.zeros(carry.shape, f32)

    tm = x_ref.shape[0]
    logits = jnp.dot(x_ref[...], w_ref[...], precision=lax.Precision.HIGHEST,
                     preferred_element_type=f32)
    lane = lax.broadcasted_iota(jnp.int32, logits.shape, 1)
    logits = jnp.where(lane < N_EXPERTS, logits, -jnp.inf)
    m1 = jnp.max(logits, axis=-1, keepdims=True)
    i1 = jnp.min(jnp.where(logits == m1, lane, LANE), axis=-1, keepdims=True)
    rest = jnp.where(lane == i1, -jnp.inf, logits)
    m2 = jnp.max(rest, axis=-1, keepdims=True)
    i2 = jnp.min(jnp.where(rest == m2, lane, LANE), axis=-1, keepdims=True)
    e = jnp.exp(m2 - m1)
    p1 = 1.0 / (1.0 + e)
    p2 = e / (1.0 + e)
    sel = jnp.where(lane == i1, 1.0, jnp.where(lane == i2, 1.0, 0.0))
    r_i = lax.broadcasted_iota(jnp.int32, (tm, tm), 0)
    c_i = lax.broadcasted_iota(jnp.int32, (tm, tm), 1)
    tri = jnp.where(r_i > c_i, 1.0, 0.0).astype(bf16)
    before = jnp.dot(tri, sel.astype(bf16), preferred_element_type=f32) + carry[...]
    r1 = jnp.sum(jnp.where(lane == i1, before, 0.0), axis=-1, keepdims=True).astype(jnp.int32)
    r2 = jnp.sum(jnp.where(lane == i2, before, 0.0), axis=-1, keepdims=True).astype(jnp.int32)
    carry[...] = carry[...] + jnp.sum(sel, axis=0, keepdims=True)
    cnt_ref[...] = carry[...]
    p_ref[...] = jnp.where(lane == 0, p1, jnp.where(lane == 1, p2, 0.0))
    i_ref[...] = jnp.where(lane == 0, i1, jnp.where(lane == 1, i2,
                           jnp.where(lane == 2, r1, jnp.where(lane == 3, r2, 0))))


def _router(x, w_router_pad, j):
    n = x.shape[0]
    tm = TM_FFN
    return pl.pallas_call(
        _router_kernel,
        grid=(n // tm,),
        in_specs=[pl.BlockSpec((tm, D_MODEL), lambda i: (i, 0)),
                  pl.BlockSpec((None, D_MODEL, LANE), lambda i: (j, 0, 0))],
        out_specs=[pl.BlockSpec((tm, LANE), lambda i: (i, 0)),
                   pl.BlockSpec((tm, LANE), lambda i: (i, 0)),
                   pl.BlockSpec((1, LANE), lambda i: (0, 0))],
        out_shape=[jax.ShapeDtypeStruct((n, LANE), f32),
                   jax.ShapeDtypeStruct((n, LANE), jnp.int32),
                   jax.ShapeDtypeStruct((1, LANE), f32)],
        scratch_shapes=[pltpu.VMEM((1, LANE), f32)],
        compiler_params=_params(1, 40),
        name="router",
    )(x, w_router_pad)


def _routing_plan(iinfo, counts):
    cnt = counts[0, :N_EXPERTS].astype(jnp.int32)
    tiles = (cnt + (TG - 1)) // TG
    end_tile = jnp.cumsum(tiles)
    offset = (end_tile - tiles) * TG
    pos1 = offset[iinfo[:, 0]] + iinfo[:, 2]
    pos2 = offset[iinfo[:, 1]] + iinfo[:, 3]
    tile_ids = jnp.arange(_max_row_tiles(iinfo.shape[0]), dtype=jnp.int32)
    tile_expert = jnp.minimum(jnp.sum(tile_ids[:, None] >= end_tile[None, :], axis=1),
                              N_EXPERTS - 1).astype(jnp.int32)
    return pos1, pos2, tile_expert, end_tile[N_EXPERTS - 1:].astype(jnp.int32)


DMA_UNROLL = 8


def _row_copy(src, s, dst, d, sem):
    return pltpu.make_async_copy(src.at[pl.ds(s, 1)], dst.at[pl.ds(d, 1)], sem)


def _dispatch_kernel(pos1_ref, pos2_ref, x_ref, xs_in, xs_ref, sem):
    del xs_in
    tm = x_ref.shape[0]
    base = pl.program_id(0) * tm

    def issue(r, c):
        _row_copy(x_ref, r, xs_ref, pos1_ref[base + r], sem).start()
        _row_copy(x_ref, r, xs_ref, pos2_ref[base + r], sem).start()
        return c

    def drain(r, c):
        _row_copy(x_ref, r, xs_ref, pos1_ref[base + r], sem).wait()
        _row_copy(x_ref, r, xs_ref, pos2_ref[base + r], sem).wait()
        return c

    lax.fori_loop(0, tm, issue, 0, unroll=DMA_UNROLL)
    lax.fori_loop(0, tm, drain, 0, unroll=DMA_UNROLL)


def _dispatch(x, pos1, pos2):
    tm = TM_FFN
    n_rows = _max_row_tiles(x.shape[0]) * TG
    xs0 = jnp.zeros((n_rows, D_MODEL), f32)
    return pl.pallas_call(
        _dispatch_kernel,
        grid_spec=pltpu.PrefetchScalarGridSpec(
            num_scalar_prefetch=2,
            grid=(x.shape[0] // tm,),
            in_specs=[pl.BlockSpec((tm, D_MODEL), lambda i, p1, p2: (i, 0)),
                      pl.BlockSpec(memory_space=pl.ANY)],
            out_specs=pl.BlockSpec(memory_space=pl.ANY),
            scratch_shapes=[pltpu.SemaphoreType.DMA(())]),
        out_shape=jax.ShapeDtypeStruct((n_rows, D_MODEL), f32),
        input_output_aliases={3: 0},
        compiler_params=_params(1, 32),
        name="moe_dispatch",
    )(pos1, pos2, x, xs0)


def _gffn_kernel(te_ref, nt_ref, x_ref, wg_ref, wu_ref, wd_ref, o_ref, xb_ref):
    del te_ref
    f = pl.program_id(1)
    used = pl.program_id(0) < nt_ref[0]

    @pl.when(jnp.logical_and(jnp.logical_not(used), f == 0))
    def _():
        o_ref[...] = jnp.zeros(o_ref.shape, f32)

    @pl.when(jnp.logical_and(used, f == 0))
    def _():
        xb_ref[...] = x_ref[...].astype(bf16)
        o_ref[...] = jnp.zeros(o_ref.shape, f32)

    @pl.when(used)
    def _():
        o_ref[...] += _swiglu_partial(xb_ref[...], wg_ref, wu_ref, wd_ref)


def _grouped_ffn(xs, tile_expert, n_tiles, wg, wu, wd):
    n_f = D_FF // TF
    row = lambda i, nt: jnp.minimum(i, nt[0] - 1)
    fcol = lambda i, f, nt: jnp.where(i < nt[0], f, n_f - 1)
    xspec = pl.BlockSpec((TG, D_MODEL), lambda i, f, te, nt: (row(i, nt), 0))
    w_in = pl.BlockSpec((None, D_MODEL, TF),
                        lambda i, f, te, nt: (te[row(i, nt)], 0, fcol(i, f, nt)))
    w_out = pl.BlockSpec((None, TF, D_MODEL),
                         lambda i, f, te, nt: (te[row(i, nt)], fcol(i, f, nt), 0))
    return pl.pallas_call(
        _gffn_kernel,
        grid_spec=pltpu.PrefetchScalarGridSpec(
            num_scalar_prefetch=2,
            grid=(xs.shape[0] // TG, n_f),
            in_specs=[xspec, w_in, w_in, w_out],
            out_specs=pl.BlockSpec((TG, D_MODEL), lambda i, f, te, nt: (i, 0)),
            scratch_shapes=[pltpu.VMEM((TG, D_MODEL), bf16)]),
        out_shape=jax.ShapeDtypeStruct(xs.shape, f32),
        compiler_params=_params(2, 52),
        name="ffn_experts",
    )(tile_expert, n_tiles, xs, wg, wu, wd)


def _combine_kernel(pos1_ref, pos2_ref, x_ref, p_ref, y_ref, g_ref, b_ref, o_ref, buf1, buf2, sem):
    tm = x_ref.shape[0]
    base = pl.program_id(0) * tm

    def issue(r, c):
        _row_copy(y_ref, pos1_ref[base + r], buf1, r, sem).start()
        _row_copy(y_ref, pos2_ref[base + r], buf2, r, sem).start()
        return c

    def drain(r, c):
        _row_copy(y_ref, pos1_ref[base + r], buf1, r, sem).wait()
        _row_copy(y_ref, pos2_ref[base + r], buf2, r, sem).wait()
        return c

    lax.fori_loop(0, tm, issue, 0, unroll=DMA_UNROLL)
    lax.fori_loop(0, tm, drain, 0, unroll=DMA_UNROLL)
    p = p_ref[...]
    mixed = p[:, 0:1] * buf1[...] + p[:, 1:2] * buf2[...]
    o_ref[...] = _layer_norm(ALPHA * x_ref[...] + mixed, g_ref[1:2, :], b_ref[1:2, :])


def _combine(x, pinfo, y, pos1, pos2, ln_g, ln_b, layer):
    tm = TM_FFN
    xspec = pl.BlockSpec((tm, D_MODEL), lambda i, p1, p2: (i, 0))
    lnspec = pl.BlockSpec((None, 2, D_MODEL), lambda i, p1, p2: (layer, 0, 0))
    return pl.pallas_call(
        _combine_kernel,
        grid_spec=pltpu.PrefetchScalarGridSpec(
            num_scalar_prefetch=2,
            grid=(x.shape[0] // tm,),
            in_specs=[xspec, pl.BlockSpec((tm, LANE), lambda i, p1, p2: (i, 0)),
                      pl.BlockSpec(memory_space=pl.ANY), lnspec, lnspec],
            out_specs=xspec,
            scratch_shapes=[pltpu.VMEM((tm, D_MODEL), f32), pltpu.VMEM((tm, D_MODEL), f32),
                            pltpu.SemaphoreType.DMA(())]),
        out_shape=jax.ShapeDtypeStruct(x.shape, f32),
        compiler_params=_params(1, 48),
        name="moe_combine",
    )(pos1, pos2, x, pinfo, y, ln_g, ln_b)


def _moe(x, w_router_pad, wg, wu, wd, ln_g, ln_b, layer, j):
    pinfo, iinfo, counts = _router(x, w_router_pad, j)
    pos1, pos2, tile_expert, n_tiles = _routing_plan(iinfo, counts)
    xs = _dispatch(x, pos1, pos2)
    y = _grouped_ffn(xs, tile_expert, n_tiles, wg, wu, wd)
    return _combine(x, pinfo, y, pos1, pos2, ln_g, ln_b, layer)


def kernel(x_prompt, x_sample, mem_prompt, cache_win_k, cache_win_v, cache_mem_k, cache_mem_v,
           state_sconv, state_lru_conv, state_lru_h, w_mix_in, sinks, w_sconv, w_lru_conv, b_lru_conv,
           w_lru_a, b_lru_a, w_lru_x, b_lru_x, lru_lambda, w_mem_kv, w_branch, w_o, ln_g, ln_b,
           w_router, w_ffn_gate, w_ffn_up, w_ffn_down, w_exp_gate, w_exp_up, w_exp_down):
    x = jnp.concatenate([x_prompt.reshape(ROWS_P, D_MODEL),
                         x_sample.transpose(1, 0, 2).reshape(ROWS_S, D_MODEL)], axis=0)
    mem = mem_prompt.reshape(BATCH * MEM_LEN, D_MODEL)
    w_in_b = w_mix_in.astype(bf16)
    w_gate_b = w_in_b[:, :, OFF_G:]
    w_mkv_b = w_mem_kv.astype(bf16)
    w_br_b = w_branch.astype(bf16)
    w_o_b = w_o.astype(bf16)
    wfg, wfu, wfd = w_ffn_gate.astype(bf16), w_ffn_up.astype(bf16), w_ffn_down.astype(bf16)
    n_moe = w_exp_gate.shape[0]
    weg_rows = w_exp_gate.reshape(n_moe, N_EXPERTS * D_MODEL, D_FF)
    weu_rows = w_exp_up.reshape(n_moe, N_EXPERTS * D_MODEL, D_FF)
    wed_rows = w_exp_down.reshape(n_moe, N_EXPERTS * D_FF, D_MODEL)
    w_router_pad = jnp.pad(w_router, ((0, 0), (0, 0), (0, LANE - N_EXPERTS)))
    ck = cache_win_k.reshape(DEPTH, DEC_BATCH, WINDOW, KV_WIDTH)
    cv = cache_win_v.reshape(DEPTH, DEC_BATCH, WINDOW, KV_WIDTH)
    cmk = cache_mem_k.reshape(DEPTH, DEC_BATCH, MEM_LEN, MEM_WIDTH)
    cmv = cache_mem_v.reshape(DEPTH, DEC_BATCH, MEM_LEN, MEM_WIDTH)
    st_sc_t = state_sconv.transpose(0, 2, 1, 3)
    st_lc_t = state_lru_conv.transpose(0, 2, 1, 3)
    lru_w = (w_lru_conv, b_lru_conv.reshape(DEPTH, 1, LRU_WIDTH), w_lru_a,
             b_lru_a.reshape(DEPTH, 1, LRU_WIDTH), w_lru_x, b_lru_x.reshape(DEPTH, 1, LRU_WIDTH),
             lru_lambda.reshape(DEPTH, 1, LRU_WIDTH))
    tab_p = _rope_tables(jnp.arange(SEQ, dtype=jnp.int32))
    tab_s = _rope_tables(PAST_LEN + jnp.arange(DEC_SEQ, dtype=jnp.int32))

    def to_b_major(v):
        return v.reshape(DEC_SEQ, DEC_BATCH, v.shape[-1]).transpose(1, 0, 2)

    def to_t_major(v):
        return v.transpose(1, 0, 2).reshape(ROWS_S, v.shape[-1])

    outs = [[] for _ in range(12)]
    for l in range(DEPTH):
        z = _matmul(x, w_in_b, l, TM, TN_MIX, n=OFF_G)
        mkv = _matmul(mem, w_mkv_b, l, BATCH * MEM_LEN, TN_MKV)
        zs = z[ROWS_P:]

        def all_rows(y_s):
            return lax.dynamic_update_slice(jnp.zeros((ROWS, y_s.shape[-1]), bf16), y_s, (ROWS_P, 0))

        ya_s, nk_s, nv_s = _attn_sample(to_b_major(zs[:, :OFF_CB]), ck, cv, sinks, tab_s, l)
        yc_s, yl_s, sc_s, h_s = _seq_sample(z, st_sc_t, st_lc_t, state_lru_h, w_sconv, lru_w, l)
        ym_s = _mem_sample(to_b_major(zs[:, OFF_QM:OFF_G]), cmk, cmv, l)
        j = l // 2
        zg, ya, krot_p, w_cast = _gates_attn_prompt(
            all_rows(to_t_major(ya_s).astype(bf16)), x, w_gate_b, z, sinks, tab_p, l,
            cast_w=wed_rows if l % 2 == 0 else weu_rows, cast_j=j)
        yc, sc_p = _sconv_prompt(all_rows(yc_s), z, w_sconv, l)
        yl, h_p = _lru_prompt(all_rows(yl_s), z, lru_w, l)
        ym = _mem_prompt(all_rows(to_t_major(ym_s).astype(bf16)), z, mkv)
        merged = _merge((ya, yc, yl, ym), zg, w_br_b, l)
        x = _oproj(merged, w_o_b, x, ln_g, ln_b, l)
        if l % 2 == 0:
            x, weg_j = _ffn(x, wfg, wfu, wfd, ln_g, ln_b, l, j, weg_rows, j)
            wed_j = w_cast
        else:
            x = _moe(x, w_router_pad, weg_j.reshape(N_EXPERTS, D_MODEL, D_FF),
                     w_cast.reshape(N_EXPERTS, D_MODEL, D_FF),
                     wed_j.reshape(N_EXPERTS, D_FF, D_MODEL), ln_g, ln_b, l, j)

        def tail(nrows, lo, hi):
            return jnp.stack([z[(b + 1) * SEQ - nrows:(b + 1) * SEQ, lo:hi] for b in range(BATCH)])

        kv_shape = (BATCH, WINDOW, KV_HEADS, HEAD_DIM)
        outs[0].append(krot_p.reshape(BATCH, SEQ, KV_WIDTH)[:, SEQ - WINDOW:].reshape(kv_shape))
        outs[1].append(tail(WINDOW, OFF_V, OFF_CB).reshape(kv_shape))
        mem_shape = (BATCH, MEM_LEN, MEM_HEADS, MEM_HEAD_DIM)
        outs[2].append(mkv[:, :MEM_WIDTH].reshape(mem_shape))
        outs[3].append(mkv[:, MEM_WIDTH:].reshape(mem_shape))
        outs[4].append(sc_p[:, SUBLANE - (SCONV_K - 1):])
        outs[5].append(tail(LRU_CONV_K - 1, OFF_XL, OFF_QM))
        outs[6].append(h_p[:, SUBLANE - 1])
        kvs_shape = (DEC_BATCH, WINDOW, KV_HEADS, HEAD_DIM)
        outs[7].append(nk_s.reshape(kvs_shape))
        outs[8].append(nv_s.reshape(kvs_shape))
        outs[9].append(sc_s.transpose(1, 0, 2))
        outs[10].append(zs[DEC_BATCH:, OFF_XL:OFF_QM].reshape(LRU_CONV_K - 1, DEC_BATCH, LRU_WIDTH
                                                              ).transpose(1, 0, 2))
        outs[11].append(h_s)

    y_prompt = x[:ROWS_P].reshape(BATCH, SEQ, D_MODEL)
    y_sample = to_b_major(x[ROWS_P:])
    return (y_prompt, y_sample) + tuple(jnp.stack(o) for o in outs)
```

```python
import functools

import jax
import jax.numpy as jnp
from jax import lax
from jax.experimental import pallas as pl
from jax.experimental.pallas import tpu as pltpu

f32 = jnp.float32
bf16 = jnp.bfloat16

D_MODEL = 2048
BATCH = 2
SEQ = 4096
DEPTH = 4
DEC_BATCH = 32
DEC_SEQ = 4
PAST_LEN = 16384
N_HEADS = 16
KV_HEADS = 4
GQA_GROUP = N_HEADS // KV_HEADS
HEAD_DIM = 64
ROPE_DIM = HEAD_DIM // 4
ROPE_THETA = 500000.0
WINDOW = 128
ATTN_WIDTH = N_HEADS * HEAD_DIM
KV_WIDTH = KV_HEADS * HEAD_DIM
CONV_WIDTH = D_MODEL // 2
SCONV_K = 3
LRU_WIDTH = D_MODEL // 2
LRU_BLOCKS = 16
LRU_BLOCK_DIM = LRU_WIDTH // LRU_BLOCKS
LRU_CONV_K = 4
LRU_C = 8.0
MEM_LEN = 256
MEM_HEADS = 4
MEM_HEAD_DIM = 256
MEM_WIDTH = MEM_HEADS * MEM_HEAD_DIM
N_BRANCH = 4
BRANCH_WIDTH = D_MODEL // 2
D_FF = 5632
N_EXPERTS = 8
TOP_K = 2
ALPHA = (2 * DEPTH) ** 0.25
LN_EPS = 1e-5
NEG_INF = -1e30

ROWS_P = BATCH * SEQ
ROWS_S = DEC_BATCH * DEC_SEQ
ROWS = ROWS_P + ROWS_S

OFF_Q = 0
OFF_K = OFF_Q + ATTN_WIDTH
OFF_V = OFF_K + KV_WIDTH
OFF_CB = OFF_V + KV_WIDTH
OFF_CC = OFF_CB + CONV_WIDTH
OFF_CH = OFF_CC + CONV_WIDTH
OFF_XL = OFF_CH + CONV_WIDTH
OFF_QM = OFF_XL + LRU_WIDTH
OFF_G = OFF_QM + MEM_WIDTH
MIX_IN = OFF_G + N_BRANCH * D_MODEL

LANE = 128
SUBLANE = 8
BF16_SUBLANE = 16
MIB = 1024 * 1024
TM = 832
TN_MIX = 1664
TN_MKV = 512
TN_MERGE = 512
TM_OPROJ = 416
TM_FFN = 640
TF = 512
CW = 512
TT_CONV = 1024
TT_LRU = 512
TQ_MEM = 2048

NT_DIMS = (((1,), (1,)), ((), ()))


def _params(n_axes, vmem_mib):
    return pltpu.CompilerParams(dimension_semantics=("arbitrary",) * n_axes,
                                vmem_limit_bytes=vmem_mib * MIB)


def _layer_norm(v, g, b):
    mu = jnp.mean(v, axis=-1, keepdims=True)
    vc = v - mu
    var = jnp.mean(vc * vc, axis=-1, keepdims=True)
    return vc * lax.rsqrt(var + LN_EPS) * g + b


def _expm1(x):
    fact = [1.0]
    for k in range(1, 10):
        fact.append(fact[-1] * k)
    p = 1.0 / fact[9]
    for k in range(8, 0, -1):
        p = p * x + 1.0 / fact[k]
    return jnp.where(jnp.abs(x) < 0.5, x * p, jnp.exp(x) - 1.0)


def _cast_rider(w, j, n_steps, step_of):
    rows, cols = w.shape[1:]
    n_blocks = max(n for n in range(1, n_steps + 1)
                   if rows % n == 0 and (rows // n) % BF16_SUBLANE == 0)
    br = rows // n_blocks
    blk = lambda *g: jnp.minimum(step_of(*g), n_blocks - 1)
    return (pl.BlockSpec((None, br, cols), lambda *g: (j, blk(*g), 0)),
            pl.BlockSpec((br, cols), lambda *g: (blk(*g), 0)),
            jax.ShapeDtypeStruct((rows, cols), bf16))


def _mm_kernel(x_ref, w_ref, o_ref, xb_ref):
    @pl.when(pl.program_id(1) == 0)
    def _():
        xb_ref[...] = x_ref[...].astype(bf16)

    o_ref[...] = jnp.dot(xb_ref[...], w_ref[...], preferred_element_type=f32)


def _matmul(x, w_stack, layer, tm, tn, n=None):
    m, k = x.shape
    n = w_stack.shape[-1] if n is None else n
    vmem = 2 * (tm * k * 4 + k * tn * 2 + tm * tn * 4) + tm * k * 2
    return pl.pallas_call(
        _mm_kernel,
        grid=(m // tm, n // tn),
        in_specs=[pl.BlockSpec((tm, k), lambda i, j: (i, 0)),
                  pl.BlockSpec((None, k, tn), lambda i, j: (layer, 0, j))],
        out_specs=pl.BlockSpec((tm, tn), lambda i, j: (i, j)),
        out_shape=jax.ShapeDtypeStruct((m, n), f32),
        scratch_shapes=[pltpu.VMEM((tm, k), bf16)],
        compiler_params=_params(2, vmem // MIB + 8),
        name="matmul",
    )(x, w_stack)


def _rope_tables(pos):
    half = ROPE_DIM // 2
    inv_freq = ROPE_THETA ** (-jnp.arange(half, dtype=f32) * (2.0 / ROPE_DIM))
    ang = pos.astype(f32)[:, None] * inv_freq[None, :]
    cos = jnp.cos(ang)
    sin = jnp.sin(ang)
    p = pos.shape[0]
    rest = HEAD_DIM - ROPE_DIM
    c = jnp.concatenate([cos, cos, jnp.ones((p, rest), f32)], axis=1)
    sa = jnp.concatenate([-sin, jnp.zeros((p, HEAD_DIM - half), f32)], axis=1)
    sb = jnp.concatenate([jnp.zeros((p, half), f32), sin, jnp.zeros((p, rest), f32)], axis=1)
    rep = LANE // HEAD_DIM
    return jnp.tile(c, (1, rep)), jnp.tile(sa, (1, rep)), jnp.tile(sb, (1, rep))


def _rope(x, c, sa, sb):
    half = ROPE_DIM // 2
    chunks = []
    for j in range(x.shape[1] // LANE):
        xc = x[:, LANE * j:LANE * (j + 1)]
        chunks.append(xc * c + pltpu.roll(xc, LANE - half, 1) * sa + pltpu.roll(xc, half, 1) * sb)
    return chunks[0] if len(chunks) == 1 else jnp.concatenate(chunks, axis=1)


def _sink_softmax(s, sink):
    m = jnp.maximum(jnp.max(s, axis=-1, keepdims=True), sink)
    p = jnp.exp(s - m)
    denom = jnp.sum(p, axis=-1, keepdims=True) + jnp.exp(sink - m)
    return p * (1.0 / denom)


def _attn_block(n, sinks_ref, q_ref, k_ref, v_ref, c_ref, sa_ref, sb_ref, y_ref, kr_ref,
                kprev, vprev, layer, before_head=None):
    @pl.when(n == 0)
    def _():
        kprev[...] = jnp.zeros(kprev.shape, f32)
        vprev[...] = jnp.zeros(vprev.shape, f32)

    c, sa, sb = c_ref[...], sa_ref[...], sb_ref[...]
    q = _rope(q_ref[...], c, sa, sb)
    kc = _rope(k_ref[...], c, sa, sb)
    vc = v_ref[...]
    kr_ref[...] = kc
    kall = jnp.concatenate([kprev[...], kc], axis=0).astype(bf16)
    vall = jnp.concatenate([vprev[...], vc], axis=0).astype(bf16)
    kprev[...] = kc
    vprev[...] = vc

    rows = GQA_GROUP * WINDOW
    qi = lax.broadcasted_iota(jnp.int32, (rows, 2 * WINDOW), 0) % WINDOW
    kj = lax.broadcasted_iota(jnp.int32, (rows, 2 * WINDOW), 1)
    kmin = jnp.where(n > 0, 0, WINDOW)
    mask = (kj > qi) & (kj <= qi + WINDOW) & (kj >= kmin)
    rg = lax.broadcasted_iota(jnp.int32, (rows, 1), 0) // WINDOW
    for h in range(KV_HEADS):
        if before_head is not None:
            before_head(h)
        qh = jnp.concatenate(
            [q[:, (GQA_GROUP * h + g) * HEAD_DIM:(GQA_GROUP * h + g + 1) * HEAD_DIM]
             for g in range(GQA_GROUP)], axis=0).astype(bf16)
        kh = kall[:, HEAD_DIM * h:HEAD_DIM * (h + 1)]
        vh = vall[:, HEAD_DIM * h:HEAD_DIM * (h + 1)]
        s = lax.dot_general(qh, kh, NT_DIMS, preferred_element_type=f32) * (HEAD_DIM ** -0.5)
        s = jnp.where(mask, s, NEG_INF)
        sink = jnp.zeros((rows, 1), f32)
        for g in range(GQA_GROUP):
            sink = jnp.where(rg == g, sinks_ref[layer, GQA_GROUP * h + g], sink)
        pn = _sink_softmax(s, sink).astype(bf16)
        o = jnp.dot(pn, vh, preferred_element_type=f32)
        for g in range(GQA_GROUP):
            hd = GQA_GROUP * h + g
            y_ref[:, hd * HEAD_DIM:(hd + 1) * HEAD_DIM] = o[WINDOW * g:WINDOW * (g + 1), :].astype(bf16)


GA_ROWS = 8
GA_COLS = 8
TM_GA = ROWS // GA_ROWS
TN_GA = N_BRANCH * D_MODEL // GA_COLS


def _gates_attn_kernel(y_init, sinks_ref, x_ref, w_ref, q_ref, k_ref, v_ref, c_ref, sa_ref, sb_ref,
                       cast_in, zg_ref, y_ref, kr_ref, cast_out, xb_ref, kprev, vprev, *, layer):
    del y_init
    cast_out[...] = cast_in[...].astype(bf16)
    j = pl.program_id(1)
    step = pl.program_id(0) * GA_COLS + j

    @pl.when(j == 0)
    def _():
        xb_ref[...] = x_ref[...].astype(bf16)

    cw = TN_GA // KV_HEADS

    def gate_columns(h):
        zg_ref[:, cw * h:cw * (h + 1)] = jnp.dot(
            xb_ref[...], w_ref[:, cw * h:cw * (h + 1)], preferred_element_type=f32).astype(zg_ref.dtype)

    _attn_block(step % (SEQ // WINDOW), sinks_ref, q_ref, k_ref, v_ref, c_ref, sa_ref, sb_ref,
                y_ref, kr_ref, kprev, vprev, layer, before_head=gate_columns)


def _gates_attn_prompt(y_init, x, w_gate_b, z, sinks, tables, layer, cast_w, cast_j):
    nblk = SEQ // WINDOW
    assert GA_ROWS * GA_COLS == BATCH * nblk
    step = lambda i, j: i * GA_COLS + j
    tab = pl.BlockSpec((WINDOW, LANE), lambda i, j: (step(i, j) % nblk, 0))
    c_in, c_out, c_shape = _cast_rider(cast_w, cast_j, GA_ROWS * GA_COLS, step)
    in_specs = [pl.BlockSpec(memory_space=pl.ANY),
                pl.BlockSpec(memory_space=pltpu.SMEM),
                pl.BlockSpec((TM_GA, D_MODEL), lambda i, j: (i, 0), pipeline_mode=pl.Buffered(1)),
                pl.BlockSpec((None, D_MODEL, TN_GA), lambda i, j: (layer, 0, j)),
                pl.BlockSpec((WINDOW, ATTN_WIDTH), lambda i, j: (step(i, j), OFF_Q // ATTN_WIDTH)),
                pl.BlockSpec((WINDOW, KV_WIDTH), lambda i, j: (step(i, j), OFF_K // KV_WIDTH)),
                pl.BlockSpec((WINDOW, KV_WIDTH), lambda i, j: (step(i, j), OFF_V // KV_WIDTH)),
                tab, tab, tab, c_in]
    out_specs = [pl.BlockSpec((TM_GA, TN_GA), lambda i, j: (i, j)),
                 pl.BlockSpec((WINDOW, ATTN_WIDTH), lambda i, j: (step(i, j), 0)),
                 pl.BlockSpec((WINDOW, KV_WIDTH), lambda i, j: (step(i, j), 0)), c_out]
    out_shape = [jax.ShapeDtypeStruct((ROWS, N_BRANCH * D_MODEL), bf16),
                 jax.ShapeDtypeStruct((ROWS, ATTN_WIDTH), bf16),
                 jax.ShapeDtypeStruct((ROWS_P, KV_WIDTH), f32), c_shape]
    args = [y_init, sinks, x, w_gate_b, z, z, z, *tables, cast_w]
    return pl.pallas_call(
        functools.partial(_gates_attn_kernel, layer=layer),
        grid=(GA_ROWS, GA_COLS),
        in_specs=in_specs,
        out_specs=out_specs,
        out_shape=out_shape,
        scratch_shapes=[pltpu.VMEM((TM_GA, D_MODEL), bf16),
                        pltpu.VMEM((WINDOW, KV_WIDTH), f32), pltpu.VMEM((WINDOW, KV_WIDTH), f32)],
        input_output_aliases={0: 1},
        compiler_params=_params(2, 52),
        name="gates_attn_prompt",
    )(*args)


SB_ATTN = 8
KALL = WINDOW + SUBLANE


def _attn_s_kernel(sinks_ref, z_ref, ck_ref, cv_ref, c_ref, sa_ref, sb_ref, y_ref, nk_ref, nv_ref,
                   kall, vall, qs, *, layer):
    c, sa, sb = c_ref[...], sa_ref[...], sb_ref[...]
    rows = GQA_GROUP * DEC_SEQ
    kall[WINDOW:KALL, :] = jnp.zeros((SUBLANE, KV_WIDTH), f32)
    vall[WINDOW:KALL, :] = jnp.zeros((SUBLANE, KV_WIDTH), f32)
    qt = lax.broadcasted_iota(jnp.int32, (rows, KALL), 0) % DEC_SEQ
    kj = lax.broadcasted_iota(jnp.int32, (rows, KALL), 1)
    mask = jnp.where(kj < WINDOW, kj - qt, qt - (kj - WINDOW) + 1) > 0
    rg = lax.broadcasted_iota(jnp.int32, (rows, 1), 0) // DEC_SEQ
    for bb in range(SB_ATTN):
        zb = z_ref[bb]
        q = _rope(zb[:, OFF_Q:OFF_Q + ATTN_WIDTH], c, sa, sb)
        kn = _rope(zb[:, OFF_K:OFF_K + KV_WIDTH], c, sa, sb)
        vn = zb[:, OFF_V:OFF_V + KV_WIDTH]
        kall[0:WINDOW, :] = ck_ref[bb]
        vall[0:WINDOW, :] = cv_ref[bb]
        kall[WINDOW:WINDOW + DEC_SEQ, :] = kn
        vall[WINDOW:WINDOW + DEC_SEQ, :] = vn
        nk_ref[bb] = kall[DEC_SEQ:DEC_SEQ + WINDOW, :]
        nv_ref[bb] = vall[DEC_SEQ:DEC_SEQ + WINDOW, :]
        for h in range(KV_HEADS):
            for g in range(GQA_GROUP):
                hd = GQA_GROUP * h + g
                qs[DEC_SEQ * g:DEC_SEQ * (g + 1), :] = q[:, hd * HEAD_DIM:(hd + 1) * HEAD_DIM]
            qh = qs[...].astype(bf16)
            kh = kall[:, HEAD_DIM * h:HEAD_DIM * (h + 1)].astype(bf16)
            vh = vall[:, HEAD_DIM * h:HEAD_DIM * (h + 1)].astype(bf16)
            s = lax.dot_general(qh, kh, NT_DIMS, preferred_element_type=f32) * (HEAD_DIM ** -0.5)
            s = jnp.where(mask, s, NEG_INF)
            sink = jnp.zeros((rows, 1), f32)
            for g in range(GQA_GROUP):
                sink = jnp.where(rg == g, sinks_ref[layer, GQA_GROUP * h + g], sink)
            pn = _sink_softmax(s, sink).astype(bf16)
            o = jnp.dot(pn, vh, preferred_element_type=f32)
            for g in range(GQA_GROUP):
                hd = GQA_GROUP * h + g
                y_ref[bb, :, hd * HEAD_DIM:(hd + 1) * HEAD_DIM] = o[DEC_SEQ * g:DEC_SEQ * (g + 1), :]


def _attn_sample(zs3, cache_k, cache_v, sinks, tables, layer):
    qkv = ATTN_WIDTH + 2 * KV_WIDTH
    return pl.pallas_call(
        functools.partial(_attn_s_kernel, layer=layer),
        grid=(DEC_BATCH // SB_ATTN,),
        in_specs=[pl.BlockSpec(memory_space=pltpu.SMEM),
                  pl.BlockSpec((SB_ATTN, DEC_SEQ, qkv), lambda i: (i, 0, 0)),
                  pl.BlockSpec((None, SB_ATTN, WINDOW, KV_WIDTH), lambda i: (layer, i, 0, 0)),
                  pl.BlockSpec((None, SB_ATTN, WINDOW, KV_WIDTH), lambda i: (layer, i, 0, 0)),
                  pl.BlockSpec((DEC_SEQ, LANE), lambda i: (0, 0)),
                  pl.BlockSpec((DEC_SEQ, LANE), lambda i: (0, 0)),
                  pl.BlockSpec((DEC_SEQ, LANE), lambda i: (0, 0))],
        out_specs=[pl.BlockSpec((SB_ATTN, DEC_SEQ, ATTN_WIDTH), lambda i: (i, 0, 0)),
                   pl.BlockSpec((SB_ATTN, WINDOW, KV_WIDTH), lambda i: (i, 0, 0)),
                   pl.BlockSpec((SB_ATTN, WINDOW, KV_WIDTH), lambda i: (i, 0, 0))],
        out_shape=[jax.ShapeDtypeStruct((DEC_BATCH, DEC_SEQ, ATTN_WIDTH), f32),
                   jax.ShapeDtypeStruct((DEC_BATCH, WINDOW, KV_WIDTH), f32),
                   jax.ShapeDtypeStruct((DEC_BATCH, WINDOW, KV_WIDTH), f32)],
        scratch_shapes=[pltpu.VMEM((KALL, KV_WIDTH), f32), pltpu.VMEM((KALL, KV_WIDTH), f32),
                        pltpu.VMEM((GQA_GROUP * DEC_SEQ, HEAD_DIM), f32)],
        compiler_params=_params(1, 32),
        name="attn_sample",
    )(sinks, zs3, cache_k, cache_v, *tables)


def _sconv_p_kernel(y_init, cb_ref, cc_ref, ch_ref, w_ref, y_ref, st_ref, ubuf):
    del y_init
    t = pl.program_id(2)
    tt = cc_ref.shape[0]

    @pl.when(t == 0)
    def _():
        ubuf[0:SUBLANE, :] = jnp.zeros((SUBLANE, CW), f32)

    ubuf[SUBLANE:SUBLANE + tt, :] = cc_ref[...] * ch_ref[...]
    w = w_ref[...]
    yc = ubuf[SUBLANE - 2:SUBLANE - 2 + tt, :] * w[0:1]
    yc = yc + ubuf[SUBLANE - 1:SUBLANE - 1 + tt, :] * w[1:2]
    yc = yc + ubuf[SUBLANE:SUBLANE + tt, :] * w[2:3]
    y_ref[...] = (cb_ref[...] * yc).astype(bf16)
    last = ubuf[tt:tt + SUBLANE, :]
    st_ref[...] = last
    ubuf[0:SUBLANE, :] = last


def _sconv_prompt(y_init, z, w_sconv, layer):
    nt = SEQ // TT_CONV
    zspec = lambda off: pl.BlockSpec((TT_CONV, CW), lambda b, c, t: (b * nt + t, off // CW + c))
    return pl.pallas_call(
        _sconv_p_kernel,
        grid=(BATCH, CONV_WIDTH // CW, nt),
        in_specs=[pl.BlockSpec(memory_space=pl.ANY), zspec(OFF_CB), zspec(OFF_CC), zspec(OFF_CH),
                  pl.BlockSpec((None, SCONV_K, CW), lambda b, c, t: (layer, 0, c))],
        out_specs=[pl.BlockSpec((TT_CONV, CW), lambda b, c, t: (b * nt + t, c)),
                   pl.BlockSpec((None, SUBLANE, CW), lambda b, c, t: (b, 0, c))],
        out_shape=[jax.ShapeDtypeStruct((ROWS, CONV_WIDTH), bf16),
                   jax.ShapeDtypeStruct((BATCH, SUBLANE, CONV_WIDTH), f32)],
        scratch_shapes=[pltpu.VMEM((TT_CONV + SUBLANE, CW), f32)],
        input_output_aliases={0: 0},
        compiler_params=_params(3, 40),
        name="sconv_prompt",
    )(y_init, z, z, z, w_sconv)


LRU_BLOCKS_PER_TILE = CW // LRU_BLOCK_DIM


def _build_block_diag(w_ref, bd_ref):
    bd_ref[...] = jnp.zeros(bd_ref.shape, bf16)
    for n in range(LRU_BLOCKS_PER_TILE):
        lo, hi = LRU_BLOCK_DIM * n, LRU_BLOCK_DIM * (n + 1)
        bd_ref[lo:hi, lo:hi] = w_ref[n].astype(bf16)


def _lru_gates(xc, wa_bd, wx_bd, ba, bx, lam):
    xcb = xc.astype(bf16)
    r = jax.nn.sigmoid(jnp.dot(xcb, wa_bd, preferred_element_type=f32) + ba)
    i = jax.nn.sigmoid(jnp.dot(xcb, wx_bd, preferred_element_type=f32) + bx)
    log_a = -LRU_C * r * jax.nn.softplus(-lam)
    a = jnp.exp(log_a)
    mult = jnp.sqrt(-_expm1(2.0 * log_a))
    return a, mult * (i * xc)


def _lru_p_kernel(y_init, xl_ref, wc_ref, bc_ref, wa_ref, ba_ref, wx_ref, bx_ref, lam_ref,
                  y_ref, h_ref, xbuf, wa_bd, wx_bd, a_s, b_s, hcar):
    del y_init
    t = pl.program_id(2)
    tt = xl_ref.shape[0]

    @pl.when(t == 0)
    def _():
        xbuf[0:SUBLANE, :] = jnp.zeros((SUBLANE, CW), f32)
        hcar[...] = jnp.zeros(hcar.shape, f32)
        _build_block_diag(wa_ref, wa_bd)
        _build_block_diag(wx_ref, wx_bd)

    xbuf[SUBLANE:SUBLANE + tt, :] = xl_ref[...]
    w = wc_ref[...]
    xc = xbuf[SUBLANE - 3:SUBLANE - 3 + tt, :] * w[0:1]
    for j in range(1, LRU_CONV_K):
        xc = xc + xbuf[SUBLANE - 3 + j:SUBLANE - 3 + j + tt, :] * w[j:j + 1]
    xc = xc + bc_ref[...]
    xbuf[0:SUBLANE, :] = xbuf[tt:tt + SUBLANE, :]

    a, bx = _lru_gates(xc, wa_bd[...], wx_bd[...], ba_ref[...], bx_ref[...], lam_ref[...])

    r8 = lax.broadcasted_iota(jnp.int32, (tt, CW), 0) % SUBLANE
    for s in (1, 2, 4):
        keep = r8 >= s
        a_sh = jnp.where(keep, pltpu.roll(a, s, 0), 1.0)
        b_sh = jnp.where(keep, pltpu.roll(bx, s, 0), 0.0)
        bx = bx + a * b_sh
        a = a * a_sh
    a_s[...] = a
    b_s[...] = bx

    def tile_step(j, h):
        r0 = pl.multiple_of(j * SUBLANE, SUBLANE)
        ht = b_s[pl.ds(r0, SUBLANE), :] + a_s[pl.ds(r0, SUBLANE), :] * h
        b_s[pl.ds(r0, SUBLANE), :] = ht
        return ht[SUBLANE - 1:SUBLANE, :]

    h_last = lax.fori_loop(0, tt // SUBLANE, tile_step, hcar[...])
    hcar[...] = h_last
    y_ref[...] = b_s[...].astype(bf16)
    h_ref[...] = b_s[tt - SUBLANE:tt, :]


def _lru_specs(layer, idx):
    return [pl.BlockSpec((None, LRU_CONV_K, CW), lambda *g: (layer, 0, idx(*g))),
            pl.BlockSpec((None, 1, CW), lambda *g: (layer, 0, idx(*g))),
            pl.BlockSpec((None, LRU_BLOCKS_PER_TILE, LRU_BLOCK_DIM, LRU_BLOCK_DIM),
                         lambda *g: (layer, idx(*g), 0, 0)),
            pl.BlockSpec((None, 1, CW), lambda *g: (layer, 0, idx(*g))),
            pl.BlockSpec((None, LRU_BLOCKS_PER_TILE, LRU_BLOCK_DIM, LRU_BLOCK_DIM),
                         lambda *g: (layer, idx(*g), 0, 0)),
            pl.BlockSpec((None, 1, CW), lambda *g: (layer, 0, idx(*g))),
            pl.BlockSpec((None, 1, CW), lambda *g: (layer, 0, idx(*g)))]


def _lru_prompt(y_init, z, lru_w, layer):
    nt = SEQ // TT_LRU
    return pl.pallas_call(
        _lru_p_kernel,
        grid=(BATCH, LRU_WIDTH // CW, nt),
        in_specs=[pl.BlockSpec(memory_space=pl.ANY),
                  pl.BlockSpec((TT_LRU, CW), lambda b, c, t: (b * nt + t, OFF_XL // CW + c))]
                 + _lru_specs(layer, lambda b, c, t: c),
        out_specs=[pl.BlockSpec((TT_LRU, CW), lambda b, c, t: (b * nt + t, c)),
                   pl.BlockSpec((None, SUBLANE, CW), lambda b, c, t: (b, 0, c))],
        out_shape=[jax.ShapeDtypeStruct((ROWS, LRU_WIDTH), bf16),
                   jax.ShapeDtypeStruct((BATCH, SUBLANE, LRU_WIDTH), f32)],
        input_output_aliases={0: 0},
        scratch_shapes=[pltpu.VMEM((TT_LRU + SUBLANE, CW), f32),
                        pltpu.VMEM((CW, CW), bf16), pltpu.VMEM((CW, CW), bf16),
                        pltpu.VMEM((TT_LRU, CW), f32), pltpu.VMEM((TT_LRU, CW), f32),
                        pltpu.VMEM((1, CW), f32)],
        compiler_params=_params(3, 40),
        name="lru_prompt",
    )(y_init, z, *lru_w)


def _seq_s_kernel(cb_ref, cc_ref, ch_ref, xl_ref, sc_ref, lc_ref, h0_ref, wsc_ref,
                  wc_ref, bc_ref, wa_ref, ba_ref, wx_ref, bx_ref, lam_ref,
                  yc_ref, yl_ref, scn_ref, hn_ref, wa_bd, wx_bd):
    nb = DEC_BATCH
    rows = lambda v, t: v[nb * t:nb * (t + 1), :]
    u = cc_ref[...] * ch_ref[...]
    cb = cb_ref[...]
    up = [sc_ref[j] for j in range(SCONV_K - 1)] + [rows(u, t) for t in range(DEC_SEQ)]
    w = wsc_ref[...]
    for t in range(DEC_SEQ):
        yc = up[t] * w[0:1]
        for j in range(1, SCONV_K):
            yc = yc + up[t + j] * w[j:j + 1]
        yc_ref[nb * t:nb * (t + 1), :] = (rows(cb, t) * yc).astype(bf16)
    for j in range(SCONV_K - 1):
        scn_ref[j] = up[DEC_SEQ + j]

    _build_block_diag(wa_ref, wa_bd)
    _build_block_diag(wx_ref, wx_bd)
    xl = xl_ref[...]
    xp = [lc_ref[j] for j in range(LRU_CONV_K - 1)] + [rows(xl, t) for t in range(DEC_SEQ)]
    wl = wc_ref[...]
    xcs = []
    for t in range(DEC_SEQ):
        xc = xp[t] * wl[0:1]
        for j in range(1, LRU_CONV_K):
            xc = xc + xp[t + j] * wl[j:j + 1]
        xcs.append(xc + bc_ref[...])
    xc = jnp.concatenate(xcs, axis=0)
    a, bx = _lru_gates(xc, wa_bd[...], wx_bd[...], ba_ref[...], bx_ref[...], lam_ref[...])
    h = h0_ref[...]
    for t in range(DEC_SEQ):
        h = rows(a, t) * h + rows(bx, t)
        yl_ref[nb * t:nb * (t + 1), :] = h.astype(bf16)
    hn_ref[...] = h


def _seq_sample(z, st_sconv_t, st_lconv_t, st_h, w_sconv, lru_w, layer):
    rblk = ROWS_P // ROWS_S
    zspec = lambda off: pl.BlockSpec((ROWS_S, CW), lambda c: (rblk, off // CW + c))
    return pl.pallas_call(
        _seq_s_kernel,
        grid=(CONV_WIDTH // CW,),
        in_specs=[zspec(OFF_CB), zspec(OFF_CC), zspec(OFF_CH), zspec(OFF_XL),
                  pl.BlockSpec((None, SCONV_K - 1, DEC_BATCH, CW), lambda c: (layer, 0, 0, c)),
                  pl.BlockSpec((None, LRU_CONV_K - 1, DEC_BATCH, CW), lambda c: (layer, 0, 0, c)),
                  pl.BlockSpec((None, DEC_BATCH, CW), lambda c: (layer, 0, c)),
                  pl.BlockSpec((None, SCONV_K, CW), lambda c: (layer, 0, c))]
                 + _lru_specs(layer, lambda c: c),
        out_specs=[pl.BlockSpec((ROWS_S, CW), lambda c: (0, c)),
                   pl.BlockSpec((ROWS_S, CW), lambda c: (0, c)),
                   pl.BlockSpec((SCONV_K - 1, DEC_BATCH, CW), lambda c: (0, 0, c)),
                   pl.BlockSpec((DEC_BATCH, CW), lambda c: (0, c))],
        out_shape=[jax.ShapeDtypeStruct((ROWS_S, CONV_WIDTH), bf16),
                   jax.ShapeDtypeStruct((ROWS_S, LRU_WIDTH), bf16),
                   jax.ShapeDtypeStruct((SCONV_K - 1, DEC_BATCH, CONV_WIDTH), f32),
                   jax.ShapeDtypeStruct((DEC_BATCH, LRU_WIDTH), f32)],
        scratch_shapes=[pltpu.VMEM((CW, CW), bf16), pltpu.VMEM((CW, CW), bf16)],
        compiler_params=_params(1, 32),
        name="seq_sample",
    )(z, z, z, z, st_sconv_t, st_lconv_t, st_h, w_sconv, *lru_w)


def _softmax_rows(s):
    m = jnp.max(s, axis=-1, keepdims=True)
    p = jnp.exp(s - m)
    return p * (1.0 / jnp.sum(p, axis=-1, keepdims=True))


def _mem_p_kernel(y_init, q_ref, k_ref, v_ref, y_ref):
    del y_init
    s = lax.dot_general(q_ref[...].astype(bf16), k_ref[...].astype(bf16), NT_DIMS,
                        preferred_element_type=f32) * (MEM_HEAD_DIM ** -0.5)
    pn = _softmax_rows(s).astype(bf16)
    y_ref[...] = jnp.dot(pn, v_ref[...].astype(bf16), preferred_element_type=f32).astype(bf16)


def _mem_prompt(y_init, z, mkv):
    nt = SEQ // TQ_MEM
    hw = MEM_HEAD_DIM
    return pl.pallas_call(
        _mem_p_kernel,
        grid=(BATCH, MEM_HEADS, nt),
        in_specs=[pl.BlockSpec(memory_space=pl.ANY),
                  pl.BlockSpec((TQ_MEM, hw), lambda b, h, t: (b * nt + t, OFF_QM // hw + h)),
                  pl.BlockSpec((MEM_LEN, hw), lambda b, h, t: (b, h)),
                  pl.BlockSpec((MEM_LEN, hw), lambda b, h, t: (b, MEM_HEADS + h))],
        out_specs=pl.BlockSpec((TQ_MEM, hw), lambda b, h, t: (b * nt + t, h)),
        out_shape=jax.ShapeDtypeStruct((ROWS, MEM_WIDTH), bf16),
        input_output_aliases={0: 0},
        compiler_params=_params(3, 32),
        name="mem_prompt",
    )(y_init, z, mkv, mkv)


SB_MEM = 4


QEXP = MEM_HEADS * DEC_SEQ


def _mem_s_kernel(q_ref, k_ref, v_ref, y_ref, qexp):
    hw = MEM_HEAD_DIM
    qexp[...] = jnp.zeros(qexp.shape, f32)
    for bb in range(SB_MEM):
        qb = q_ref[bb]
        for h in range(MEM_HEADS):
            qexp[DEC_SEQ * h:DEC_SEQ * (h + 1), hw * h:hw * (h + 1)] = qb[:, hw * h:hw * (h + 1)]
        s = lax.dot_general(qexp[...].astype(bf16), k_ref[bb].astype(bf16), NT_DIMS,
                            preferred_element_type=f32) * (MEM_HEAD_DIM ** -0.5)
        pn = _softmax_rows(s).astype(bf16)
        o = jnp.dot(pn, v_ref[bb].astype(bf16), preferred_element_type=f32)
        for h in range(MEM_HEADS):
            y_ref[bb, :, hw * h:hw * (h + 1)] = o[DEC_SEQ * h:DEC_SEQ * (h + 1), hw * h:hw * (h + 1)]


def _mem_sample(qm3, cache_k, cache_v, layer):
    cache_spec = pl.BlockSpec((None, SB_MEM, MEM_LEN, MEM_WIDTH), lambda i: (layer, i, 0, 0))
    return pl.pallas_call(
        _mem_s_kernel,
        grid=(DEC_BATCH // SB_MEM,),
        in_specs=[pl.BlockSpec((SB_MEM, DEC_SEQ, MEM_WIDTH), lambda i: (i, 0, 0)),
                  cache_spec, cache_spec],
        out_specs=pl.BlockSpec((SB_MEM, DEC_SEQ, MEM_WIDTH), lambda i: (i, 0, 0)),
        out_shape=jax.ShapeDtypeStruct((DEC_BATCH, DEC_SEQ, MEM_WIDTH), f32),
        scratch_shapes=[pltpu.VMEM((QEXP, MEM_WIDTH), f32)],
        compiler_params=_params(1, 40),
        name="mem_sample",
    )(qm3, cache_k, cache_v)


def _merge_kernel(y0, y1, y2, y3, g0, g1, g2, g3, wb_ref, o_ref):
    acc = None
    for b, (y, g) in enumerate(((y0, g0), (y1, g1), (y2, g2), (y3, g3))):
        proj = jnp.dot(y[...], wb_ref[b], preferred_element_type=f32)
        term = jax.nn.sigmoid(g[...].astype(f32)) * proj
        acc = term if acc is None else acc + term
    o_ref[...] = acc.astype(bf16)


def _merge(ys, zg, w_branch, layer):
    tn = TN_MERGE
    yspec = pl.BlockSpec((TM, BRANCH_WIDTH), lambda i, j: (i, 0))
    gspec = lambda b: pl.BlockSpec((TM, tn), lambda i, j: (i, (b * D_MODEL) // tn + j))
    return pl.pallas_call(
        _merge_kernel,
        grid=(zg.shape[0] // TM, D_MODEL // tn),
        in_specs=[yspec] * N_BRANCH + [gspec(b) for b in range(N_BRANCH)]
                 + [pl.BlockSpec((None, N_BRANCH, BRANCH_WIDTH, tn), lambda i, j: (layer, 0, 0, j))],
        out_specs=pl.BlockSpec((TM, tn), lambda i, j: (i, j)),
        out_shape=jax.ShapeDtypeStruct((zg.shape[0], D_MODEL), bf16),
        compiler_params=_params(2, 48),
        name="merge",
    )(*ys, zg, zg, zg, zg, w_branch)


def _oproj_kernel(m_ref, w_ref, x_ref, g_ref, b_ref, o_ref):
    d = jnp.dot(m_ref[...], w_ref[...], preferred_element_type=f32)
    o_ref[...] = _layer_norm(ALPHA * x_ref[...] + d, g_ref[0:1, :], b_ref[0:1, :])


def _oproj(merged, w_o, x, ln_g, ln_b, layer):
    tm = TM_OPROJ
    return pl.pallas_call(
        _oproj_kernel,
        grid=(x.shape[0] // tm,),
        in_specs=[pl.BlockSpec((tm, D_MODEL), lambda i: (i, 0)),
                  pl.BlockSpec((None, D_MODEL, D_MODEL), lambda i: (layer, 0, 0)),
                  pl.BlockSpec((tm, D_MODEL), lambda i: (i, 0)),
                  pl.BlockSpec((None, 2, D_MODEL), lambda i: (layer, 0, 0)),
                  pl.BlockSpec((None, 2, D_MODEL), lambda i: (layer, 0, 0))],
        out_specs=pl.BlockSpec((tm, D_MODEL), lambda i: (i, 0)),
        out_shape=jax.ShapeDtypeStruct(x.shape, f32),
        compiler_params=_params(1, 48),
        name="oproj_ln",
    )(merged, w_o, x, ln_g, ln_b)


def _swiglu_partial(xb, wg_ref, wu_ref, wd_ref):
    g = jnp.dot(xb, wg_ref[...], preferred_element_type=f32)
    u = jnp.dot(xb, wu_ref[...], preferred_element_type=f32)
    h = (jax.nn.silu(g) * u).astype(bf16)
    return jnp.dot(h, wd_ref[...], preferred_element_type=f32)


def _ffn_kernel(x_ref, wg_ref, wu_ref, wd_ref, g_ref, b_ref, cast_in, o_ref, cast_out, xb_ref):
    f = pl.program_id(1)
    cast_out[...] = cast_in[...].astype(bf16)

    @pl.when(f == 0)
    def _():
        xb_ref[...] = x_ref[...].astype(bf16)
        o_ref[...] = jnp.zeros(o_ref.shape, f32)

    o_ref[...] += _swiglu_partial(xb_ref[...], wg_ref, wu_ref, wd_ref)

    @pl.when(f == pl.num_programs(1) - 1)
    def _():
        o_ref[...] = _layer_norm(ALPHA * x_ref[...] + o_ref[...], g_ref[1:2, :], b_ref[1:2, :])


def _ffn(x, wg, wu, wd, ln_g, ln_b, layer, j, cast_w, cast_j):
    tm = TM_FFN
    n_f = D_FF // TF
    w_in = pl.BlockSpec((None, D_MODEL, TF), lambda i, f: (j, 0, f))
    w_out = pl.BlockSpec((None, TF, D_MODEL), lambda i, f: (j, f, 0))
    xspec = pl.BlockSpec((tm, D_MODEL), lambda i, f: (i, 0))
    x_once = pl.BlockSpec((tm, D_MODEL), lambda i, f: (i, 0), pipeline_mode=pl.Buffered(1))
    lnspec = pl.BlockSpec((None, 2, D_MODEL), lambda i, f: (layer, 0, 0))
    n_steps = (x.shape[0] // tm) * n_f
    c_in, c_out, c_shape = _cast_rider(cast_w, cast_j, n_steps, lambda i, f: i * n_f + f)
    return pl.pallas_call(
        _ffn_kernel,
        grid=(x.shape[0] // tm, n_f),
        in_specs=[x_once, w_in, w_in, w_out, lnspec, lnspec, c_in],
        out_specs=[xspec, c_out],
        out_shape=[jax.ShapeDtypeStruct(x.shape, f32), c_shape],
        scratch_shapes=[pltpu.VMEM((tm, D_MODEL), bf16)],
        compiler_params=_params(2, 54),
        name="ffn_dense",
    )(x, wg, wu, wd, ln_g, ln_b, cast_w)


TG = 736


def _max_row_tiles(n_tokens):
    return (n_tokens * TOP_K + N_EXPERTS * (TG - 1)) // TG


def _router_kernel(x_ref, w_ref, p_ref, i_ref, cnt_ref, carry):
    @pl.when(pl.program_id(0) == 0)
    def _():
        carry[...] = jnp.zeros(carry.shape, f32)

    tm = x_ref.shape[0]
    x = x_ref[...]
    w = w_ref[...]
    xh = x.astype(bf16)
    wh = w.astype(bf16)
    xl = (x - xh.astype(f32)).astype(bf16)
    wl = (w - wh.astype(f32)).astype(bf16)
    logits = (jnp.dot(xh, wh, preferred_element_type=f32) + jnp.dot(xl, wh, preferred_element_type=f32)
              + jnp.dot(xh, wl, preferred_element_type=f32))
    lane = lax.broadcasted_iota(jnp.int32, logits.shape, 1)
    logits = jnp.where(lane < N_EXPERTS, logits, -jnp.inf)
    m1 = jnp.max(logits, axis=-1, keepdims=True)
    i1 = jnp.min(jnp.where(logits == m1, lane, LANE), axis=-1, keepdims=True)
    rest = jnp.where(lane == i1, -jnp.inf, logits)
    m2 = jnp.max(rest, axis=-1, keepdims=True)
    i2 = jnp.min(jnp.where(rest == m2, lane, LANE), axis=-1, keepdims=True)
    e = jnp.exp(m2 - m1)
    p1 = 1.0 / (1.0 + e)
    p2 = e / (1.0 + e)
    sel = jnp.where(lane == i1, 1.0, jnp.where(lane == i2, 1.0, 0.0))
    r_i = lax.broadcasted_iota(jnp.int32, (tm, tm), 0)
    c_i = lax.broadcasted_iota(jnp.int32, (tm, tm), 1)
    tri = jnp.where(r_i > c_i, 1.0, 0.0).astype(bf16)
    before = jnp.dot(tri, sel.astype(bf16), preferred_element_type=f32) + carry[...]
    r1 = jnp.sum(jnp.where(lane == i1, before, 0.0), axis=-1, keepdims=True).astype(jnp.int32)
    r2 = jnp.sum(jnp.where(lane == i2, before, 0.0), axis=-1, keepdims=True).astype(jnp.int32)
    carry[...] = carry[...] + jnp.sum(sel, axis=0, keepdims=True)
    cnt_ref[...] = carry[...]
    p_ref[...] = jnp.where(lane == 0, p1, jnp.where(lane == 1, p2, 0.0))
    i_ref[...] = jnp.where(lane == 0, i1, jnp.where(lane == 1, i2,
                           jnp.where(lane == 2, r1, jnp.where(lane == 3, r2, 0))))


def _router(x, w_router_pad, j):
    n = x.shape[0]
    tm = TM_FFN
    return pl.pallas_call(
        _router_kernel,
        grid=(n // tm,),
        in_specs=[pl.BlockSpec((tm, D_MODEL), lambda i: (i, 0)),
                  pl.BlockSpec((None, D_MODEL, LANE), lambda i: (j, 0, 0))],
        out_specs=[pl.BlockSpec((tm, LANE), lambda i: (i, 0)),
                   pl.BlockSpec((tm, LANE), lambda i: (i, 0)),
                   pl.BlockSpec((1, LANE), lambda i: (0, 0))],
        out_shape=[jax.ShapeDtypeStruct((n, LANE), f32),
                   jax.ShapeDtypeStruct((n, LANE), jnp.int32),
                   jax.ShapeDtypeStruct((1, LANE), f32)],
        scratch_shapes=[pltpu.VMEM((1, LANE), f32)],
        compiler_params=_params(1, 40),
        name="router",
    )(x, w_router_pad)


def _routing_plan(iinfo, counts):
    cnt = counts[0, :N_EXPERTS].astype(jnp.int32)
    tiles = (cnt + (TG - 1)) // TG
    end_tile = jnp.cumsum(tiles)
    offset = (end_tile - tiles) * TG
    pos1 = offset[iinfo[:, 0]] + iinfo[:, 2]
    pos2 = offset[iinfo[:, 1]] + iinfo[:, 3]
    tile_ids = jnp.arange(_max_row_tiles(iinfo.shape[0]), dtype=jnp.int32)
    tile_expert = jnp.minimum(jnp.sum(tile_ids[:, None] >= end_tile[None, :], axis=1),
                              N_EXPERTS - 1).astype(jnp.int32)
    return pos1, pos2, tile_expert, end_tile[N_EXPERTS - 1:].astype(jnp.int32)


DMA_UNROLL = 8


def _row_copy(src, s, dst, d, sem):
    return pltpu.make_async_copy(src.at[pl.ds(s, 1)], dst.at[pl.ds(d, 1)], sem)


def _dispatch_kernel(pos1_ref, pos2_ref, x_ref, xs_in, xs_ref, sem):
    del xs_in
    tm = x_ref.shape[0]
    base = pl.program_id(0) * tm

    def issue(r, c):
        _row_copy(x_ref, r, xs_ref, pos1_ref[base + r], sem).start()
        _row_copy(x_ref, r, xs_ref, pos2_ref[base + r], sem).start()
        return c

    def drain(r, c):
        _row_copy(x_ref, r, xs_ref, pos1_ref[base + r], sem).wait()
        _row_copy(x_ref, r, xs_ref, pos2_ref[base + r], sem).wait()
        return c

    lax.fori_loop(0, tm, issue, 0, unroll=DMA_UNROLL)
    lax.fori_loop(0, tm, drain, 0, unroll=DMA_UNROLL)


def _dispatch(x, pos1, pos2):
    tm = TM_FFN
    n_rows = _max_row_tiles(x.shape[0]) * TG
    xs0 = jnp.zeros((n_rows, D_MODEL), f32)
    return pl.pallas_call(
        _dispatch_kernel,
        grid_spec=pltpu.PrefetchScalarGridSpec(
            num_scalar_prefetch=2,
            grid=(x.shape[0] // tm,),
            in_specs=[pl.BlockSpec((tm, D_MODEL), lambda i, p1, p2: (i, 0)),
                      pl.BlockSpec(memory_space=pl.ANY)],
            out_specs=pl.BlockSpec(memory_space=pl.ANY),
            scratch_shapes=[pltpu.SemaphoreType.DMA(())]),
        out_shape=jax.ShapeDtypeStruct((n_rows, D_MODEL), f32),
        input_output_aliases={3: 0},
        compiler_params=_params(1, 32),
        name="moe_dispatch",
    )(pos1, pos2, x, xs0)


def _gffn_kernel(te_ref, nt_ref, x_ref, wg_ref, wu_ref, wd_ref, o_ref, xb_ref):
    del te_ref
    f = pl.program_id(1)
    used = pl.program_id(0) < nt_ref[0]

    @pl.when(jnp.logical_and(jnp.logical_not(used), f == 0))
    def _():
        o_ref[...] = jnp.zeros(o_ref.shape, f32)

    @pl.when(jnp.logical_and(used, f == 0))
    def _():
        xb_ref[...] = x_ref[...].astype(bf16)
        o_ref[...] = jnp.zeros(o_ref.shape, f32)

    @pl.when(used)
    def _():
        o_ref[...] += _swiglu_partial(xb_ref[...], wg_ref, wu_ref, wd_ref)


def _grouped_ffn(xs, tile_expert, n_tiles, wg, wu, wd):
    n_f = D_FF // TF
    row = lambda i, nt: jnp.minimum(i, nt[0] - 1)
    fcol = lambda i, f, nt: jnp.where(i < nt[0], f, n_f - 1)
    xspec = pl.BlockSpec((TG, D_MODEL), lambda i, f, te, nt: (row(i, nt), 0),
                         pipeline_mode=pl.Buffered(1))
    w_in = pl.BlockSpec((None, D_MODEL, TF),
                        lambda i, f, te, nt: (te[row(i, nt)], 0, fcol(i, f, nt)))
    w_out = pl.BlockSpec((None, TF, D_MODEL),
                         lambda i, f, te, nt: (te[row(i, nt)], fcol(i, f, nt), 0))
    return pl.pallas_call(
        _gffn_kernel,
        grid_spec=pltpu.PrefetchScalarGridSpec(
            num_scalar_prefetch=2,
            grid=(xs.shape[0] // TG, n_f),
            in_specs=[xspec, w_in, w_in, w_out],
            out_specs=pl.BlockSpec((TG, D_MODEL), lambda i, f, te, nt: (i, 0)),
            scratch_shapes=[pltpu.VMEM((TG, D_MODEL), bf16)]),
        out_shape=jax.ShapeDtypeStruct(xs.shape, f32),
        compiler_params=_params(2, 52),
        name="ffn_experts",
    )(tile_expert, n_tiles, xs, wg, wu, wd)


def _combine_kernel(pos1_ref, pos2_ref, x_ref, p_ref, y_ref, g_ref, b_ref, o_ref, buf1, buf2, sem):
    tm = x_ref.shape[0]
    base = pl.program_id(0) * tm

    def issue(r, c):
        _row_copy(y_ref, pos1_ref[base + r], buf1, r, sem).start()
        _row_copy(y_ref, pos2_ref[base + r], buf2, r, sem).start()
        return c

    def drain(r, c):
        _row_copy(y_ref, pos1_ref[base + r], buf1, r, sem).wait()
        _row_copy(y_ref, pos2_ref[base + r], buf2, r, sem).wait()
        return c

    lax.fori_loop(0, tm, issue, 0, unroll=DMA_UNROLL)
    lax.fori_loop(0, tm, drain, 0, unroll=DMA_UNROLL)
    p = p_ref[...]
    mixed = p[:, 0:1] * buf1[...] + p[:, 1:2] * buf2[...]
    o_ref[...] = _layer_norm(ALPHA * x_ref[...] + mixed, g_ref[1:2, :], b_ref[1:2, :])


def _combine(x, pinfo, y, pos1, pos2, ln_g, ln_b, layer):
    tm = TM_FFN
    xspec = pl.BlockSpec((tm, D_MODEL), lambda i, p1, p2: (i, 0))
    lnspec = pl.BlockSpec((None, 2, D_MODEL), lambda i, p1, p2: (layer, 0, 0))
    return pl.pallas_call(
        _combine_kernel,
        grid_spec=pltpu.PrefetchScalarGridSpec(
            num_scalar_prefetch=2,
            grid=(x.shape[0] // tm,),
            in_specs=[xspec, pl.BlockSpec((tm, LANE), lambda i, p1, p2: (i, 0)),
                      pl.BlockSpec(memory_space=pl.ANY), lnspec, lnspec],
            out_specs=xspec,
            scratch_shapes=[pltpu.VMEM((tm, D_MODEL), f32), pltpu.VMEM((tm, D_MODEL), f32),
                            pltpu.SemaphoreType.DMA(())]),
        out_shape=jax.ShapeDtypeStruct(x.shape, f32),
        compiler_params=_params(1, 48),
        name="moe_combine",
    )(pos1, pos2, x, pinfo, y, ln_g, ln_b)


def _moe(x, w_router_pad, wg, wu, wd, ln_g, ln_b, layer, j):
    pinfo, iinfo, counts = _router(x, w_router_pad, j)
    pos1, pos2, tile_expert, n_tiles = _routing_plan(iinfo, counts)
    xs = _dispatch(x, pos1, pos2)
    y = _grouped_ffn(xs, tile_expert, n_tiles, wg, wu, wd)
    return _combine(x, pinfo, y, pos1, pos2, ln_g, ln_b, layer)


def kernel(x_prompt, x_sample, mem_prompt, cache_win_k, cache_win_v, cache_mem_k, cache_mem_v,
           state_sconv, state_lru_conv, state_lru_h, w_mix_in, sinks, w_sconv, w_lru_conv, b_lru_conv,
           w_lru_a, b_lru_a, w_lru_x, b_lru_x, lru_lambda, w_mem_kv, w_branch, w_o, ln_g, ln_b,
           w_router, w_ffn_gate, w_ffn_up, w_ffn_down, w_exp_gate, w_exp_up, w_exp_down):
    x = jnp.concatenate([x_prompt.reshape(ROWS_P, D_MODEL),
                         x_sample.transpose(1, 0, 2).reshape(ROWS_S, D_MODEL)], axis=0)
    mem = mem_prompt.reshape(BATCH * MEM_LEN, D_MODEL)
    w_in_b = w_mix_in.astype(bf16)
    w_gate_b = w_in_b[:, :, OFF_G:]
    w_mkv_b = w_mem_kv.astype(bf16)
    w_br_b = w_branch.astype(bf16)
    w_o_b = w_o.astype(bf16)
    wfg, wfu, wfd = w_ffn_gate.astype(bf16), w_ffn_up.astype(bf16), w_ffn_down.astype(bf16)
    n_moe = w_exp_gate.shape[0]
    weg_rows = w_exp_gate.reshape(n_moe, N_EXPERTS * D_MODEL, D_FF)
    weu_rows = w_exp_up.reshape(n_moe, N_EXPERTS * D_MODEL, D_FF)
    wed_rows = w_exp_down.reshape(n_moe, N_EXPERTS * D_FF, D_MODEL)
    w_router_pad = jnp.pad(w_router, ((0, 0), (0, 0), (0, LANE - N_EXPERTS)))
    ck = cache_win_k.reshape(DEPTH, DEC_BATCH, WINDOW, KV_WIDTH)
    cv = cache_win_v.reshape(DEPTH, DEC_BATCH, WINDOW, KV_WIDTH)
    cmk = cache_mem_k.reshape(DEPTH, DEC_BATCH, MEM_LEN, MEM_WIDTH)
    cmv = cache_mem_v.reshape(DEPTH, DEC_BATCH, MEM_LEN, MEM_WIDTH)
    st_sc_t = state_sconv.transpose(0, 2, 1, 3)
    st_lc_t = state_lru_conv.transpose(0, 2, 1, 3)
    lru_w = (w_lru_conv, b_lru_conv.reshape(DEPTH, 1, LRU_WIDTH), w_lru_a,
             b_lru_a.reshape(DEPTH, 1, LRU_WIDTH), w_lru_x, b_lru_x.reshape(DEPTH, 1, LRU_WIDTH),
             lru_lambda.reshape(DEPTH, 1, LRU_WIDTH))
    tab_p = _rope_tables(jnp.arange(SEQ, dtype=jnp.int32))
    tab_s = _rope_tables(PAST_LEN + jnp.arange(DEC_SEQ, dtype=jnp.int32))

    def to_b_major(v):
        return v.reshape(DEC_SEQ, DEC_BATCH, v.shape[-1]).transpose(1, 0, 2)

    def to_t_major(v):
        return v.transpose(1, 0, 2).reshape(ROWS_S, v.shape[-1])

    outs = [[] for _ in range(12)]
    for l in range(DEPTH):
        z = _matmul(x, w_in_b, l, TM, TN_MIX, n=OFF_G)
        mkv = _matmul(mem, w_mkv_b, l, BATCH * MEM_LEN, TN_MKV)
        zs = z[ROWS_P:]

        def all_rows(y_s):
            return lax.dynamic_update_slice(jnp.zeros((ROWS, y_s.shape[-1]), bf16), y_s, (ROWS_P, 0))

        ya_s, nk_s, nv_s = _attn_sample(to_b_major(zs[:, :OFF_CB]), ck, cv, sinks, tab_s, l)
        yc_s, yl_s, sc_s, h_s = _seq_sample(z, st_sc_t, st_lc_t, state_lru_h, w_sconv, lru_w, l)
        ym_s = _mem_sample(to_b_major(zs[:, OFF_QM:OFF_G]), cmk, cmv, l)
        j = l // 2
        zg, ya, krot_p, w_cast = _gates_attn_prompt(
            all_rows(to_t_major(ya_s).astype(bf16)), x, w_gate_b, z, sinks, tab_p, l,
            cast_w=wed_rows if l % 2 == 0 else weu_rows, cast_j=j)
        yc, sc_p = _sconv_prompt(all_rows(yc_s), z, w_sconv, l)
        yl, h_p = _lru_prompt(all_rows(yl_s), z, lru_w, l)
        ym = _mem_prompt(all_rows(to_t_major(ym_s).astype(bf16)), z, mkv)
        merged = _merge((ya, yc, yl, ym), zg, w_br_b, l)
        x = _oproj(merged, w_o_b, x, ln_g, ln_b, l)
        if l % 2 == 0:
            x, weg_j = _ffn(x, wfg, wfu, wfd, ln_g, ln_b, l, j, weg_rows, j)
            wed_j = w_cast
        else:
            x = _moe(x, w_router_pad, weg_j.reshape(N_EXPERTS, D_MODEL, D_FF),
                     w_cast.reshape(N_EXPERTS, D_MODEL, D_FF),
                     wed_j.reshape(N_EXPERTS, D_FF, D_MODEL), ln_g, ln_b, l, j)

        def tail(nrows, lo, hi):
            return jnp.stack([z[(b + 1) * SEQ - nrows:(b + 1) * SEQ, lo:hi] for b in range(BATCH)])

        kv_shape = (BATCH, WINDOW, KV_HEADS, HEAD_DIM)
        outs[0].append(krot_p.reshape(BATCH, SEQ, KV_WIDTH)[:, SEQ - WINDOW:].reshape(kv_shape))
        outs[1].append(tail(WINDOW, OFF_V, OFF_CB).reshape(kv_shape))
        mem_shape = (BATCH, MEM_LEN, MEM_HEADS, MEM_HEAD_DIM)
        outs[2].append(mkv[:, :MEM_WIDTH].reshape(mem_shape))
        outs[3].append(mkv[:, MEM_WIDTH:].reshape(mem_shape))
        outs[4].append(sc_p[:, SUBLANE - (SCONV_K - 1):])
        outs[5].append(tail(LRU_CONV_K - 1, OFF_XL, OFF_QM))
        outs[6].append(h_p[:, SUBLANE - 1])
        kvs_shape = (DEC_BATCH, WINDOW, KV_HEADS, HEAD_DIM)
        outs[7].append(nk_s.reshape(kvs_shape))
        outs[8].append(nv_s.reshape(kvs_shape))
        outs[9].append(sc_s.transpose(1, 0, 2))
        outs[10].append(zs[DEC_BATCH:, OFF_XL:OFF_QM].reshape(LRU_CONV_K - 1, DEC_BATCH, LRU_WIDTH
                                                              ).transpose(1, 0, 2))
        outs[11].append(h_s)

    y_prompt = x[:ROWS_P].reshape(BATCH, SEQ, D_MODEL)
    y_sample = to_b_major(x[ROWS_P:])
    return (y_prompt, y_sample) + tuple(jnp.stack(o) for o in outs)
```

```python
import functools

import jax
import jax.numpy as jnp
from jax import lax
from jax.experimental import pallas as pl
from jax.experimental.pallas import tpu as pltpu

f32 = jnp.float32
bf16 = jnp.bfloat16

D_MODEL = 2048
BATCH = 2
SEQ = 4096
DEPTH = 4
DEC_BATCH = 32
DEC_SEQ = 4
PAST_LEN = 16384
N_HEADS = 16
KV_HEADS = 4
GQA_GROUP = N_HEADS // KV_HEADS
HEAD_DIM = 64
ROPE_DIM = HEAD_DIM // 4
ROPE_THETA = 500000.0
WINDOW = 128
ATTN_WIDTH = N_HEADS * HEAD_DIM
KV_WIDTH = KV_HEADS * HEAD_DIM
CONV_WIDTH = D_MODEL // 2
SCONV_K = 3
LRU_WIDTH = D_MODEL // 2
LRU_BLOCKS = 16
LRU_BLOCK_DIM = LRU_WIDTH // LRU_BLOCKS
LRU_CONV_K = 4
LRU_C = 8.0
MEM_LEN = 256
MEM_HEADS = 4
MEM_HEAD_DIM = 256
MEM_WIDTH = MEM_HEADS * MEM_HEAD_DIM
N_BRANCH = 4
BRANCH_WIDTH = D_MODEL // 2
D_FF = 5632
N_EXPERTS = 8
TOP_K = 2
ALPHA = (2 * DEPTH) ** 0.25
LN_EPS = 1e-5
NEG_INF = -1e30

ROWS_P = BATCH * SEQ
ROWS_S = DEC_BATCH * DEC_SEQ
ROWS = ROWS_P + ROWS_S

OFF_Q = 0
OFF_K = OFF_Q + ATTN_WIDTH
OFF_V = OFF_K + KV_WIDTH
OFF_CB = OFF_V + KV_WIDTH
OFF_CC = OFF_CB + CONV_WIDTH
OFF_CH = OFF_CC + CONV_WIDTH
OFF_XL = OFF_CH + CONV_WIDTH
OFF_QM = OFF_XL + LRU_WIDTH
OFF_G = OFF_QM + MEM_WIDTH
MIX_IN = OFF_G + N_BRANCH * D_MODEL

LANE = 128
SUBLANE = 8
BF16_SUBLANE = 16
MIB = 1024 * 1024
TM = 832
TN_MIX = 1664
TN_MKV = 512
TN_MERGE = 512
TM_OPROJ = 416
TM_FFN = 640
TF = 512
CW = 512
TT_CONV = 1024
TT_LRU = 512
TQ_MEM = 2048

NT_DIMS = (((1,), (1,)), ((), ()))


def _params(n_axes, vmem_mib):
    return pltpu.CompilerParams(dimension_semantics=("arbitrary",) * n_axes,
                                vmem_limit_bytes=vmem_mib * MIB)


def _layer_norm(v, g, b):
    mu = jnp.mean(v, axis=-1, keepdims=True)
    vc = v - mu
    var = jnp.mean(vc * vc, axis=-1, keepdims=True)
    return vc * lax.rsqrt(var + LN_EPS) * g + b


def _expm1(x):
    fact = [1.0]
    for k in range(1, 10):
        fact.append(fact[-1] * k)
    p = 1.0 / fact[9]
    for k in range(8, 0, -1):
        p = p * x + 1.0 / fact[k]
    return jnp.where(jnp.abs(x) < 0.5, x * p, jnp.exp(x) - 1.0)


def _cast_rider(w, j, n_steps, step_of):
    rows, cols = w.shape[1:]
    n_blocks = max(n for n in range(1, n_steps + 1)
                   if rows % n == 0 and (rows // n) % BF16_SUBLANE == 0)
    br = rows // n_blocks
    blk = lambda *g: jnp.minimum(step_of(*g), n_blocks - 1)
    return (pl.BlockSpec((None, br, cols), lambda *g: (j, blk(*g), 0)),
            pl.BlockSpec((br, cols), lambda *g: (blk(*g), 0)),
            jax.ShapeDtypeStruct((rows, cols), bf16))


def _mm_kernel(x_ref, w_ref, o_ref, xb_ref):
    @pl.when(pl.program_id(1) == 0)
    def _():
        xb_ref[...] = x_ref[...].astype(bf16)

    o_ref[...] = jnp.dot(xb_ref[...], w_ref[...], preferred_element_type=f32)


def _matmul(x, w_stack, layer, tm, tn, n=None):
    m, k = x.shape
    n = w_stack.shape[-1] if n is None else n
    vmem = 2 * (tm * k * 4 + k * tn * 2 + tm * tn * 4) + tm * k * 2
    return pl.pallas_call(
        _mm_kernel,
        grid=(m // tm, n // tn),
        in_specs=[pl.BlockSpec((tm, k), lambda i, j: (i, 0)),
                  pl.BlockSpec((None, k, tn), lambda i, j: (layer, 0, j))],
        out_specs=pl.BlockSpec((tm, tn), lambda i, j: (i, j)),
        out_shape=jax.ShapeDtypeStruct((m, n), f32),
        scratch_shapes=[pltpu.VMEM((tm, k), bf16)],
        compiler_params=_params(2, vmem // MIB + 8),
        name="matmul",
    )(x, w_stack)


def _rope_tables(pos):
    half = ROPE_DIM // 2
    inv_freq = ROPE_THETA ** (-jnp.arange(half, dtype=f32) * (2.0 / ROPE_DIM))
    ang = pos.astype(f32)[:, None] * inv_freq[None, :]
    cos = jnp.cos(ang)
    sin = jnp.sin(ang)
    p = pos.shape[0]
    rest = HEAD_DIM - ROPE_DIM
    c = jnp.concatenate([cos, cos, jnp.ones((p, rest), f32)], axis=1)
    sa = jnp.concatenate([-sin, jnp.zeros((p, HEAD_DIM - half), f32)], axis=1)
    sb = jnp.concatenate([jnp.zeros((p, half), f32), sin, jnp.zeros((p, rest), f32)], axis=1)
    rep = LANE // HEAD_DIM
    return jnp.tile(c, (1, rep)), jnp.tile(sa, (1, rep)), jnp.tile(sb, (1, rep))


def _rope(x, c, sa, sb):
    half = ROPE_DIM // 2
    chunks = []
    for j in range(x.shape[1] // LANE):
        xc = x[:, LANE * j:LANE * (j + 1)]
        chunks.append(xc * c + pltpu.roll(xc, LANE - half, 1) * sa + pltpu.roll(xc, half, 1) * sb)
    return chunks[0] if len(chunks) == 1 else jnp.concatenate(chunks, axis=1)


def _sink_softmax(s, sink):
    m = jnp.maximum(jnp.max(s, axis=-1, keepdims=True), sink)
    p = jnp.exp(s - m)
    denom = jnp.sum(p, axis=-1, keepdims=True) + jnp.exp(sink - m)
    return p * (1.0 / denom)


def _attn_block(n, sinks_ref, q_ref, k_ref, v_ref, c_ref, sa_ref, sb_ref, y_ref, kr_ref,
                kprev, vprev, layer, before_head=None):
    @pl.when(n == 0)
    def _():
        kprev[...] = jnp.zeros(kprev.shape, f32)
        vprev[...] = jnp.zeros(vprev.shape, f32)

    c, sa, sb = c_ref[...], sa_ref[...], sb_ref[...]
    q = _rope(q_ref[...], c, sa, sb)
    kc = _rope(k_ref[...], c, sa, sb)
    vc = v_ref[...]
    kr_ref[...] = kc
    kall = jnp.concatenate([kprev[...], kc], axis=0).astype(bf16)
    vall = jnp.concatenate([vprev[...], vc], axis=0).astype(bf16)
    kprev[...] = kc
    vprev[...] = vc

    rows = GQA_GROUP * WINDOW
    qi = lax.broadcasted_iota(jnp.int32, (rows, 2 * WINDOW), 0) % WINDOW
    kj = lax.broadcasted_iota(jnp.int32, (rows, 2 * WINDOW), 1)
    kmin = jnp.where(n > 0, 0, WINDOW)
    mask = (kj > qi) & (kj <= qi + WINDOW) & (kj >= kmin)
    rg = lax.broadcasted_iota(jnp.int32, (rows, 1), 0) // WINDOW
    for h in range(KV_HEADS):
        if before_head is not None:
            before_head(h)
        qh = jnp.concatenate(
            [q[:, (GQA_GROUP * h + g) * HEAD_DIM:(GQA_GROUP * h + g + 1) * HEAD_DIM]
             for g in range(GQA_GROUP)], axis=0).astype(bf16)
        kh = kall[:, HEAD_DIM * h:HEAD_DIM * (h + 1)]
        vh = vall[:, HEAD_DIM * h:HEAD_DIM * (h + 1)]
        s = lax.dot_general(qh, kh, NT_DIMS, preferred_element_type=f32) * (HEAD_DIM ** -0.5)
        s = jnp.where(mask, s, NEG_INF)
        sink = jnp.zeros((rows, 1), f32)
        for g in range(GQA_GROUP):
            sink = jnp.where(rg == g, sinks_ref[layer, GQA_GROUP * h + g], sink)
        pn = _sink_softmax(s, sink).astype(bf16)
        o = jnp.dot(pn, vh, preferred_element_type=f32)
        for g in range(GQA_GROUP):
            hd = GQA_GROUP * h + g
            y_ref[:, hd * HEAD_DIM:(hd + 1) * HEAD_DIM] = o[WINDOW * g:WINDOW * (g + 1), :].astype(bf16)


GA_ROWS = 8
GA_COLS = 8
TM_GA = ROWS // GA_ROWS
TN_GA = N_BRANCH * D_MODEL // GA_COLS


def _gates_attn_kernel(y_init, sinks_ref, x_ref, w_ref, q_ref, k_ref, v_ref, c_ref, sa_ref, sb_ref,
                       cast_in, zg_ref, y_ref, kr_ref, cast_out, xb_ref, kprev, vprev, *, layer):
    del y_init
    cast_out[...] = cast_in[...].astype(bf16)
    j = pl.program_id(1)
    step = pl.program_id(0) * GA_COLS + j

    @pl.when(j == 0)
    def _():
        xb_ref[...] = x_ref[...].astype(bf16)

    cw = TN_GA // KV_HEADS

    def gate_columns(h):
        zg_ref[:, cw * h:cw * (h + 1)] = jnp.dot(
            xb_ref[...], w_ref[:, cw * h:cw * (h + 1)], preferred_element_type=f32).astype(zg_ref.dtype)

    _attn_block(step % (SEQ // WINDOW), sinks_ref, q_ref, k_ref, v_ref, c_ref, sa_ref, sb_ref,
                y_ref, kr_ref, kprev, vprev, layer, before_head=gate_columns)


def _gates_attn_prompt(y_init, x, w_gate_b, z, sinks, tables, layer, cast_w, cast_j):
    nblk = SEQ // WINDOW
    assert GA_ROWS * GA_COLS == BATCH * nblk
    step = lambda i, j: i * GA_COLS + j
    tab = pl.BlockSpec((WINDOW, LANE), lambda i, j: (step(i, j) % nblk, 0))
    c_in, c_out, c_shape = _cast_rider(cast_w, cast_j, GA_ROWS * GA_COLS, step)
    in_specs = [pl.BlockSpec(memory_space=pl.ANY),
                pl.BlockSpec(memory_space=pltpu.SMEM),
                pl.BlockSpec((TM_GA, D_MODEL), lambda i, j: (i, 0), pipeline_mode=pl.Buffered(1)),
                pl.BlockSpec((None, D_MODEL, TN_GA), lambda i, j: (layer, 0, j)),
                pl.BlockSpec((WINDOW, ATTN_WIDTH), lambda i, j: (step(i, j), OFF_Q // ATTN_WIDTH)),
                pl.BlockSpec((WINDOW, KV_WIDTH), lambda i, j: (step(i, j), OFF_K // KV_WIDTH)),
                pl.BlockSpec((WINDOW, KV_WIDTH), lambda i, j: (step(i, j), OFF_V // KV_WIDTH)),
                tab, tab, tab, c_in]
    out_specs = [pl.BlockSpec((TM_GA, TN_GA), lambda i, j: (i, j)),
                 pl.BlockSpec((WINDOW, ATTN_WIDTH), lambda i, j: (step(i, j), 0)),
                 pl.BlockSpec((WINDOW, KV_WIDTH), lambda i, j: (step(i, j), 0)), c_out]
    out_shape = [jax.ShapeDtypeStruct((ROWS, N_BRANCH * D_MODEL), bf16),
                 jax.ShapeDtypeStruct((ROWS, ATTN_WIDTH), bf16),
                 jax.ShapeDtypeStruct((ROWS_P, KV_WIDTH), f32), c_shape]
    args = [y_init, sinks, x, w_gate_b, z, z, z, *tables, cast_w]
    return pl.pallas_call(
        functools.partial(_gates_attn_kernel, layer=layer),
        grid=(GA_ROWS, GA_COLS),
        in_specs=in_specs,
        out_specs=out_specs,
        out_shape=out_shape,
        scratch_shapes=[pltpu.VMEM((TM_GA, D_MODEL), bf16),
                        pltpu.VMEM((WINDOW, KV_WIDTH), f32), pltpu.VMEM((WINDOW, KV_WIDTH), f32)],
        input_output_aliases={0: 1},
        compiler_params=_params(2, 52),
        name="gates_attn_prompt",
    )(*args)


SB_ATTN = 8
KALL = WINDOW + SUBLANE


def _attn_s_kernel(sinks_ref, z_ref, ck_ref, cv_ref, c_ref, sa_ref, sb_ref, y_ref, nk_ref, nv_ref,
                   kall, vall, qs, *, layer):
    c, sa, sb = c_ref[...], sa_ref[...], sb_ref[...]
    rows = GQA_GROUP * DEC_SEQ
    kall[WINDOW:KALL, :] = jnp.zeros((SUBLANE, KV_WIDTH), f32)
    vall[WINDOW:KALL, :] = jnp.zeros((SUBLANE, KV_WIDTH), f32)
    qt = lax.broadcasted_iota(jnp.int32, (rows, KALL), 0) % DEC_SEQ
    kj = lax.broadcasted_iota(jnp.int32, (rows, KALL), 1)
    mask = jnp.where(kj < WINDOW, kj - qt, qt - (kj - WINDOW) + 1) > 0
    rg = lax.broadcasted_iota(jnp.int32, (rows, 1), 0) // DEC_SEQ
    for bb in range(SB_ATTN):
        zb = z_ref[bb]
        q = _rope(zb[:, OFF_Q:OFF_Q + ATTN_WIDTH], c, sa, sb)
        kn = _rope(zb[:, OFF_K:OFF_K + KV_WIDTH], c, sa, sb)
        vn = zb[:, OFF_V:OFF_V + KV_WIDTH]
        kall[0:WINDOW, :] = ck_ref[bb]
        vall[0:WINDOW, :] = cv_ref[bb]
        kall[WINDOW:WINDOW + DEC_SEQ, :] = kn
        vall[WINDOW:WINDOW + DEC_SEQ, :] = vn
        nk_ref[bb] = kall[DEC_SEQ:DEC_SEQ + WINDOW, :]
        nv_ref[bb] = vall[DEC_SEQ:DEC_SEQ + WINDOW, :]
        for h in range(KV_HEADS):
            for g in range(GQA_GROUP):
                hd = GQA_GROUP * h + g
                qs[DEC_SEQ * g:DEC_SEQ * (g + 1), :] = q[:, hd * HEAD_DIM:(hd + 1) * HEAD_DIM]
            qh = qs[...].astype(bf16)
            kh = kall[:, HEAD_DIM * h:HEAD_DIM * (h + 1)].astype(bf16)
            vh = vall[:, HEAD_DIM * h:HEAD_DIM * (h + 1)].astype(bf16)
            s = lax.dot_general(qh, kh, NT_DIMS, preferred_element_type=f32) * (HEAD_DIM ** -0.5)
            s = jnp.where(mask, s, NEG_INF)
            sink = jnp.zeros((rows, 1), f32)
            for g in range(GQA_GROUP):
                sink = jnp.where(rg == g, sinks_ref[layer, GQA_GROUP * h + g], sink)
            pn = _sink_softmax(s, sink).astype(bf16)
            o = jnp.dot(pn, vh, preferred_element_type=f32)
            for g in range(GQA_GROUP):
                hd = GQA_GROUP * h + g
                y_ref[bb, :, hd * HEAD_DIM:(hd + 1) * HEAD_DIM] = o[DEC_SEQ * g:DEC_SEQ * (g + 1), :]


def _attn_sample(zs3, cache_k, cache_v, sinks, tables, layer):
    qkv = ATTN_WIDTH + 2 * KV_WIDTH
    return pl.pallas_call(
        functools.partial(_attn_s_kernel, layer=layer),
        grid=(DEC_BATCH // SB_ATTN,),
        in_specs=[pl.BlockSpec(memory_space=pltpu.SMEM),
                  pl.BlockSpec((SB_ATTN, DEC_SEQ, qkv), lambda i: (i, 0, 0)),
                  pl.BlockSpec((None, SB_ATTN, WINDOW, KV_WIDTH), lambda i: (layer, i, 0, 0)),
                  pl.BlockSpec((None, SB_ATTN, WINDOW, KV_WIDTH), lambda i: (layer, i, 0, 0)),
                  pl.BlockSpec((DEC_SEQ, LANE), lambda i: (0, 0)),
                  pl.BlockSpec((DEC_SEQ, LANE), lambda i: (0, 0)),
                  pl.BlockSpec((DEC_SEQ, LANE), lambda i: (0, 0))],
        out_specs=[pl.BlockSpec((SB_ATTN, DEC_SEQ, ATTN_WIDTH), lambda i: (i, 0, 0)),
                   pl.BlockSpec((SB_ATTN, WINDOW, KV_WIDTH), lambda i: (i, 0, 0)),
                   pl.BlockSpec((SB_ATTN, WINDOW, KV_WIDTH), lambda i: (i, 0, 0))],
        out_shape=[jax.ShapeDtypeStruct((DEC_BATCH, DEC_SEQ, ATTN_WIDTH), f32),
                   jax.ShapeDtypeStruct((DEC_BATCH, WINDOW, KV_WIDTH), f32),
                   jax.ShapeDtypeStruct((DEC_BATCH, WINDOW, KV_WIDTH), f32)],
        scratch_shapes=[pltpu.VMEM((KALL, KV_WIDTH), f32), pltpu.VMEM((KALL, KV_WIDTH), f32),
                        pltpu.VMEM((GQA_GROUP * DEC_SEQ, HEAD_DIM), f32)],
        compiler_params=_params(1, 32),
        name="attn_sample",
    )(sinks, zs3, cache_k, cache_v, *tables)


def _sconv_p_kernel(y_init, cb_ref, cc_ref, ch_ref, w_ref, y_ref, st_ref, ubuf):
    del y_init
    t = pl.program_id(2)
    tt = cc_ref.shape[0]

    @pl.when(t == 0)
    def _():
        ubuf[0:SUBLANE, :] = jnp.zeros((SUBLANE, CW), f32)

    ubuf[SUBLANE:SUBLANE + tt, :] = cc_ref[...] * ch_ref[...]
    w = w_ref[...]
    yc = ubuf[SUBLANE - 2:SUBLANE - 2 + tt, :] * w[0:1]
    yc = yc + ubuf[SUBLANE - 1:SUBLANE - 1 + tt, :] * w[1:2]
    yc = yc + ubuf[SUBLANE:SUBLANE + tt, :] * w[2:3]
    y_ref[...] = (cb_ref[...] * yc).astype(bf16)
    last = ubuf[tt:tt + SUBLANE, :]
    st_ref[...] = last
    ubuf[0:SUBLANE, :] = last


def _sconv_prompt(y_init, z, w_sconv, layer):
    nt = SEQ // TT_CONV
    zspec = lambda off: pl.BlockSpec((TT_CONV, CW), lambda b, c, t: (b * nt + t, off // CW + c))
    return pl.pallas_call(
        _sconv_p_kernel,
        grid=(BATCH, CONV_WIDTH // CW, nt),
        in_specs=[pl.BlockSpec(memory_space=pl.ANY), zspec(OFF_CB), zspec(OFF_CC), zspec(OFF_CH),
                  pl.BlockSpec((None, SCONV_K, CW), lambda b, c, t: (layer, 0, c))],
        out_specs=[pl.BlockSpec((TT_CONV, CW), lambda b, c, t: (b * nt + t, c)),
                   pl.BlockSpec((None, SUBLANE, CW), lambda b, c, t: (b, 0, c))],
        out_shape=[jax.ShapeDtypeStruct((ROWS, CONV_WIDTH), bf16),
                   jax.ShapeDtypeStruct((BATCH, SUBLANE, CONV_WIDTH), f32)],
        scratch_shapes=[pltpu.VMEM((TT_CONV + SUBLANE, CW), f32)],
        input_output_aliases={0: 0},
        compiler_params=_params(3, 40),
        name="sconv_prompt",
    )(y_init, z, z, z, w_sconv)


LRU_BLOCKS_PER_TILE = CW // LRU_BLOCK_DIM


def _build_block_diag(w_ref, bd_ref):
    bd_ref[...] = jnp.zeros(bd_ref.shape, bf16)
    for n in range(LRU_BLOCKS_PER_TILE):
        lo, hi = LRU_BLOCK_DIM * n, LRU_BLOCK_DIM * (n + 1)
        bd_ref[lo:hi, lo:hi] = w_ref[n].astype(bf16)


def _lru_gates(xc, wa_bd, wx_bd, ba, bx, lam):
    xcb = xc.astype(bf16)
    r = jax.nn.sigmoid(jnp.dot(xcb, wa_bd, preferred_element_type=f32) + ba)
    i = jax.nn.sigmoid(jnp.dot(xcb, wx_bd, preferred_element_type=f32) + bx)
    log_a = -LRU_C * r * jax.nn.softplus(-lam)
    a = jnp.exp(log_a)
    mult = jnp.sqrt(-_expm1(2.0 * log_a))
    return a, mult * (i * xc)


def _lru_p_kernel(y_init, xl_ref, wc_ref, bc_ref, wa_ref, ba_ref, wx_ref, bx_ref, lam_ref,
                  y_ref, h_ref, xbuf, wa_bd, wx_bd, a_s, b_s, hcar):
    del y_init
    t = pl.program_id(2)
    tt = xl_ref.shape[0]

    @pl.when(t == 0)
    def _():
        xbuf[0:SUBLANE, :] = jnp.zeros((SUBLANE, CW), f32)
        hcar[...] = jnp.zeros(hcar.shape, f32)
        _build_block_diag(wa_ref, wa_bd)
        _build_block_diag(wx_ref, wx_bd)

    xbuf[SUBLANE:SUBLANE + tt, :] = xl_ref[...]
    w = wc_ref[...]
    xc = xbuf[SUBLANE - 3:SUBLANE - 3 + tt, :] * w[0:1]
    for j in range(1, LRU_CONV_K):
        xc = xc + xbuf[SUBLANE - 3 + j:SUBLANE - 3 + j + tt, :] * w[j:j + 1]
    xc = xc + bc_ref[...]
    xbuf[0:SUBLANE, :] = xbuf[tt:tt + SUBLANE, :]

    a, bx = _lru_gates(xc, wa_bd[...], wx_bd[...], ba_ref[...], bx_ref[...], lam_ref[...])

    r8 = lax.broadcasted_iota(jnp.int32, (tt, CW), 0) % SUBLANE
    for s in (1, 2, 4):
        keep = r8 >= s
        a_sh = jnp.where(keep, pltpu.roll(a, s, 0), 1.0)
        b_sh = jnp.where(keep, pltpu.roll(bx, s, 0), 0.0)
        bx = bx + a * b_sh
        a = a * a_sh
    a_s[...] = a
    b_s[...] = bx

    def tile_step(j, h):
        r0 = pl.multiple_of(j * SUBLANE, SUBLANE)
        ht = b_s[pl.ds(r0, SUBLANE), :] + a_s[pl.ds(r0, SUBLANE), :] * h
        b_s[pl.ds(r0, SUBLANE), :] = ht
        return ht[SUBLANE - 1:SUBLANE, :]

    h_last = lax.fori_loop(0, tt // SUBLANE, tile_step, hcar[...])
    hcar[...] = h_last
    y_ref[...] = b_s[...].astype(bf16)
    h_ref[...] = b_s[tt - SUBLANE:tt, :]


def _lru_specs(layer, idx):
    return [pl.BlockSpec((None, LRU_CONV_K, CW), lambda *g: (layer, 0, idx(*g))),
            pl.BlockSpec((None, 1, CW), lambda *g: (layer, 0, idx(*g))),
            pl.BlockSpec((None, LRU_BLOCKS_PER_TILE, LRU_BLOCK_DIM, LRU_BLOCK_DIM),
                         lambda *g: (layer, idx(*g), 0, 0)),
            pl.BlockSpec((None, 1, CW), lambda *g: (layer, 0, idx(*g))),
            pl.BlockSpec((None, LRU_BLOCKS_PER_TILE, LRU_BLOCK_DIM, LRU_BLOCK_DIM),
                         lambda *g: (layer, idx(*g), 0, 0)),
            pl.BlockSpec((None, 1, CW), lambda *g: (layer, 0, idx(*g))),
            pl.BlockSpec((None, 1, CW), lambda *g: (layer, 0, idx(*g)))]


def _lru_prompt(y_init, z, lru_w, layer):
    nt = SEQ // TT_LRU
    return pl.pallas_call(
        _lru_p_kernel,
        grid=(BATCH, LRU_WIDTH // CW, nt),
        in_specs=[pl.BlockSpec(memory_space=pl.ANY),
                  pl.BlockSpec((TT_LRU, CW), lambda b, c, t: (b * nt + t, OFF_XL // CW + c))]
                 + _lru_specs(layer, lambda b, c, t: c),
        out_specs=[pl.BlockSpec((TT_LRU, CW), lambda b, c, t: (b * nt + t, c)),
                   pl.BlockSpec((None, SUBLANE, CW), lambda b, c, t: (b, 0, c))],
        out_shape=[jax.ShapeDtypeStruct((ROWS, LRU_WIDTH), bf16),
                   jax.ShapeDtypeStruct((BATCH, SUBLANE, LRU_WIDTH), f32)],
        input_output_aliases={0: 0},
        scratch_shapes=[pltpu.VMEM((TT_LRU + SUBLANE, CW), f32),
                        pltpu.VMEM((CW, CW), bf16), pltpu.VMEM((CW, CW), bf16),
                        pltpu.VMEM((TT_LRU, CW), f32), pltpu.VMEM((TT_LRU, CW), f32),
                        pltpu.VMEM((1, CW), f32)],
        compiler_params=_params(3, 40),
        name="lru_prompt",
    )(y_init, z, *lru_w)


def _seq_s_kernel(cb_ref, cc_ref, ch_ref, xl_ref, sc_ref, lc_ref, h0_ref, wsc_ref,
                  wc_ref, bc_ref, wa_ref, ba_ref, wx_ref, bx_ref, lam_ref,
                  yc_ref, yl_ref, scn_ref, hn_ref, wa_bd, wx_bd):
    nb = DEC_BATCH
    rows = lambda v, t: v[nb * t:nb * (t + 1), :]
    u = cc_ref[...] * ch_ref[...]
    cb = cb_ref[...]
    up = [sc_ref[j] for j in range(SCONV_K - 1)] + [rows(u, t) for t in range(DEC_SEQ)]
    w = wsc_ref[...]
    for t in range(DEC_SEQ):
        yc = up[t] * w[0:1]
        for j in range(1, SCONV_K):
            yc = yc + up[t + j] * w[j:j + 1]
        yc_ref[nb * t:nb * (t + 1), :] = (rows(cb, t) * yc).astype(bf16)
    for j in range(SCONV_K - 1):
        scn_ref[j] = up[DEC_SEQ + j]

    _build_block_diag(wa_ref, wa_bd)
    _build_block_diag(wx_ref, wx_bd)
    xl = xl_ref[...]
    xp = [lc_ref[j] for j in range(LRU_CONV_K - 1)] + [rows(xl, t) for t in range(DEC_SEQ)]
    wl = wc_ref[...]
    xcs = []
    for t in range(DEC_SEQ):
        xc = xp[t] * wl[0:1]
        for j in range(1, LRU_CONV_K):
            xc = xc + xp[t + j] * wl[j:j + 1]
        xcs.append(xc + bc_ref[...])
    xc = jnp.concatenate(xcs, axis=0)
    a, bx = _lru_gates(xc, wa_bd[...], wx_bd[...], ba_ref[...], bx_ref[...], lam_ref[...])
    h = h0_ref[...]
    for t in range(DEC_SEQ):
        h = rows(a, t) * h + rows(bx, t)
        yl_ref[nb * t:nb * (t + 1), :] = h.astype(bf16)
    hn_ref[...] = h


def _seq_sample(z, st_sconv_t, st_lconv_t, st_h, w_sconv, lru_w, layer):
    rblk = ROWS_P // ROWS_S
    zspec = lambda off: pl.BlockSpec((ROWS_S, CW), lambda c: (rblk, off // CW + c))
    return pl.pallas_call(
        _seq_s_kernel,
        grid=(CONV_WIDTH // CW,),
        in_specs=[zspec(OFF_CB), zspec(OFF_CC), zspec(OFF_CH), zspec(OFF_XL),
                  pl.BlockSpec((None, SCONV_K - 1, DEC_BATCH, CW), lambda c: (layer, 0, 0, c)),
                  pl.BlockSpec((None, LRU_CONV_K - 1, DEC_BATCH, CW), lambda c: (layer, 0, 0, c)),
                  pl.BlockSpec((None, DEC_BATCH, CW), lambda c: (layer, 0, c)),
                  pl.BlockSpec((None, SCONV_K, CW), lambda c: (layer, 0, c))]
                 + _lru_specs(layer, lambda c: c),
        out_specs=[pl.BlockSpec((ROWS_S, CW), lambda c: (0, c)),
                   pl.BlockSpec((ROWS_S, CW), lambda c: (0, c)),
                   pl.BlockSpec((SCONV_K - 1, DEC_BATCH, CW), lambda c: (0, 0, c)),
                   pl.BlockSpec((DEC_BATCH, CW), lambda c: (0, c))],
        out_shape=[jax.ShapeDtypeStruct((ROWS_S, CONV_WIDTH), bf16),
                   jax.ShapeDtypeStruct((ROWS_S, LRU_WIDTH), bf16),
                   jax.ShapeDtypeStruct((SCONV_K - 1, DEC_BATCH, CONV_WIDTH), f32),
                   jax.ShapeDtypeStruct((DEC_BATCH, LRU_WIDTH), f32)],
        scratch_shapes=[pltpu.VMEM((CW, CW), bf16), pltpu.VMEM((CW, CW), bf16)],
        compiler_params=_params(1, 32),
        name="seq_sample",
    )(z, z, z, z, st_sconv_t, st_lconv_t, st_h, w_sconv, *lru_w)


def _softmax_rows(s):
    m = jnp.max(s, axis=-1, keepdims=True)
    p = jnp.exp(s - m)
    return p * (1.0 / jnp.sum(p, axis=-1, keepdims=True))


def _mem_p_kernel(y_init, q_ref, k_ref, v_ref, y_ref):
    del y_init
    s = lax.dot_general(q_ref[...].astype(bf16), k_ref[...].astype(bf16), NT_DIMS,
                        preferred_element_type=f32) * (MEM_HEAD_DIM ** -0.5)
    pn = _softmax_rows(s).astype(bf16)
    y_ref[...] = jnp.dot(pn, v_ref[...].astype(bf16), preferred_element_type=f32).astype(bf16)


def _mem_prompt(y_init, z, mkv):
    nt = SEQ // TQ_MEM
    hw = MEM_HEAD_DIM
    return pl.pallas_call(
        _mem_p_kernel,
        grid=(BATCH, MEM_HEADS, nt),
        in_specs=[pl.BlockSpec(memory_space=pl.ANY),
                  pl.BlockSpec((TQ_MEM, hw), lambda b, h, t: (b * nt + t, OFF_QM // hw + h)),
                  pl.BlockSpec((MEM_LEN, hw), lambda b, h, t: (b, h)),
                  pl.BlockSpec((MEM_LEN, hw), lambda b, h, t: (b, MEM_HEADS + h))],
        out_specs=pl.BlockSpec((TQ_MEM, hw), lambda b, h, t: (b * nt + t, h)),
        out_shape=jax.ShapeDtypeStruct((ROWS, MEM_WIDTH), bf16),
        input_output_aliases={0: 0},
        compiler_params=_params(3, 32),
        name="mem_prompt",
    )(y_init, z, mkv, mkv)


SB_MEM = 4


QEXP = MEM_HEADS * DEC_SEQ


def _mem_s_kernel(q_ref, k_ref, v_ref, y_ref, qexp):
    hw = MEM_HEAD_DIM
    qexp[...] = jnp.zeros(qexp.shape, f32)
    for bb in range(SB_MEM):
        qb = q_ref[bb]
        for h in range(MEM_HEADS):
            qexp[DEC_SEQ * h:DEC_SEQ * (h + 1), hw * h:hw * (h + 1)] = qb[:, hw * h:hw * (h + 1)]
        s = lax.dot_general(qexp[...].astype(bf16), k_ref[bb].astype(bf16), NT_DIMS,
                            preferred_element_type=f32) * (MEM_HEAD_DIM ** -0.5)
        pn = _softmax_rows(s).astype(bf16)
        o = jnp.dot(pn, v_ref[bb].astype(bf16), preferred_element_type=f32)
        for h in range(MEM_HEADS):
            y_ref[bb, :, hw * h:hw * (h + 1)] = o[DEC_SEQ * h:DEC_SEQ * (h + 1), hw * h:hw * (h + 1)]


def _mem_sample(qm3, cache_k, cache_v, layer):
    cache_spec = pl.BlockSpec((None, SB_MEM, MEM_LEN, MEM_WIDTH), lambda i: (layer, i, 0, 0))
    return pl.pallas_call(
        _mem_s_kernel,
        grid=(DEC_BATCH // SB_MEM,),
        in_specs=[pl.BlockSpec((SB_MEM, DEC_SEQ, MEM_WIDTH), lambda i: (i, 0, 0)),
                  cache_spec, cache_spec],
        out_specs=pl.BlockSpec((SB_MEM, DEC_SEQ, MEM_WIDTH), lambda i: (i, 0, 0)),
        out_shape=jax.ShapeDtypeStruct((DEC_BATCH, DEC_SEQ, MEM_WIDTH), f32),
        scratch_shapes=[pltpu.VMEM((QEXP, MEM_WIDTH), f32)],
        compiler_params=_params(1, 40),
        name="mem_sample",
    )(qm3, cache_k, cache_v)


def _merge_kernel(y0, y1, y2, y3, g0, g1, g2, g3, wb_ref, o_ref):
    acc = None
    for b, (y, g) in enumerate(((y0, g0), (y1, g1), (y2, g2), (y3, g3))):
        proj = jnp.dot(y[...], wb_ref[b], preferred_element_type=f32)
        term = jax.nn.sigmoid(g[...].astype(f32)) * proj
        acc = term if acc is None else acc + term
    o_ref[...] = acc.astype(bf16)


def _merge(ys, zg, w_branch, layer):
    tn = TN_MERGE
    yspec = pl.BlockSpec((TM, BRANCH_WIDTH), lambda i, j: (i, 0))
    gspec = lambda b: pl.BlockSpec((TM, tn), lambda i, j: (i, (b * D_MODEL) // tn + j))
    return pl.pallas_call(
        _merge_kernel,
        grid=(zg.shape[0] // TM, D_MODEL // tn),
        in_specs=[yspec] * N_BRANCH + [gspec(b) for b in range(N_BRANCH)]
                 + [pl.BlockSpec((None, N_BRANCH, BRANCH_WIDTH, tn), lambda i, j: (layer, 0, 0, j))],
        out_specs=pl.BlockSpec((TM, tn), lambda i, j: (i, j)),
        out_shape=jax.ShapeDtypeStruct((zg.shape[0], D_MODEL), bf16),
        compiler_params=_params(2, 48),
        name="merge",
    )(*ys, zg, zg, zg, zg, w_branch)


def _oproj_kernel(m_ref, w_ref, x_ref, g_ref, b_ref, o_ref):
    d = jnp.dot(m_ref[...], w_ref[...], preferred_element_type=f32)
    o_ref[...] = _layer_norm(ALPHA * x_ref[...] + d, g_ref[0:1, :], b_ref[0:1, :])


def _oproj(merged, w_o, x, ln_g, ln_b, layer):
    tm = TM_OPROJ
    return pl.pallas_call(
        _oproj_kernel,
        grid=(x.shape[0] // tm,),
        in_specs=[pl.BlockSpec((tm, D_MODEL), lambda i: (i, 0)),
                  pl.BlockSpec((None, D_MODEL, D_MODEL), lambda i: (layer, 0, 0)),
                  pl.BlockSpec((tm, D_MODEL), lambda i: (i, 0)),
                  pl.BlockSpec((None, 2, D_MODEL), lambda i: (layer, 0, 0)),
                  pl.BlockSpec((None, 2, D_MODEL), lambda i: (layer, 0, 0))],
        out_specs=pl.BlockSpec((tm, D_MODEL), lambda i: (i, 0)),
        out_shape=jax.ShapeDtypeStruct(x.shape, f32),
        compiler_params=_params(1, 48),
        name="oproj_ln",
    )(merged, w_o, x, ln_g, ln_b)


def _swiglu_partial(xb, wg_ref, wu_ref, wd_ref):
    g = jnp.dot(xb, wg_ref[...], preferred_element_type=f32)
    u = jnp.dot(xb, wu_ref[...], preferred_element_type=f32)
    h = (jax.nn.silu(g) * u).astype(bf16)
    return jnp.dot(h, wd_ref[...], preferred_element_type=f32)


def _ffn_kernel(x_ref, wg_ref, wu_ref, wd_ref, g_ref, b_ref, cast_in, o_ref, cast_out, xb_ref):
    f = pl.program_id(1)
    cast_out[...] = cast_in[...].astype(bf16)

    @pl.when(f == 0)
    def _():
        xb_ref[...] = x_ref[...].astype(bf16)
        o_ref[...] = jnp.zeros(o_ref.shape, f32)

    o_ref[...] += _swiglu_partial(xb_ref[...], wg_ref, wu_ref, wd_ref)

    @pl.when(f == pl.num_programs(1) - 1)
    def _():
        o_ref[...] = _layer_norm(ALPHA * x_ref[...] + o_ref[...], g_ref[1:2, :], b_ref[1:2, :])


def _ffn(x, wg, wu, wd, ln_g, ln_b, layer, j, cast_w, cast_j):
    tm = TM_FFN
    n_f = D_FF // TF
    w_in = pl.BlockSpec((None, D_MODEL, TF), lambda i, f: (j, 0, f))
    w_out = pl.BlockSpec((None, TF, D_MODEL), lambda i, f: (j, f, 0))
    xspec = pl.BlockSpec((tm, D_MODEL), lambda i, f: (i, 0))
    x_once = pl.BlockSpec((tm, D_MODEL), lambda i, f: (i, 0), pipeline_mode=pl.Buffered(1))
    lnspec = pl.BlockSpec((None, 2, D_MODEL), lambda i, f: (layer, 0, 0))
    n_steps = (x.shape[0] // tm) * n_f
    c_in, c_out, c_shape = _cast_rider(cast_w, cast_j, n_steps, lambda i, f: i * n_f + f)
    return pl.pallas_call(
        _ffn_kernel,
        grid=(x.shape[0] // tm, n_f),
        in_specs=[x_once, w_in, w_in, w_out, lnspec, lnspec, c_in],
        out_specs=[xspec, c_out],
        out_shape=[jax.ShapeDtypeStruct(x.shape, f32), c_shape],
        scratch_shapes=[pltpu.VMEM((tm, D_MODEL), bf16)],
        compiler_params=_params(2, 54),
        name="ffn_dense",
    )(x, wg, wu, wd, ln_g, ln_b, cast_w)


TG = 512


def _max_row_tiles(n_tokens):
    return (n_tokens * TOP_K + N_EXPERTS * (TG - 1)) // TG


def _router_kernel(x_ref, w_ref, p_ref, i_ref, cnt_ref, carry):
    @pl.when(pl.program_id(0) == 0)
    def _():
        carry[...] = jnp.zeros(carry.shape, f32)

    tm = x_ref.shape[0]
    x = x_ref[...]
    w = w_ref[...]
    xh = x.astype(bf16)
    wh = w.astype(bf16)
    xl = (x - xh.astype(f32)).astype(bf16)
    wl = (w - wh.astype(f32)).astype(bf16)
    logits = (jnp.dot(xh, wh, preferred_element_type=f32) + jnp.dot(xl, wh, preferred_element_type=f32)
              + jnp.dot(xh, wl, preferred_element_type=f32))
    lane = lax.broadcasted_iota(jnp.int32, logits.shape, 1)
    logits = jnp.where(lane < N_EXPERTS, logits, -jnp.inf)
    m1 = jnp.max(logits, axis=-1, keepdims=True)
    i1 = jnp.min(jnp.where(logits == m1, lane, LANE), axis=-1, keepdims=True)
    rest = jnp.where(lane == i1, -jnp.inf, logits)
    m2 = jnp.max(rest, axis=-1, keepdims=True)
    i2 = jnp.min(jnp.where(rest == m2, lane, LANE), axis=-1, keepdims=True)
    e = jnp.exp(m2 - m1)
    p1 = 1.0 / (1.0 + e)
    p2 = e / (1.0 + e)
    sel = jnp.where(lane == i1, 1.0, jnp.where(lane == i2, 1.0, 0.0))
    r_i = lax.broadcasted_iota(jnp.int32, (tm, tm), 0)
    c_i = lax.broadcasted_iota(jnp.int32, (tm, tm), 1)
    tri = jnp.where(r_i > c_i, 1.0, 0.0).astype(bf16)
    before = jnp.dot(tri, sel.astype(bf16), preferred_element_type=f32) + carry[...]
    r1 = jnp.sum(jnp.where(lane == i1, before, 0.0), axis=-1, keepdims=True).astype(jnp.int32)
    r2 = jnp.sum(jnp.where(lane == i2, before, 0.0), axis=-1, keepdims=True).astype(jnp.int32)
    carry[...] = carry[...] + jnp.sum(sel, axis=0, keepdims=True)
    cnt_ref[...] = carry[...]
    p_ref[...] = jnp.where(lane == 0, p1, jnp.where(lane == 1, p2, 0.0))
    i_ref[...] = jnp.where(lane == 0, i1, jnp.where(lane == 1, i2,
                           jnp.where(lane == 2, r1, jnp.where(lane == 3, r2, 0))))


def _router(x, w_router_pad, j):
    n = x.shape[0]
    tm = TM_FFN
    return pl.pallas_call(
        _router_kernel,
        grid=(n // tm,),
        in_specs=[pl.BlockSpec((tm, D_MODEL), lambda i: (i, 0)),
                  pl.BlockSpec((None, D_MODEL, LANE), lambda i: (j, 0, 0))],
        out_specs=[pl.BlockSpec((tm, LANE), lambda i: (i, 0)),
                   pl.BlockSpec((tm, LANE), lambda i: (i, 0)),
                   pl.BlockSpec((1, LANE), lambda i: (0, 0))],
        out_shape=[jax.ShapeDtypeStruct((n, LANE), f32),
                   jax.ShapeDtypeStruct((n, LANE), jnp.int32),
                   jax.ShapeDtypeStruct((1, LANE), f32)],
        scratch_shapes=[pltpu.VMEM((1, LANE), f32)],
        compiler_params=_params(1, 40),
        name="router",
    )(x, w_router_pad)


def _routing_plan(iinfo, counts):
    cnt = counts[0, :N_EXPERTS].astype(jnp.int32)
    tiles = (cnt + (TG - 1)) // TG
    end_tile = jnp.cumsum(tiles)
    offset = (end_tile - tiles) * TG
    pos1 = offset[iinfo[:, 0]] + iinfo[:, 2]
    pos2 = offset[iinfo[:, 1]] + iinfo[:, 3]
    tile_ids = jnp.arange(_max_row_tiles(iinfo.shape[0]), dtype=jnp.int32)
    tile_expert = jnp.minimum(jnp.sum(tile_ids[:, None] >= end_tile[None, :], axis=1),
                              N_EXPERTS - 1).astype(jnp.int32)
    return pos1, pos2, tile_expert, end_tile[N_EXPERTS - 1:].astype(jnp.int32)


DMA_UNROLL = 8


def _row_copy(src, s, dst, d, sem):
    return pltpu.make_async_copy(src.at[pl.ds(s, 1)], dst.at[pl.ds(d, 1)], sem)


def _dispatch_kernel(pos1_ref, pos2_ref, x_ref, xs_in, xs_ref, sem):
    del xs_in
    tm = x_ref.shape[0]
    base = pl.program_id(0) * tm

    def issue(r, c):
        _row_copy(x_ref, r, xs_ref, pos1_ref[base + r], sem).start()
        _row_copy(x_ref, r, xs_ref, pos2_ref[base + r], sem).start()
        return c

    def drain(r, c):
        _row_copy(x_ref, r, xs_ref, pos1_ref[base + r], sem).wait()
        _row_copy(x_ref, r, xs_ref, pos2_ref[base + r], sem).wait()
        return c

    lax.fori_loop(0, tm, issue, 0, unroll=DMA_UNROLL)
    lax.fori_loop(0, tm, drain, 0, unroll=DMA_UNROLL)


def _dispatch(x, pos1, pos2):
    tm = TM_FFN
    n_rows = _max_row_tiles(x.shape[0]) * TG
    xs0 = jnp.zeros((n_rows, D_MODEL), f32)
    return pl.pallas_call(
        _dispatch_kernel,
        grid_spec=pltpu.PrefetchScalarGridSpec(
            num_scalar_prefetch=2,
            grid=(x.shape[0] // tm,),
            in_specs=[pl.BlockSpec((tm, D_MODEL), lambda i, p1, p2: (i, 0)),
                      pl.BlockSpec(memory_space=pl.ANY)],
            out_specs=pl.BlockSpec(memory_space=pl.ANY),
            scratch_shapes=[pltpu.SemaphoreType.DMA(())]),
        out_shape=jax.ShapeDtypeStruct((n_rows, D_MODEL), f32),
        input_output_aliases={3: 0},
        compiler_params=_params(1, 32),
        name="moe_dispatch",
    )(pos1, pos2, x, xs0)


def _gffn_kernel(te_ref, nt_ref, x_ref, wg_ref, wu_ref, wd_ref, o_ref, xb_ref):
    del te_ref
    f = pl.program_id(1)
    used = pl.program_id(0) < nt_ref[0]

    @pl.when(jnp.logical_and(jnp.logical_not(used), f == 0))
    def _():
        o_ref[...] = jnp.zeros(o_ref.shape, f32)

    @pl.when(jnp.logical_and(used, f == 0))
    def _():
        xb_ref[...] = x_ref[...].astype(bf16)
        o_ref[...] = jnp.zeros(o_ref.shape, f32)

    @pl.when(used)
    def _():
        o_ref[...] += _swiglu_partial(xb_ref[...], wg_ref, wu_ref, wd_ref)


def _grouped_ffn(xs, tile_expert, n_tiles, wg, wu, wd):
    n_f = D_FF // TF
    row = lambda i, nt: jnp.minimum(i, nt[0] - 1)
    fcol = lambda i, f, nt: jnp.where(i < nt[0], f, n_f - 1)
    xspec = pl.BlockSpec((TG, D_MODEL), lambda i, f, te, nt: (row(i, nt), 0))
    w_in = pl.BlockSpec((None, D_MODEL, TF),
                        lambda i, f, te, nt: (te[row(i, nt)], 0, fcol(i, f, nt)))
    w_out = pl.BlockSpec((None, TF, D_MODEL),
                         lambda i, f, te, nt: (te[row(i, nt)], fcol(i, f, nt), 0))
    return pl.pallas_call(
        _gffn_kernel,
        grid_spec=pltpu.PrefetchScalarGridSpec(
            num_scalar_prefetch=2,
            grid=(xs.shape[0] // TG, n_f),
            in_specs=[xspec, w_in, w_in, w_out],
            out_specs=pl.BlockSpec((TG, D_MODEL), lambda i, f, te, nt: (i, 0)),
            scratch_shapes=[pltpu.VMEM((TG, D_MODEL), bf16)]),
        out_shape=jax.ShapeDtypeStruct(xs.shape, f32),
        compiler_params=_params(2, 52),
        name="ffn_experts",
    )(tile_expert, n_tiles, xs, wg, wu, wd)


def _combine_kernel(pos1_ref, pos2_ref, x_ref, p_ref, y_ref, g_ref, b_ref, o_ref, buf1, buf2, sem):
    tm = x_ref.shape[0]
    base = pl.program_id(0) * tm

    def issue(r, c):
        _row_copy(y_ref, pos1_ref[base + r], buf1, r, sem).start()
        _row_copy(y_ref, pos2_ref[base + r], buf2, r, sem).start()
        return c

    def drain(r, c):
        _row_copy(y_ref, pos1_ref[base + r], buf1, r, sem).wait()
        _row_copy(y_ref, pos2_ref[base + r], buf2, r, sem).wait()
        return c

    lax.fori_loop(0, tm, issue, 0, unroll=DMA_UNROLL)
    lax.fori_loop(0, tm, drain, 0, unroll=DMA_UNROLL)
    p = p_ref[...]
    mixed = p[:, 0:1] * buf1[...] + p[:, 1:2] * buf2[...]
    o_ref[...] = _layer_norm(ALPHA * x_ref[...] + mixed, g_ref[1:2, :], b_ref[1:2, :])


def _combine(x, pinfo, y, pos1, pos2, ln_g, ln_b, layer):
    tm = TM_FFN
    xspec = pl.BlockSpec((tm, D_MODEL), lambda i, p1, p2: (i, 0))
    lnspec = pl.BlockSpec((None, 2, D_MODEL), lambda i, p1, p2: (layer, 0, 0))
    return pl.pallas_call(
        _combine_kernel,
        grid_spec=pltpu.PrefetchScalarGridSpec(
            num_scalar_prefetch=2,
            grid=(x.shape[0] // tm,),
            in_specs=[xspec, pl.BlockSpec((tm, LANE), lambda i, p1, p2: (i, 0)),
                      pl.BlockSpec(memory_space=pl.ANY), lnspec, lnspec],
            out_specs=xspec,
            scratch_shapes=[pltpu.VMEM((tm, D_MODEL), f32), pltpu.VMEM((tm, D_MODEL), f32),
                            pltpu.SemaphoreType.DMA(())]),
        out_shape=jax.ShapeDtypeStruct(x.shape, f32),
        compiler_params=_params(1, 48),
        name="moe_combine",
    )(pos1, pos2, x, pinfo, y, ln_g, ln_b)


def _moe(x, w_router_pad, wg, wu, wd, ln_g, ln_b, layer, j):
    pinfo, iinfo, counts = _router(x, w_router_pad, j)
    pos1, pos2, tile_expert, n_tiles = _routing_plan(iinfo, counts)
    xs = _dispatch(x, pos1, pos2)
    y = _grouped_ffn(xs, tile_expert, n_tiles, wg, wu, wd)
    return _combine(x, pinfo, y, pos1, pos2, ln_g, ln_b, layer)


def kernel(x_prompt, x_sample, mem_prompt, cache_win_k, cache_win_v, cache_mem_k, cache_mem_v,
           state_sconv, state_lru_conv, state_lru_h, w_mix_in, sinks, w_sconv, w_lru_conv, b_lru_conv,
           w_lru_a, b_lru_a, w_lru_x, b_lru_x, lru_lambda, w_mem_kv, w_branch, w_o, ln_g, ln_b,
           w_router, w_ffn_gate, w_ffn_up, w_ffn_down, w_exp_gate, w_exp_up, w_exp_down):
    x = jnp.concatenate([x_prompt.reshape(ROWS_P, D_MODEL),
                         x_sample.transpose(1, 0, 2).reshape(ROWS_S, D_MODEL)], axis=0)
    mem = mem_prompt.reshape(BATCH * MEM_LEN, D_MODEL)
    w_in_b = w_mix_in.astype(bf16)
    w_gate_b = w_in_b[:, :, OFF_G:]
    w_mkv_b = w_mem_kv.astype(bf16)
    w_br_b = w_branch.astype(bf16)
    w_o_b = w_o.astype(bf16)
    wfg, wfu, wfd = w_ffn_gate.astype(bf16), w_ffn_up.astype(bf16), w_ffn_down.astype(bf16)
    n_moe = w_exp_gate.shape[0]
    weg_rows = w_exp_gate.reshape(n_moe, N_EXPERTS * D_MODEL, D_FF)
    weu_rows = w_exp_up.reshape(n_moe, N_EXPERTS * D_MODEL, D_FF)
    wed_rows = w_exp_down.reshape(n_moe, N_EXPERTS * D_FF, D_MODEL)
    w_router_pad = jnp.pad(w_router, ((0, 0), (0, 0), (0, LANE - N_EXPERTS)))
    ck = cache_win_k.reshape(DEPTH, DEC_BATCH, WINDOW, KV_WIDTH)
    cv = cache_win_v.reshape(DEPTH, DEC_BATCH, WINDOW, KV_WIDTH)
    cmk = cache_mem_k.reshape(DEPTH, DEC_BATCH, MEM_LEN, MEM_WIDTH)
    cmv = cache_mem_v.reshape(DEPTH, DEC_BATCH, MEM_LEN, MEM_WIDTH)
    st_sc_t = state_sconv.transpose(0, 2, 1, 3)
    st_lc_t = state_lru_conv.transpose(0, 2, 1, 3)
    lru_w = (w_lru_conv, b_lru_conv.reshape(DEPTH, 1, LRU_WIDTH), w_lru_a,
             b_lru_a.reshape(DEPTH, 1, LRU_WIDTH), w_lru_x, b_lru_x.reshape(DEPTH, 1, LRU_WIDTH),
             lru_lambda.reshape(DEPTH, 1, LRU_WIDTH))
    tab_p = _rope_tables(jnp.arange(SEQ, dtype=jnp.int32))
    tab_s = _rope_tables(PAST_LEN + jnp.arange(DEC_SEQ, dtype=jnp.int32))

    def to_b_major(v):
        return v.reshape(DEC_SEQ, DEC_BATCH, v.shape[-1]).transpose(1, 0, 2)

    def to_t_major(v):
        return v.transpose(1, 0, 2).reshape(ROWS_S, v.shape[-1])

    outs = [[] for _ in range(12)]
    for l in range(DEPTH):
        z = _matmul(x, w_in_b, l, TM, TN_MIX, n=OFF_G)
        mkv = _matmul(mem, w_mkv_b, l, BATCH * MEM_LEN, TN_MKV)
        zs = z[ROWS_P:]

        def all_rows(y_s):
            return lax.dynamic_update_slice(jnp.zeros((ROWS, y_s.shape[-1]), bf16), y_s, (ROWS_P, 0))

        ya_s, nk_s, nv_s = _attn_sample(to_b_major(zs[:, :OFF_CB]), ck, cv, sinks, tab_s, l)
        yc_s, yl_s, sc_s, h_s = _seq_sample(z, st_sc_t, st_lc_t, state_lru_h, w_sconv, lru_w, l)
        ym_s = _mem_sample(to_b_major(zs[:, OFF_QM:OFF_G]), cmk, cmv, l)
        j = l // 2
        zg, ya, krot_p, w_cast = _gates_attn_prompt(
            all_rows(to_t_major(ya_s).astype(bf16)), x, w_gate_b, z, sinks, tab_p, l,
            cast_w=wed_rows if l % 2 == 0 else weu_rows, cast_j=j)
        yc, sc_p = _sconv_prompt(all_rows(yc_s), z, w_sconv, l)
        yl, h_p = _lru_prompt(all_rows(yl_s), z, lru_w, l)
        ym = _mem_prompt(all_rows(to_t_major(ym_s).astype(bf16)), z, mkv)
        merged = _merge((ya, yc, yl, ym), zg, w_br_b, l)
        x = _oproj(merged, w_o_b, x, ln_g, ln_b, l)
        if l % 2 == 0:
            x, weg_j = _ffn(x, wfg, wfu, wfd, ln_g, ln_b, l, j, weg_rows, j)
            wed_j = w_cast
        else:
            x = _moe(x, w_router_pad, weg_j.reshape(N_EXPERTS, D_MODEL, D_FF),
                     w_cast.reshape(N_EXPERTS, D_MODEL, D_FF),
                     wed_j.reshape(N_EXPERTS, D_FF, D_MODEL), ln_g, ln_b, l, j)

        def tail(nrows, lo, hi):
            return jnp.stack([z[(b + 1) * SEQ - nrows:(b + 1) * SEQ, lo:hi] for b in range(BATCH)])

        kv_shape = (BATCH, WINDOW, KV_HEADS, HEAD_DIM)
        outs[0].append(krot_p.reshape(BATCH, SEQ, KV_WIDTH)[:, SEQ - WINDOW:].reshape(kv_shape))
        outs[1].append(tail(WINDOW, OFF_V, OFF_CB).reshape(kv_shape))
        mem_shape = (BATCH, MEM_LEN, MEM_HEADS, MEM_HEAD_DIM)
        outs[2].append(mkv[:, :MEM_WIDTH].reshape(mem_shape))
        outs[3].append(mkv[:, MEM_WIDTH:].reshape(mem_shape))
        outs[4].append(sc_p[:, SUBLANE - (SCONV_K - 1):])
        outs[5].append(tail(LRU_CONV_K - 1, OFF_XL, OFF_QM))
        outs[6].append(h_p[:, SUBLANE - 1])
        kvs_shape = (DEC_BATCH, WINDOW, KV_HEADS, HEAD_DIM)
        outs[7].append(nk_s.reshape(kvs_shape))
        outs[8].append(nv_s.reshape(kvs_shape))
        outs[9].append(sc_s.transpose(1, 0, 2))
        outs[10].append(zs[DEC_BATCH:, OFF_XL:OFF_QM].reshape(LRU_CONV_K - 1, DEC_BATCH, LRU_WIDTH
                                                              ).transpose(1, 0, 2))
        outs[11].append(h_s)

    y_prompt = x[:ROWS_P].reshape(BATCH, SEQ, D_MODEL)
    y_sample = to_b_major(x[ROWS_P:])
    return (y_prompt, y_sample) + tuple(jnp.stack(o) for o in outs)
```

```python
import functools

import jax
import jax.numpy as jnp
from jax import lax
from jax.experimental import pallas as pl
from jax.experimental.pallas import tpu as pltpu

f32 = jnp.float32
bf16 = jnp.bfloat16

D_MODEL = 2048
BATCH = 2
SEQ = 4096
DEPTH = 4
DEC_BATCH = 32
DEC_SEQ = 4
PAST_LEN = 16384
N_HEADS = 16
KV_HEADS = 4
GQA_GROUP = N_HEADS // KV_HEADS
HEAD_DIM = 64
ROPE_DIM = HEAD_DIM // 4
ROPE_THETA = 500000.0
WINDOW = 128
ATTN_WIDTH = N_HEADS * HEAD_DIM
KV_WIDTH = KV_HEADS * HEAD_DIM
CONV_WIDTH = D_MODEL // 2
SCONV_K = 3
LRU_WIDTH = D_MODEL // 2
LRU_BLOCKS = 16
LRU_BLOCK_DIM = LRU_WIDTH // LRU_BLOCKS
LRU_CONV_K = 4
LRU_C = 8.0
MEM_LEN = 256
MEM_HEADS = 4
MEM_HEAD_DIM = 256
MEM_WIDTH = MEM_HEADS * MEM_HEAD_DIM
N_BRANCH = 4
BRANCH_WIDTH = D_MODEL // 2
D_FF = 5632
N_EXPERTS = 8
TOP_K = 2
ALPHA = (2 * DEPTH) ** 0.25
LN_EPS = 1e-5
NEG_INF = -1e30

ROWS_P = BATCH * SEQ
ROWS_S = DEC_BATCH * DEC_SEQ
ROWS = ROWS_P + ROWS_S

OFF_Q = 0
OFF_K = OFF_Q + ATTN_WIDTH
OFF_V = OFF_K + KV_WIDTH
OFF_CB = OFF_V + KV_WIDTH
OFF_CC = OFF_CB + CONV_WIDTH
OFF_CH = OFF_CC + CONV_WIDTH
OFF_XL = OFF_CH + CONV_WIDTH
OFF_QM = OFF_XL + LRU_WIDTH
OFF_G = OFF_QM + MEM_WIDTH
MIX_IN = OFF_G + N_BRANCH * D_MODEL

LANE = 128
SUBLANE = 8
BF16_SUBLANE = 16
MIB = 1024 * 1024
TM = 832
TN_MIX = 1664
TN_MKV = 512
TN_MERGE = 512
TM_OPROJ = 416
TM_FFN = 640
TF = 512
CW = 512
TT_CONV = 1024
TT_LRU = 512
TQ_MEM = 2048

NT_DIMS = (((1,), (1,)), ((), ()))


def _params(n_axes, vmem_mib):
    return pltpu.CompilerParams(dimension_semantics=("arbitrary",) * n_axes,
                                vmem_limit_bytes=vmem_mib * MIB)


def _layer_norm(v, g, b):
    mu = jnp.mean(v, axis=-1, keepdims=True)
    vc = v - mu
    var = jnp.mean(vc * vc, axis=-1, keepdims=True)
    return vc * lax.rsqrt(var + LN_EPS) * g + b


def _expm1(x):
    fact = [1.0]
    for k in range(1, 10):
        fact.append(fact[-1] * k)
    p = 1.0 / fact[9]
    for k in range(8, 0, -1):
        p = p * x + 1.0 / fact[k]
    return jnp.where(jnp.abs(x) < 0.5, x * p, jnp.exp(x) - 1.0)


def _cast_rider(w, j, n_steps, step_of):
    rows, cols = w.shape[1:]
    n_blocks = max(n for n in range(1, n_steps + 1)
                   if rows % n == 0 and (rows // n) % BF16_SUBLANE == 0)
    br = rows // n_blocks
    blk = lambda *g: jnp.minimum(step_of(*g), n_blocks - 1)
    return (pl.BlockSpec((None, br, cols), lambda *g: (j, blk(*g), 0)),
            pl.BlockSpec((br, cols), lambda *g: (blk(*g), 0)),
            jax.ShapeDtypeStruct((rows, cols), bf16))


def _mm_kernel(x_ref, w_ref, o_ref, xb_ref):
    @pl.when(pl.program_id(1) == 0)
    def _():
        xb_ref[...] = x_ref[...].astype(bf16)

    o_ref[...] = jnp.dot(xb_ref[...], w_ref[...], preferred_element_type=f32)


def _matmul(x, w_stack, layer, tm, tn, n=None):
    m, k = x.shape
    n = w_stack.shape[-1] if n is None else n
    vmem = 2 * (tm * k * 4 + k * tn * 2 + tm * tn * 4 + tm * k * 2)
    return pl.pallas_call(
        _mm_kernel,
        grid=(m // tm, n // tn),
        in_specs=[pl.BlockSpec((tm, k), lambda i, j: (i, 0)),
                  pl.BlockSpec((None, k, tn), lambda i, j: (layer, 0, j))],
        out_specs=[pl.BlockSpec((tm, tn), lambda i, j: (i, j)),
                   pl.BlockSpec((tm, k), lambda i, j: (i, 0))],
        out_shape=[jax.ShapeDtypeStruct((m, n), f32), jax.ShapeDtypeStruct((m, k), bf16)],
        compiler_params=_params(2, vmem // MIB + 8),
        name="matmul",
    )(x, w_stack)


def _rope_tables(pos):
    half = ROPE_DIM // 2
    inv_freq = ROPE_THETA ** (-jnp.arange(half, dtype=f32) * (2.0 / ROPE_DIM))
    ang = pos.astype(f32)[:, None] * inv_freq[None, :]
    cos = jnp.cos(ang)
    sin = jnp.sin(ang)
    p = pos.shape[0]
    rest = HEAD_DIM - ROPE_DIM
    c = jnp.concatenate([cos, cos, jnp.ones((p, rest), f32)], axis=1)
    sa = jnp.concatenate([-sin, jnp.zeros((p, HEAD_DIM - half), f32)], axis=1)
    sb = jnp.concatenate([jnp.zeros((p, half), f32), sin, jnp.zeros((p, rest), f32)], axis=1)
    rep = LANE // HEAD_DIM
    return jnp.tile(c, (1, rep)), jnp.tile(sa, (1, rep)), jnp.tile(sb, (1, rep))


def _rope(x, c, sa, sb):
    half = ROPE_DIM // 2
    chunks = []
    for j in range(x.shape[1] // LANE):
        xc = x[:, LANE * j:LANE * (j + 1)]
        chunks.append(xc * c + pltpu.roll(xc, LANE - half, 1) * sa + pltpu.roll(xc, half, 1) * sb)
    return chunks[0] if len(chunks) == 1 else jnp.concatenate(chunks, axis=1)


def _sink_softmax(s, sink):
    m = jnp.maximum(jnp.max(s, axis=-1, keepdims=True), sink)
    p = jnp.exp(s - m)
    denom = jnp.sum(p, axis=-1, keepdims=True) + jnp.exp(sink - m)
    return p * (1.0 / denom)


def _attn_block(n, sinks_ref, q_ref, k_ref, v_ref, c_ref, sa_ref, sb_ref, y_ref, kr_ref,
                kprev, vprev, layer, before_head=None):
    @pl.when(n == 0)
    def _():
        kprev[...] = jnp.zeros(kprev.shape, f32)
        vprev[...] = jnp.zeros(vprev.shape, f32)

    c, sa, sb = c_ref[...], sa_ref[...], sb_ref[...]
    q = _rope(q_ref[...], c, sa, sb)
    kc = _rope(k_ref[...], c, sa, sb)
    vc = v_ref[...]
    kr_ref[...] = kc
    kall = jnp.concatenate([kprev[...], kc], axis=0).astype(bf16)
    vall = jnp.concatenate([vprev[...], vc], axis=0).astype(bf16)
    kprev[...] = kc
    vprev[...] = vc

    rows = GQA_GROUP * WINDOW
    qi = lax.broadcasted_iota(jnp.int32, (rows, 2 * WINDOW), 0) % WINDOW
    kj = lax.broadcasted_iota(jnp.int32, (rows, 2 * WINDOW), 1)
    kmin = jnp.where(n > 0, 0, WINDOW)
    mask = (kj > qi) & (kj <= qi + WINDOW) & (kj >= kmin)
    rg = lax.broadcasted_iota(jnp.int32, (rows, 1), 0) // WINDOW
    for h in range(KV_HEADS):
        if before_head is not None:
            before_head(h)
        qh = jnp.concatenate(
            [q[:, (GQA_GROUP * h + g) * HEAD_DIM:(GQA_GROUP * h + g + 1) * HEAD_DIM]
             for g in range(GQA_GROUP)], axis=0).astype(bf16)
        kh = kall[:, HEAD_DIM * h:HEAD_DIM * (h + 1)]
        vh = vall[:, HEAD_DIM * h:HEAD_DIM * (h + 1)]
        s = lax.dot_general(qh, kh, NT_DIMS, preferred_element_type=f32) * (HEAD_DIM ** -0.5)
        s = jnp.where(mask, s, NEG_INF)
        sink = jnp.zeros((rows, 1), f32)
        for g in range(GQA_GROUP):
            sink = jnp.where(rg == g, sinks_ref[layer, GQA_GROUP * h + g], sink)
        pn = _sink_softmax(s, sink).astype(bf16)
        o = jnp.dot(pn, vh, preferred_element_type=f32)
        for g in range(GQA_GROUP):
            hd = GQA_GROUP * h + g
            y_ref[:, hd * HEAD_DIM:(hd + 1) * HEAD_DIM] = o[WINDOW * g:WINDOW * (g + 1), :].astype(bf16)


GA_ROWS = 8
GA_COLS = 8
TM_GA = ROWS // GA_ROWS
TN_GA = N_BRANCH * D_MODEL // GA_COLS


def _gates_attn_kernel(y_init, sinks_ref, xb_ref, w_ref, q_ref, k_ref, v_ref, c_ref, sa_ref, sb_ref,
                       cast_in, zg_ref, y_ref, kr_ref, cast_out, kprev, vprev, *, layer):
    del y_init
    cast_out[...] = cast_in[...].astype(bf16)
    step = pl.program_id(0) * GA_COLS + pl.program_id(1)
    cw = TN_GA // KV_HEADS

    def gate_columns(h):
        zg_ref[:, cw * h:cw * (h + 1)] = jnp.dot(
            xb_ref[...], w_ref[:, cw * h:cw * (h + 1)], preferred_element_type=f32).astype(zg_ref.dtype)

    _attn_block(step % (SEQ // WINDOW), sinks_ref, q_ref, k_ref, v_ref, c_ref, sa_ref, sb_ref,
                y_ref, kr_ref, kprev, vprev, layer, before_head=gate_columns)


def _gates_attn_prompt(y_init, x, w_gate_b, z, sinks, tables, layer, cast_w, cast_j):
    nblk = SEQ // WINDOW
    assert GA_ROWS * GA_COLS == BATCH * nblk
    step = lambda i, j: i * GA_COLS + j
    tab = pl.BlockSpec((WINDOW, LANE), lambda i, j: (step(i, j) % nblk, 0))
    c_in, c_out, c_shape = _cast_rider(cast_w, cast_j, GA_ROWS * GA_COLS, step)
    in_specs = [pl.BlockSpec(memory_space=pl.ANY),
                pl.BlockSpec(memory_space=pltpu.SMEM),
                pl.BlockSpec((TM_GA, D_MODEL), lambda i, j: (i, 0)),
                pl.BlockSpec((None, D_MODEL, TN_GA), lambda i, j: (layer, 0, j)),
                pl.BlockSpec((WINDOW, ATTN_WIDTH), lambda i, j: (step(i, j), OFF_Q // ATTN_WIDTH)),
                pl.BlockSpec((WINDOW, KV_WIDTH), lambda i, j: (step(i, j), OFF_K // KV_WIDTH)),
                pl.BlockSpec((WINDOW, KV_WIDTH), lambda i, j: (step(i, j), OFF_V // KV_WIDTH)),
                tab, tab, tab, c_in]
    out_specs = [pl.BlockSpec((TM_GA, TN_GA), lambda i, j: (i, j)),
                 pl.BlockSpec((WINDOW, ATTN_WIDTH), lambda i, j: (step(i, j), 0)),
                 pl.BlockSpec((WINDOW, KV_WIDTH), lambda i, j: (step(i, j), 0)), c_out]
    out_shape = [jax.ShapeDtypeStruct((ROWS, N_BRANCH * D_MODEL), bf16),
                 jax.ShapeDtypeStruct((ROWS, ATTN_WIDTH), bf16),
                 jax.ShapeDtypeStruct((ROWS_P, KV_WIDTH), f32), c_shape]
    args = [y_init, sinks, x, w_gate_b, z, z, z, *tables, cast_w]
    return pl.pallas_call(
        functools.partial(_gates_attn_kernel, layer=layer),
        grid=(GA_ROWS, GA_COLS),
        in_specs=in_specs,
        out_specs=out_specs,
        out_shape=out_shape,
        scratch_shapes=[pltpu.VMEM((WINDOW, KV_WIDTH), f32), pltpu.VMEM((WINDOW, KV_WIDTH), f32)],
        input_output_aliases={0: 1},
        compiler_params=_params(2, 52),
        name="gates_attn_prompt",
    )(*args)


SB_ATTN = 8
KALL = WINDOW + SUBLANE


def _attn_s_kernel(sinks_ref, z_ref, ck_ref, cv_ref, c_ref, sa_ref, sb_ref, y_ref, nk_ref, nv_ref,
                   kall, vall, qs, *, layer):
    c, sa, sb = c_ref[...], sa_ref[...], sb_ref[...]
    rows = GQA_GROUP * DEC_SEQ
    kall[WINDOW:KALL, :] = jnp.zeros((SUBLANE, KV_WIDTH), f32)
    vall[WINDOW:KALL, :] = jnp.zeros((SUBLANE, KV_WIDTH), f32)
    qt = lax.broadcasted_iota(jnp.int32, (rows, KALL), 0) % DEC_SEQ
    kj = lax.broadcasted_iota(jnp.int32, (rows, KALL), 1)
    mask = jnp.where(kj < WINDOW, kj - qt, qt - (kj - WINDOW) + 1) > 0
    rg = lax.broadcasted_iota(jnp.int32, (rows, 1), 0) // DEC_SEQ
    for bb in range(SB_ATTN):
        zb = z_ref[bb]
        q = _rope(zb[:, OFF_Q:OFF_Q + ATTN_WIDTH], c, sa, sb)
        kn = _rope(zb[:, OFF_K:OFF_K + KV_WIDTH], c, sa, sb)
        vn = zb[:, OFF_V:OFF_V + KV_WIDTH]
        kall[0:WINDOW, :] = ck_ref[bb]
        vall[0:WINDOW, :] = cv_ref[bb]
        kall[WINDOW:WINDOW + DEC_SEQ, :] = kn
        vall[WINDOW:WINDOW + DEC_SEQ, :] = vn
        nk_ref[bb] = kall[DEC_SEQ:DEC_SEQ + WINDOW, :]
        nv_ref[bb] = vall[DEC_SEQ:DEC_SEQ + WINDOW, :]
        for h in range(KV_HEADS):
            for g in range(GQA_GROUP):
                hd = GQA_GROUP * h + g
                qs[DEC_SEQ * g:DEC_SEQ * (g + 1), :] = q[:, hd * HEAD_DIM:(hd + 1) * HEAD_DIM]
            qh = qs[...].astype(bf16)
            kh = kall[:, HEAD_DIM * h:HEAD_DIM * (h + 1)].astype(bf16)
            vh = vall[:, HEAD_DIM * h:HEAD_DIM * (h + 1)].astype(bf16)
            s = lax.dot_general(qh, kh, NT_DIMS, preferred_element_type=f32) * (HEAD_DIM ** -0.5)
            s = jnp.where(mask, s, NEG_INF)
            sink = jnp.zeros((rows, 1), f32)
            for g in range(GQA_GROUP):
                sink = jnp.where(rg == g, sinks_ref[layer, GQA_GROUP * h + g], sink)
            pn = _sink_softmax(s, sink).astype(bf16)
            o = jnp.dot(pn, vh, preferred_element_type=f32)
            for g in range(GQA_GROUP):
                hd = GQA_GROUP * h + g
                y_ref[bb, :, hd * HEAD_DIM:(hd + 1) * HEAD_DIM] = o[DEC_SEQ * g:DEC_SEQ * (g + 1), :]


def _attn_sample(zs3, cache_k, cache_v, sinks, tables, layer):
    qkv = ATTN_WIDTH + 2 * KV_WIDTH
    return pl.pallas_call(
        functools.partial(_attn_s_kernel, layer=layer),
        grid=(DEC_BATCH // SB_ATTN,),
        in_specs=[pl.BlockSpec(memory_space=pltpu.SMEM),
                  pl.BlockSpec((SB_ATTN, DEC_SEQ, qkv), lambda i: (i, 0, 0)),
                  pl.BlockSpec((None, SB_ATTN, WINDOW, KV_WIDTH), lambda i: (layer, i, 0, 0)),
                  pl.BlockSpec((None, SB_ATTN, WINDOW, KV_WIDTH), lambda i: (layer, i, 0, 0)),
                  pl.BlockSpec((DEC_SEQ, LANE), lambda i: (0, 0)),
                  pl.BlockSpec((DEC_SEQ, LANE), lambda i: (0, 0)),
                  pl.BlockSpec((DEC_SEQ, LANE), lambda i: (0, 0))],
        out_specs=[pl.BlockSpec((SB_ATTN, DEC_SEQ, ATTN_WIDTH), lambda i: (i, 0, 0)),
                   pl.BlockSpec((SB_ATTN, WINDOW, KV_WIDTH), lambda i: (i, 0, 0)),
                   pl.BlockSpec((SB_ATTN, WINDOW, KV_WIDTH), lambda i: (i, 0, 0))],
        out_shape=[jax.ShapeDtypeStruct((DEC_BATCH, DEC_SEQ, ATTN_WIDTH), f32),
                   jax.ShapeDtypeStruct((DEC_BATCH, WINDOW, KV_WIDTH), f32),
                   jax.ShapeDtypeStruct((DEC_BATCH, WINDOW, KV_WIDTH), f32)],
        scratch_shapes=[pltpu.VMEM((KALL, KV_WIDTH), f32), pltpu.VMEM((KALL, KV_WIDTH), f32),
                        pltpu.VMEM((GQA_GROUP * DEC_SEQ, HEAD_DIM), f32)],
        compiler_params=_params(1, 32),
        name="attn_sample",
    )(sinks, zs3, cache_k, cache_v, *tables)


def _sconv_p_kernel(y_init, cb_ref, cc_ref, ch_ref, w_ref, y_ref, st_ref, ubuf):
    del y_init
    t = pl.program_id(2)
    tt = cc_ref.shape[0]

    @pl.when(t == 0)
    def _():
        ubuf[0:SUBLANE, :] = jnp.zeros((SUBLANE, CW), f32)

    ubuf[SUBLANE:SUBLANE + tt, :] = cc_ref[...] * ch_ref[...]
    w = w_ref[...]
    yc = ubuf[SUBLANE - 2:SUBLANE - 2 + tt, :] * w[0:1]
    yc = yc + ubuf[SUBLANE - 1:SUBLANE - 1 + tt, :] * w[1:2]
    yc = yc + ubuf[SUBLANE:SUBLANE + tt, :] * w[2:3]
    y_ref[...] = (cb_ref[...] * yc).astype(bf16)
    last = ubuf[tt:tt + SUBLANE, :]
    st_ref[...] = last
    ubuf[0:SUBLANE, :] = last


def _sconv_prompt(y_init, z, w_sconv, layer):
    nt = SEQ // TT_CONV
    zspec = lambda off: pl.BlockSpec((TT_CONV, CW), lambda b, c, t: (b * nt + t, off // CW + c))
    return pl.pallas_call(
        _sconv_p_kernel,
        grid=(BATCH, CONV_WIDTH // CW, nt),
        in_specs=[pl.BlockSpec(memory_space=pl.ANY), zspec(OFF_CB), zspec(OFF_CC), zspec(OFF_CH),
                  pl.BlockSpec((None, SCONV_K, CW), lambda b, c, t: (layer, 0, c))],
        out_specs=[pl.BlockSpec((TT_CONV, CW), lambda b, c, t: (b * nt + t, c)),
                   pl.BlockSpec((None, SUBLANE, CW), lambda b, c, t: (b, 0, c))],
        out_shape=[jax.ShapeDtypeStruct((ROWS, CONV_WIDTH), bf16),
                   jax.ShapeDtypeStruct((BATCH, SUBLANE, CONV_WIDTH), f32)],
        scratch_shapes=[pltpu.VMEM((TT_CONV + SUBLANE, CW), f32)],
        input_output_aliases={0: 0},
        compiler_params=_params(3, 40),
        name="sconv_prompt",
    )(y_init, z, z, z, w_sconv)


LRU_BLOCKS_PER_TILE = CW // LRU_BLOCK_DIM


def _build_block_diag(w_ref, bd_ref):
    bd_ref[...] = jnp.zeros(bd_ref.shape, bf16)
    for n in range(LRU_BLOCKS_PER_TILE):
        lo, hi = LRU_BLOCK_DIM * n, LRU_BLOCK_DIM * (n + 1)
        bd_ref[lo:hi, lo:hi] = w_ref[n].astype(bf16)


def _lru_gates(xc, wa_bd, wx_bd, ba, bx, lam):
    xcb = xc.astype(bf16)
    r = jax.nn.sigmoid(jnp.dot(xcb, wa_bd, preferred_element_type=f32) + ba)
    i = jax.nn.sigmoid(jnp.dot(xcb, wx_bd, preferred_element_type=f32) + bx)
    log_a = -LRU_C * r * jax.nn.softplus(-lam)
    a = jnp.exp(log_a)
    mult = jnp.sqrt(-_expm1(2.0 * log_a))
    return a, mult * (i * xc)


def _lru_p_kernel(y_init, xl_ref, wc_ref, bc_ref, wa_ref, ba_ref, wx_ref, bx_ref, lam_ref,
                  y_ref, h_ref, xbuf, wa_bd, wx_bd, a_s, b_s, hcar):
    del y_init
    t = pl.program_id(2)
    tt = xl_ref.shape[0]

    @pl.when(t == 0)
    def _():
        xbuf[0:SUBLANE, :] = jnp.zeros((SUBLANE, CW), f32)
        hcar[...] = jnp.zeros(hcar.shape, f32)
        _build_block_diag(wa_ref, wa_bd)
        _build_block_diag(wx_ref, wx_bd)

    xbuf[SUBLANE:SUBLANE + tt, :] = xl_ref[...]
    w = wc_ref[...]
    xc = xbuf[SUBLANE - 3:SUBLANE - 3 + tt, :] * w[0:1]
    for j in range(1, LRU_CONV_K):
        xc = xc + xbuf[SUBLANE - 3 + j:SUBLANE - 3 + j + tt, :] * w[j:j + 1]
    xc = xc + bc_ref[...]
    xbuf[0:SUBLANE, :] = xbuf[tt:tt + SUBLANE, :]

    a, bx = _lru_gates(xc, wa_bd[...], wx_bd[...], ba_ref[...], bx_ref[...], lam_ref[...])

    r8 = lax.broadcasted_iota(jnp.int32, (tt, CW), 0) % SUBLANE
    for s in (1, 2, 4):
        keep = r8 >= s
        a_sh = jnp.where(keep, pltpu.roll(a, s, 0), 1.0)
        b_sh = jnp.where(keep, pltpu.roll(bx, s, 0), 0.0)
        bx = bx + a * b_sh
        a = a * a_sh
    a_s[...] = a
    b_s[...] = bx

    def tile_step(j, h):
        r0 = pl.multiple_of(j * SUBLANE, SUBLANE)
        ht = b_s[pl.ds(r0, SUBLANE), :] + a_s[pl.ds(r0, SUBLANE), :] * h
        b_s[pl.ds(r0, SUBLANE), :] = ht
        return ht[SUBLANE - 1:SUBLANE, :]

    h_last = lax.fori_loop(0, tt // SUBLANE, tile_step, hcar[...])
    hcar[...] = h_last
    y_ref[...] = b_s[...].astype(bf16)
    h_ref[...] = b_s[tt - SUBLANE:tt, :]


def _lru_specs(layer, idx):
    return [pl.BlockSpec((None, LRU_CONV_K, CW), lambda *g: (layer, 0, idx(*g))),
            pl.BlockSpec((None, 1, CW), lambda *g: (layer, 0, idx(*g))),
            pl.BlockSpec((None, LRU_BLOCKS_PER_TILE, LRU_BLOCK_DIM, LRU_BLOCK_DIM),
                         lambda *g: (layer, idx(*g), 0, 0)),
            pl.BlockSpec((None, 1, CW), lambda *g: (layer, 0, idx(*g))),
            pl.BlockSpec((None, LRU_BLOCKS_PER_TILE, LRU_BLOCK_DIM, LRU_BLOCK_DIM),
                         lambda *g: (layer, idx(*g), 0, 0)),
            pl.BlockSpec((None, 1, CW), lambda *g: (layer, 0, idx(*g))),
            pl.BlockSpec((None, 1, CW), lambda *g: (layer, 0, idx(*g)))]


def _lru_prompt(y_init, z, lru_w, layer):
    nt = SEQ // TT_LRU
    return pl.pallas_call(
        _lru_p_kernel,
        grid=(BATCH, LRU_WIDTH // CW, nt),
        in_specs=[pl.BlockSpec(memory_space=pl.ANY),
                  pl.BlockSpec((TT_LRU, CW), lambda b, c, t: (b * nt + t, OFF_XL // CW + c))]
                 + _lru_specs(layer, lambda b, c, t: c),
        out_specs=[pl.BlockSpec((TT_LRU, CW), lambda b, c, t: (b * nt + t, c)),
                   pl.BlockSpec((None, SUBLANE, CW), lambda b, c, t: (b, 0, c))],
        out_shape=[jax.ShapeDtypeStruct((ROWS, LRU_WIDTH), bf16),
                   jax.ShapeDtypeStruct((BATCH, SUBLANE, LRU_WIDTH), f32)],
        input_output_aliases={0: 0},
        scratch_shapes=[pltpu.VMEM((TT_LRU + SUBLANE, CW), f32),
                        pltpu.VMEM((CW, CW), bf16), pltpu.VMEM((CW, CW), bf16),
                        pltpu.VMEM((TT_LRU, CW), f32), pltpu.VMEM((TT_LRU, CW), f32),
                        pltpu.VMEM((1, CW), f32)],
        compiler_params=_params(3, 40),
        name="lru_prompt",
    )(y_init, z, *lru_w)


def _seq_s_kernel(cb_ref, cc_ref, ch_ref, xl_ref, sc_ref, lc_ref, h0_ref, wsc_ref,
                  wc_ref, bc_ref, wa_ref, ba_ref, wx_ref, bx_ref, lam_ref,
                  yc_ref, yl_ref, scn_ref, hn_ref, wa_bd, wx_bd):
    nb = DEC_BATCH
    rows = lambda v, t: v[nb * t:nb * (t + 1), :]
    u = cc_ref[...] * ch_ref[...]
    cb = cb_ref[...]
    up = [sc_ref[j] for j in range(SCONV_K - 1)] + [rows(u, t) for t in range(DEC_SEQ)]
    w = wsc_ref[...]
    for t in range(DEC_SEQ):
        yc = up[t] * w[0:1]
        for j in range(1, SCONV_K):
            yc = yc + up[t + j] * w[j:j + 1]
        yc_ref[nb * t:nb * (t + 1), :] = (rows(cb, t) * yc).astype(bf16)
    for j in range(SCONV_K - 1):
        scn_ref[j] = up[DEC_SEQ + j]

    _build_block_diag(wa_ref, wa_bd)
    _build_block_diag(wx_ref, wx_bd)
    xl = xl_ref[...]
    xp = [lc_ref[j] for j in range(LRU_CONV_K - 1)] + [rows(xl, t) for t in range(DEC_SEQ)]
    wl = wc_ref[...]
    xcs = []
    for t in range(DEC_SEQ):
        xc = xp[t] * wl[0:1]
        for j in range(1, LRU_CONV_K):
            xc = xc + xp[t + j] * wl[j:j + 1]
        xcs.append(xc + bc_ref[...])
    xc = jnp.concatenate(xcs, axis=0)
    a, bx = _lru_gates(xc, wa_bd[...], wx_bd[...], ba_ref[...], bx_ref[...], lam_ref[...])
    h = h0_ref[...]
    for t in range(DEC_SEQ):
        h = rows(a, t) * h + rows(bx, t)
        yl_ref[nb * t:nb * (t + 1), :] = h.astype(bf16)
    hn_ref[...] = h


def _seq_sample(z, st_sconv_t, st_lconv_t, st_h, w_sconv, lru_w, layer):
    rblk = ROWS_P // ROWS_S
    zspec = lambda off: pl.BlockSpec((ROWS_S, CW), lambda c: (rblk, off // CW + c))
    return pl.pallas_call(
        _seq_s_kernel,
        grid=(CONV_WIDTH // CW,),
        in_specs=[zspec(OFF_CB), zspec(OFF_CC), zspec(OFF_CH), zspec(OFF_XL),
                  pl.BlockSpec((None, SCONV_K - 1, DEC_BATCH, CW), lambda c: (layer, 0, 0, c)),
                  pl.BlockSpec((None, LRU_CONV_K - 1, DEC_BATCH, CW), lambda c: (layer, 0, 0, c)),
                  pl.BlockSpec((None, DEC_BATCH, CW), lambda c: (layer, 0, c)),
                  pl.BlockSpec((None, SCONV_K, CW), lambda c: (layer, 0, c))]
                 + _lru_specs(layer, lambda c: c),
        out_specs=[pl.BlockSpec((ROWS_S, CW), lambda c: (0, c)),
                   pl.BlockSpec((ROWS_S, CW), lambda c: (0, c)),
                   pl.BlockSpec((SCONV_K - 1, DEC_BATCH, CW), lambda c: (0, 0, c)),
                   pl.BlockSpec((DEC_BATCH, CW), lambda c: (0, c))],
        out_shape=[jax.ShapeDtypeStruct((ROWS_S, CONV_WIDTH), bf16),
                   jax.ShapeDtypeStruct((ROWS_S, LRU_WIDTH), bf16),
                   jax.ShapeDtypeStruct((SCONV_K - 1, DEC_BATCH, CONV_WIDTH), f32),
                   jax.ShapeDtypeStruct((DEC_BATCH, LRU_WIDTH), f32)],
        scratch_shapes=[pltpu.VMEM((CW, CW), bf16), pltpu.VMEM((CW, CW), bf16)],
        compiler_params=_params(1, 32),
        name="seq_sample",
    )(z, z, z, z, st_sconv_t, st_lconv_t, st_h, w_sconv, *lru_w)


def _softmax_rows(s):
    m = jnp.max(s, axis=-1, keepdims=True)
    p = jnp.exp(s - m)
    return p * (1.0 / jnp.sum(p, axis=-1, keepdims=True))


def _mem_p_kernel(y_init, q_ref, k_ref, v_ref, y_ref):
    del y_init
    s = lax.dot_general(q_ref[...].astype(bf16), k_ref[...].astype(bf16), NT_DIMS,
                        preferred_element_type=f32) * (MEM_HEAD_DIM ** -0.5)
    pn = _softmax_rows(s).astype(bf16)
    y_ref[...] = jnp.dot(pn, v_ref[...].astype(bf16), preferred_element_type=f32).astype(bf16)


def _mem_prompt(y_init, z, mkv):
    nt = SEQ // TQ_MEM
    hw = MEM_HEAD_DIM
    return pl.pallas_call(
        _mem_p_kernel,
        grid=(BATCH, MEM_HEADS, nt),
        in_specs=[pl.BlockSpec(memory_space=pl.ANY),
                  pl.BlockSpec((TQ_MEM, hw), lambda b, h, t: (b * nt + t, OFF_QM // hw + h)),
                  pl.BlockSpec((MEM_LEN, hw), lambda b, h, t: (b, h)),
                  pl.BlockSpec((MEM_LEN, hw), lambda b, h, t: (b, MEM_HEADS + h))],
        out_specs=pl.BlockSpec((TQ_MEM, hw), lambda b, h, t: (b * nt + t, h)),
        out_shape=jax.ShapeDtypeStruct((ROWS, MEM_WIDTH), bf16),
        input_output_aliases={0: 0},
        compiler_params=_params(3, 32),
        name="mem_prompt",
    )(y_init, z, mkv, mkv)


SB_MEM = 4


QEXP = MEM_HEADS * DEC_SEQ


def _mem_s_kernel(q_ref, k_ref, v_ref, y_ref, qexp):
    hw = MEM_HEAD_DIM
    qexp[...] = jnp.zeros(qexp.shape, f32)
    for bb in range(SB_MEM):
        qb = q_ref[bb]
        for h in range(MEM_HEADS):
            qexp[DEC_SEQ * h:DEC_SEQ * (h + 1), hw * h:hw * (h + 1)] = qb[:, hw * h:hw * (h + 1)]
        s = lax.dot_general(qexp[...].astype(bf16), k_ref[bb].astype(bf16), NT_DIMS,
                            preferred_element_type=f32) * (MEM_HEAD_DIM ** -0.5)
        pn = _softmax_rows(s).astype(bf16)
        o = jnp.dot(pn, v_ref[bb].astype(bf16), preferred_element_type=f32)
        for h in range(MEM_HEADS):
            y_ref[bb, :, hw * h:hw * (h + 1)] = o[DEC_SEQ * h:DEC_SEQ * (h + 1), hw * h:hw * (h + 1)]


def _mem_sample(qm3, cache_k, cache_v, layer):
    cache_spec = pl.BlockSpec((None, SB_MEM, MEM_LEN, MEM_WIDTH), lambda i: (layer, i, 0, 0))
    return pl.pallas_call(
        _mem_s_kernel,
        grid=(DEC_BATCH // SB_MEM,),
        in_specs=[pl.BlockSpec((SB_MEM, DEC_SEQ, MEM_WIDTH), lambda i: (i, 0, 0)),
                  cache_spec, cache_spec],
        out_specs=pl.BlockSpec((SB_MEM, DEC_SEQ, MEM_WIDTH), lambda i: (i, 0, 0)),
        out_shape=jax.ShapeDtypeStruct((DEC_BATCH, DEC_SEQ, MEM_WIDTH), f32),
        scratch_shapes=[pltpu.VMEM((QEXP, MEM_WIDTH), f32)],
        compiler_params=_params(1, 40),
        name="mem_sample",
    )(qm3, cache_k, cache_v)


def _merge_kernel(y0, y1, y2, y3, g0, g1, g2, g3, wb_ref, o_ref):
    acc = None
    for b, (y, g) in enumerate(((y0, g0), (y1, g1), (y2, g2), (y3, g3))):
        proj = jnp.dot(y[...], wb_ref[b], preferred_element_type=f32)
        term = jax.nn.sigmoid(g[...].astype(f32)) * proj
        acc = term if acc is None else acc + term
    o_ref[...] = acc.astype(bf16)


def _merge(ys, zg, w_branch, layer):
    tn = TN_MERGE
    yspec = pl.BlockSpec((TM, BRANCH_WIDTH), lambda i, j: (i, 0))
    gspec = lambda b: pl.BlockSpec((TM, tn), lambda i, j: (i, (b * D_MODEL) // tn + j))
    return pl.pallas_call(
        _merge_kernel,
        grid=(zg.shape[0] // TM, D_MODEL // tn),
        in_specs=[yspec] * N_BRANCH + [gspec(b) for b in range(N_BRANCH)]
                 + [pl.BlockSpec((None, N_BRANCH, BRANCH_WIDTH, tn), lambda i, j: (layer, 0, 0, j))],
        out_specs=pl.BlockSpec((TM, tn), lambda i, j: (i, j)),
        out_shape=jax.ShapeDtypeStruct((zg.shape[0], D_MODEL), bf16),
        compiler_params=_params(2, 48),
        name="merge",
    )(*ys, zg, zg, zg, zg, w_branch)


def _oproj_kernel(m_ref, w_ref, x_ref, g_ref, b_ref, o_ref):
    d = jnp.dot(m_ref[...], w_ref[...], preferred_element_type=f32)
    o_ref[...] = _layer_norm(ALPHA * x_ref[...] + d, g_ref[0:1, :], b_ref[0:1, :])


def _oproj(merged, w_o, x, ln_g, ln_b, layer):
    tm = TM_OPROJ
    return pl.pallas_call(
        _oproj_kernel,
        grid=(x.shape[0] // tm,),
        in_specs=[pl.BlockSpec((tm, D_MODEL), lambda i: (i, 0)),
                  pl.BlockSpec((None, D_MODEL, D_MODEL), lambda i: (layer, 0, 0)),
                  pl.BlockSpec((tm, D_MODEL), lambda i: (i, 0)),
                  pl.BlockSpec((None, 2, D_MODEL), lambda i: (layer, 0, 0)),
                  pl.BlockSpec((None, 2, D_MODEL), lambda i: (layer, 0, 0))],
        out_specs=pl.BlockSpec((tm, D_MODEL), lambda i: (i, 0)),
        out_shape=jax.ShapeDtypeStruct(x.shape, f32),
        compiler_params=_params(1, 48),
        name="oproj_ln",
    )(merged, w_o, x, ln_g, ln_b)


def _swiglu_partial(xb, wg_ref, wu_ref, wd_ref):
    g = jnp.dot(xb, wg_ref[...], preferred_element_type=f32)
    u = jnp.dot(xb, wu_ref[...], preferred_element_type=f32)
    h = (jax.nn.silu(g) * u).astype(bf16)
    return jnp.dot(h, wd_ref[...], preferred_element_type=f32)


def _ffn_kernel(x_ref, wg_ref, wu_ref, wd_ref, g_ref, b_ref, cast_in, o_ref, cast_out, xb_ref):
    f = pl.program_id(1)
    cast_out[...] = cast_in[...].astype(bf16)

    @pl.when(f == 0)
    def _():
        xb_ref[...] = x_ref[...].astype(bf16)
        o_ref[...] = jnp.zeros(o_ref.shape, f32)

    o_ref[...] += _swiglu_partial(xb_ref[...], wg_ref, wu_ref, wd_ref)

    @pl.when(f == pl.num_programs(1) - 1)
    def _():
        o_ref[...] = _layer_norm(ALPHA * x_ref[...] + o_ref[...], g_ref[1:2, :], b_ref[1:2, :])


def _ffn(x, wg, wu, wd, ln_g, ln_b, layer, j, cast_w, cast_j):
    tm = TM_FFN
    n_f = D_FF // TF
    w_in = pl.BlockSpec((None, D_MODEL, TF), lambda i, f: (j, 0, f))
    w_out = pl.BlockSpec((None, TF, D_MODEL), lambda i, f: (j, f, 0))
    xspec = pl.BlockSpec((tm, D_MODEL), lambda i, f: (i, 0))
    x_once = pl.BlockSpec((tm, D_MODEL), lambda i, f: (i, 0), pipeline_mode=pl.Buffered(1))
    lnspec = pl.BlockSpec((None, 2, D_MODEL), lambda i, f: (layer, 0, 0))
    n_steps = (x.shape[0] // tm) * n_f
    c_in, c_out, c_shape = _cast_rider(cast_w, cast_j, n_steps, lambda i, f: i * n_f + f)
    return pl.pallas_call(
        _ffn_kernel,
        grid=(x.shape[0] // tm, n_f),
        in_specs=[x_once, w_in, w_in, w_out, lnspec, lnspec, c_in],
        out_specs=[xspec, c_out],
        out_shape=[jax.ShapeDtypeStruct(x.shape, f32), c_shape],
        scratch_shapes=[pltpu.VMEM((tm, D_MODEL), bf16)],
        compiler_params=_params(2, 54),
        name="ffn_dense",
    )(x, wg, wu, wd, ln_g, ln_b, cast_w)


TG = 512


def _max_row_tiles(n_tokens):
    return (n_tokens * TOP_K + N_EXPERTS * (TG - 1)) // TG


def _router_kernel(x_ref, w_ref, p_ref, i_ref, cnt_ref, carry):
    @pl.when(pl.program_id(0) == 0)
    def _():
        carry[...] = jnp.zeros(carry.shape, f32)

    tm = x_ref.shape[0]
    x = x_ref[...]
    w = w_ref[...]
    xh = x.astype(bf16)
    wh = w.astype(bf16)
    xl = (x - xh.astype(f32)).astype(bf16)
    wl = (w - wh.astype(f32)).astype(bf16)
    logits = (jnp.dot(xh, wh, preferred_element_type=f32) + jnp.dot(xl, wh, preferred_element_type=f32)
              + jnp.dot(xh, wl, preferred_element_type=f32))
    lane = lax.broadcasted_iota(jnp.int32, logits.shape, 1)
    logits = jnp.where(lane < N_EXPERTS, logits, -jnp.inf)
    m1 = jnp.max(logits, axis=-1, keepdims=True)
    i1 = jnp.min(jnp.where(logits == m1, lane, LANE), axis=-1, keepdims=True)
    rest = jnp.where(lane == i1, -jnp.inf, logits)
    m2 = jnp.max(rest, axis=-1, keepdims=True)
    i2 = jnp.min(jnp.where(rest == m2, lane, LANE), axis=-1, keepdims=True)
    e = jnp.exp(m2 - m1)
    p1 = 1.0 / (1.0 + e)
    p2 = e / (1.0 + e)
    sel = jnp.where(lane == i1, 1.0, jnp.where(lane == i2, 1.0, 0.0))
    r_i = lax.broadcasted_iota(jnp.int32, (tm, tm), 0)
    c_i = lax.broadcasted_iota(jnp.int32, (tm, tm), 1)
    tri = jnp.where(r_i > c_i, 1.0, 0.0).astype(bf16)
    before = jnp.dot(tri, sel.astype(bf16), preferred_element_type=f32) + carry[...]
    r1 = jnp.sum(jnp.where(lane == i1, before, 0.0), axis=-1, keepdims=True).astype(jnp.int32)
    r2 = jnp.sum(jnp.where(lane == i2, before, 0.0), axis=-1, keepdims=True).astype(jnp.int32)
    carry[...] = carry[...] + jnp.sum(sel, axis=0, keepdims=True)
    cnt_ref[...] = carry[...]
    p_ref[...] = jnp.where(lane == 0, p1, jnp.where(lane == 1, p2, 0.0))
    i_ref[...] = jnp.where(lane == 0, i1, jnp.where(lane == 1, i2,
                           jnp.where(lane == 2, r1, jnp.where(lane == 3, r2, 0))))


def _router(x, w_router_pad, j):
    n = x.shape[0]
    tm = TM_FFN
    return pl.pallas_call(
        _router_kernel,
        grid=(n // tm,),
        in_specs=[pl.BlockSpec((tm, D_MODEL), lambda i: (i, 0)),
                  pl.BlockSpec((None, D_MODEL, LANE), lambda i: (j, 0, 0))],
        out_specs=[pl.BlockSpec((tm, LANE), lambda i: (i, 0)),
                   pl.BlockSpec((tm, LANE), lambda i: (i, 0)),
                   pl.BlockSpec((1, LANE), lambda i: (0, 0))],
        out_shape=[jax.ShapeDtypeStruct((n, LANE), f32),
                   jax.ShapeDtypeStruct((n, LANE), jnp.int32),
                   jax.ShapeDtypeStruct((1, LANE), f32)],
        scratch_shapes=[pltpu.VMEM((1, LANE), f32)],
        compiler_params=_params(1, 40),
        name="router",
    )(x, w_router_pad)


def _routing_plan(iinfo, counts):
    cnt = counts[0, :N_EXPERTS].astype(jnp.int32)
    tiles = (cnt + (TG - 1)) // TG
    end_tile = jnp.cumsum(tiles)
    offset = (end_tile - tiles) * TG
    pos1 = offset[iinfo[:, 0]] + iinfo[:, 2]
    pos2 = offset[iinfo[:, 1]] + iinfo[:, 3]
    tile_ids = jnp.arange(_max_row_tiles(iinfo.shape[0]), dtype=jnp.int32)
    tile_expert = jnp.minimum(jnp.sum(tile_ids[:, None] >= end_tile[None, :], axis=1),
                              N_EXPERTS - 1).astype(jnp.int32)
    return pos1, pos2, tile_expert, end_tile[N_EXPERTS - 1:].astype(jnp.int32)


DMA_UNROLL = 8


def _row_copy(src, s, dst, d, sem):
    return pltpu.make_async_copy(src.at[pl.ds(s, 1)], dst.at[pl.ds(d, 1)], sem)


def _dispatch_kernel(pos1_ref, pos2_ref, x_ref, xs_in, xs_ref, sem):
    del xs_in
    tm = x_ref.shape[0]
    base = pl.program_id(0) * tm

    def issue(r, c):
        _row_copy(x_ref, r, xs_ref, pos1_ref[base + r], sem).start()
        _row_copy(x_ref, r, xs_ref, pos2_ref[base + r], sem).start()
        return c

    def drain(r, c):
        _row_copy(x_ref, r, xs_ref, pos1_ref[base + r], sem).wait()
        _row_copy(x_ref, r, xs_ref, pos2_ref[base + r], sem).wait()
        return c

    lax.fori_loop(0, tm, issue, 0, unroll=DMA_UNROLL)
    lax.fori_loop(0, tm, drain, 0, unroll=DMA_UNROLL)


def _dispatch(x, pos1, pos2):
    tm = TM_FFN
    n_rows = _max_row_tiles(x.shape[0]) * TG
    xs0 = jnp.zeros((n_rows, D_MODEL), f32)
    return pl.pallas_call(
        _dispatch_kernel,
        grid_spec=pltpu.PrefetchScalarGridSpec(
            num_scalar_prefetch=2,
            grid=(x.shape[0] // tm,),
            in_specs=[pl.BlockSpec((tm, D_MODEL), lambda i, p1, p2: (i, 0)),
                      pl.BlockSpec(memory_space=pl.ANY)],
            out_specs=pl.BlockSpec(memory_space=pl.ANY),
            scratch_shapes=[pltpu.SemaphoreType.DMA(())]),
        out_shape=jax.ShapeDtypeStruct((n_rows, D_MODEL), f32),
        input_output_aliases={3: 0},
        compiler_params=_params(1, 32),
        name="moe_dispatch",
    )(pos1, pos2, x, xs0)


def _gffn_kernel(te_ref, nt_ref, x_ref, wg_ref, wu_ref, wd_ref, o_ref, xb_ref):
    del te_ref
    f = pl.program_id(1)
    used = pl.program_id(0) < nt_ref[0]

    @pl.when(jnp.logical_and(jnp.logical_not(used), f == 0))
    def _():
        o_ref[...] = jnp.zeros(o_ref.shape, f32)

    @pl.when(jnp.logical_and(used, f == 0))
    def _():
        xb_ref[...] = x_ref[...].astype(bf16)
        o_ref[...] = jnp.zeros(o_ref.shape, f32)

    @pl.when(used)
    def _():
        o_ref[...] += _swiglu_partial(xb_ref[...], wg_ref, wu_ref, wd_ref)


def _grouped_ffn(xs, tile_expert, n_tiles, wg, wu, wd):
    n_f = D_FF // TF
    row = lambda i, nt: jnp.minimum(i, nt[0] - 1)
    fcol = lambda i, f, nt: jnp.where(i < nt[0], f, n_f - 1)
    xspec = pl.BlockSpec((TG, D_MODEL), lambda i, f, te, nt: (row(i, nt), 0))
    w_in = pl.BlockSpec((None, D_MODEL, TF),
                        lambda i, f, te, nt: (te[row(i, nt)], 0, fcol(i, f, nt)))
    w_out = pl.BlockSpec((None, TF, D_MODEL),
                         lambda i, f, te, nt: (te[row(i, nt)], fcol(i, f, nt), 0))
    return pl.pallas_call(
        _gffn_kernel,
        grid_spec=pltpu.PrefetchScalarGridSpec(
            num_scalar_prefetch=2,
            grid=(xs.shape[0] // TG, n_f),
            in_specs=[xspec, w_in, w_in, w_out],
            out_specs=pl.BlockSpec((TG, D_MODEL), lambda i, f, te, nt: (i, 0)),
            scratch_shapes=[pltpu.VMEM((TG, D_MODEL), bf16)]),
        out_shape=jax.ShapeDtypeStruct(xs.shape, f32),
        compiler_params=_params(2, 52),
        name="ffn_experts",
    )(tile_expert, n_tiles, xs, wg, wu, wd)


def _combine_kernel(pos1_ref, pos2_ref, x_ref, p_ref, y_ref, g_ref, b_ref, o_ref, buf1, buf2, sem):
    tm = x_ref.shape[0]
    base = pl.program_id(0) * tm

    def issue(r, c):
        _row_copy(y_ref, pos1_ref[base + r], buf1, r, sem).start()
        _row_copy(y_ref, pos2_ref[base + r], buf2, r, sem).start()
        return c

    def drain(r, c):
        _row_copy(y_ref, pos1_ref[base + r], buf1, r, sem).wait()
        _row_copy(y_ref, pos2_ref[base + r], buf2, r, sem).wait()
        return c

    lax.fori_loop(0, tm, issue, 0, unroll=DMA_UNROLL)
    lax.fori_loop(0, tm, drain, 0, unroll=DMA_UNROLL)
    p = p_ref[...]
    mixed = p[:, 0:1] * buf1[...] + p[:, 1:2] * buf2[...]
    o_ref[...] = _layer_norm(ALPHA * x_ref[...] + mixed, g_ref[1:2, :], b_ref[1:2, :])


def _combine(x, pinfo, y, pos1, pos2, ln_g, ln_b, layer):
    tm = TM_FFN
    xspec = pl.BlockSpec((tm, D_MODEL), lambda i, p1, p2: (i, 0))
    lnspec = pl.BlockSpec((None, 2, D_MODEL), lambda i, p1, p2: (layer, 0, 0))
    return pl.pallas_call(
        _combine_kernel,
        grid_spec=pltpu.PrefetchScalarGridSpec(
            num_scalar_prefetch=2,
            grid=(x.shape[0] // tm,),
            in_specs=[xspec, pl.BlockSpec((tm, LANE), lambda i, p1, p2: (i, 0)),
                      pl.BlockSpec(memory_space=pl.ANY), lnspec, lnspec],
            out_specs=xspec,
            scratch_shapes=[pltpu.VMEM((tm, D_MODEL), f32), pltpu.VMEM((tm, D_MODEL), f32),
                            pltpu.SemaphoreType.DMA(())]),
        out_shape=jax.ShapeDtypeStruct(x.shape, f32),
        compiler_params=_params(1, 48),
        name="moe_combine",
    )(pos1, pos2, x, pinfo, y, ln_g, ln_b)


def _moe(x, w_router_pad, wg, wu, wd, ln_g, ln_b, layer, j):
    pinfo, iinfo, counts = _router(x, w_router_pad, j)
    pos1, pos2, tile_expert, n_tiles = _routing_plan(iinfo, counts)
    xs = _dispatch(x, pos1, pos2)
    y = _grouped_ffn(xs, tile_expert, n_tiles, wg, wu, wd)
    return _combine(x, pinfo, y, pos1, pos2, ln_g, ln_b, layer)


def kernel(x_prompt, x_sample, mem_prompt, cache_win_k, cache_win_v, cache_mem_k, cache_mem_v,
           state_sconv, state_lru_conv, state_lru_h, w_mix_in, sinks, w_sconv, w_lru_conv, b_lru_conv,
           w_lru_a, b_lru_a, w_lru_x, b_lru_x, lru_lambda, w_mem_kv, w_branch, w_o, ln_g, ln_b,
           w_router, w_ffn_gate, w_ffn_up, w_ffn_down, w_exp_gate, w_exp_up, w_exp_down):
    x = jnp.concatenate([x_prompt.reshape(ROWS_P, D_MODEL),
                         x_sample.transpose(1, 0, 2).reshape(ROWS_S, D_MODEL)], axis=0)
    mem = mem_prompt.reshape(BATCH * MEM_LEN, D_MODEL)
    w_in_b = w_mix_in.astype(bf16)
    w_gate_b = w_in_b[:, :, OFF_G:]
    w_mkv_b = w_mem_kv.astype(bf16)
    w_br_b = w_branch.astype(bf16)
    w_o_b = w_o.astype(bf16)
    wfg, wfu, wfd = w_ffn_gate.astype(bf16), w_ffn_up.astype(bf16), w_ffn_down.astype(bf16)
    n_moe = w_exp_gate.shape[0]
    weg_rows = w_exp_gate.reshape(n_moe, N_EXPERTS * D_MODEL, D_FF)
    weu_rows = w_exp_up.reshape(n_moe, N_EXPERTS * D_MODEL, D_FF)
    wed_rows = w_exp_down.reshape(n_moe, N_EXPERTS * D_FF, D_MODEL)
    w_router_pad = jnp.pad(w_router, ((0, 0), (0, 0), (0, LANE - N_EXPERTS)))
    ck = cache_win_k.reshape(DEPTH, DEC_BATCH, WINDOW, KV_WIDTH)
    cv = cache_win_v.reshape(DEPTH, DEC_BATCH, WINDOW, KV_WIDTH)
    cmk = cache_mem_k.reshape(DEPTH, DEC_BATCH, MEM_LEN, MEM_WIDTH)
    cmv = cache_mem_v.reshape(DEPTH, DEC_BATCH, MEM_LEN, MEM_WIDTH)
    st_sc_t = state_sconv.transpose(0, 2, 1, 3)
    st_lc_t = state_lru_conv.transpose(0, 2, 1, 3)
    lru_w = (w_lru_conv, b_lru_conv.reshape(DEPTH, 1, LRU_WIDTH), w_lru_a,
             b_lru_a.reshape(DEPTH, 1, LRU_WIDTH), w_lru_x, b_lru_x.reshape(DEPTH, 1, LRU_WIDTH),
             lru_lambda.reshape(DEPTH, 1, LRU_WIDTH))
    tab_p = _rope_tables(jnp.arange(SEQ, dtype=jnp.int32))
    tab_s = _rope_tables(PAST_LEN + jnp.arange(DEC_SEQ, dtype=jnp.int32))

    def to_b_major(v):
        return v.reshape(DEC_SEQ, DEC_BATCH, v.shape[-1]).transpose(1, 0, 2)

    def to_t_major(v):
        return v.transpose(1, 0, 2).reshape(ROWS_S, v.shape[-1])

    outs = [[] for _ in range(12)]
    for l in range(DEPTH):
        z, x_b = _matmul(x, w_in_b, l, TM, TN_MIX, n=OFF_G)
        mkv, _ = _matmul(mem, w_mkv_b, l, BATCH * MEM_LEN, TN_MKV)
        zs = z[ROWS_P:]

        def all_rows(y_s):
            return lax.dynamic_update_slice(jnp.zeros((ROWS, y_s.shape[-1]), bf16), y_s, (ROWS_P, 0))

        ya_s, nk_s, nv_s = _attn_sample(to_b_major(zs[:, :OFF_CB]), ck, cv, sinks, tab_s, l)
        yc_s, yl_s, sc_s, h_s = _seq_sample(z, st_sc_t, st_lc_t, state_lru_h, w_sconv, lru_w, l)
        ym_s = _mem_sample(to_b_major(zs[:, OFF_QM:OFF_G]), cmk, cmv, l)
        j = l // 2
        zg, ya, krot_p, w_cast = _gates_attn_prompt(
            all_rows(to_t_major(ya_s).astype(bf16)), x_b, w_gate_b, z, sinks, tab_p, l,
            cast_w=wed_rows if l % 2 == 0 else weu_rows, cast_j=j)
        yc, sc_p = _sconv_prompt(all_rows(yc_s), z, w_sconv, l)
        yl, h_p = _lru_prompt(all_rows(yl_s), z, lru_w, l)
        ym = _mem_prompt(all_rows(to_t_major(ym_s).astype(bf16)), z, mkv)
        merged = _merge((ya, yc, yl, ym), zg, w_br_b, l)
        x = _oproj(merged, w_o_b, x, ln_g, ln_b, l)
        if l % 2 == 0:
            x, weg_j = _ffn(x, wfg, wfu, wfd, ln_g, ln_b, l, j, weg_rows, j)
            wed_j = w_cast
        else:
            x = _moe(x, w_router_pad, weg_j.reshape(N_EXPERTS, D_MODEL, D_FF),
                     w_cast.reshape(N_EXPERTS, D_MODEL, D_FF),
                     wed_j.reshape(N_EXPERTS, D_FF, D_MODEL), ln_g, ln_b, l, j)

        def tail(nrows, lo, hi):
            return jnp.stack([z[(b + 1) * SEQ - nrows:(b + 1) * SEQ, lo:hi] for b in range(BATCH)])

        kv_shape = (BATCH, WINDOW, KV_HEADS, HEAD_DIM)
        outs[0].append(krot_p.reshape(BATCH, SEQ, KV_WIDTH)[:, SEQ - WINDOW:].reshape(kv_shape))
        outs[1].append(tail(WINDOW, OFF_V, OFF_CB).reshape(kv_shape))
        mem_shape = (BATCH, MEM_LEN, MEM_HEADS, MEM_HEAD_DIM)
        outs[2].append(mkv[:, :MEM_WIDTH].reshape(mem_shape))
        outs[3].append(mkv[:, MEM_WIDTH:].reshape(mem_shape))
        outs[4].append(sc_p[:, SUBLANE - (SCONV_K - 1):])
        outs[5].append(tail(LRU_CONV_K - 1, OFF_XL, OFF_QM))
        outs[6].append(h_p[:, SUBLANE - 1])
        kvs_shape = (DEC_BATCH, WINDOW, KV_HEADS, HEAD_DIM)
        outs[7].append(nk_s.reshape(kvs_shape))
        outs[8].append(nv_s.reshape(kvs_shape))
        outs[9].append(sc_s.transpose(1, 0, 2))
        outs[10].append(zs[DEC_BATCH:, OFF_XL:OFF_QM].reshape(LRU_CONV_K - 1, DEC_BATCH, LRU_WIDTH
                                                              ).transpose(1, 0, 2))
        outs[11].append(h_s)

    y_prompt = x[:ROWS_P].reshape(BATCH, SEQ, D_MODEL)
    y_sample = to_b_major(x[ROWS_P:])
    return (y_prompt, y_sample) + tuple(jnp.stack(o) for o in outs)
```
